```python
import math
import jax, jax.numpy as jnp
from jax import lax
import numpy as np

D_MODEL = 1024
BATCH = 4
SEQ = 8192
DEPTH = 2

HEAD_DIM = 64
SWA_HEADS = 6
SWA_KV_HEADS = 2
SWA_GROUP = SWA_HEADS // SWA_KV_HEADS
SWA_WINDOW = 128
DIL_WINDOWS = (128, 512, 2048)
DIL_RATES = (1, 4, 16)
DIL_GROUPS = 3
DIL_HEADS_PER_GROUP = 2
DIL_HEADS = DIL_GROUPS * DIL_HEADS_PER_GROUP
MEM_HEADS = 4
MEM_LEN = 256
N_BUCKETS = 32
MAX_DISTANCE = 2048
SELF_HEADS = SWA_HEADS + DIL_HEADS
FFN_DIM = 2816
BLOCK = 128
N_BRANCH = 3
N_SANDWICH = 6
EPS = 1e-6

A_Q = SWA_HEADS * HEAD_DIM
A_KV = SWA_KV_HEADS * HEAD_DIM
B_W = DIL_HEADS * HEAD_DIM
C_W = MEM_HEADS * HEAD_DIM
P_IN = A_Q + 2 * A_KV + 3 * B_W + C_W

kernel_name = "hybrid_swa_dilated_memory_macaron_block"


def rms_norm(x, gain):
    x32 = x.astype(jnp.float32)
    y = x32 * lax.rsqrt(jnp.mean(x32 * x32, axis=-1, keepdims=True) + EPS)
    return (y * gain.astype(jnp.float32)).astype(x.dtype)


def swiglu(x, w_in, w_out):
    a, b = jnp.split(x @ w_in, 2, axis=-1)
    return (jax.nn.silu(a) * b) @ w_out


def t5_bucket(dist):
    max_exact = N_BUCKETS // 2
    d = jnp.maximum(dist, 1).astype(jnp.float32)
    large = max_exact + (jnp.log(d / max_exact) / math.log(MAX_DISTANCE / max_exact)
                         * (N_BUCKETS - max_exact)).astype(jnp.int32)
    large = jnp.minimum(large, N_BUCKETS - 1)
    return jnp.where(dist < max_exact, dist, large)


def band_distance():
    row = jnp.arange(BLOCK)[:, None]
    col = jnp.arange(2 * BLOCK)[None, :]
    return jnp.maximum(row + BLOCK - col, 0)


def band_mask(n_blocks, max_dist):
    row = jnp.arange(BLOCK)[:, None]
    col = jnp.arange(2 * BLOCK)[None, :]
    dist = row + BLOCK - col
    key_pos = jnp.arange(n_blocks)[:, None, None] * BLOCK - BLOCK + col[None]
    return ((dist >= 0) & (dist <= max_dist))[None] & (key_pos >= 0)


def key_windows(t):
    n, l = t.shape[:2]
    nb = l // BLOCK
    tp = jnp.pad(t, ((0, 0), (BLOCK, 0), (0, 0), (0, 0))).reshape(n, nb + 1, BLOCK, *t.shape[2:])
    return jnp.concatenate([tp[:, :-1], tp[:, 1:]], axis=2)


def banded_attention(q, k, v, bias, max_dist, sink):
    n, l, hk, g, hd = q.shape
    nb = l // BLOCK
    qb = q.reshape(n, nb, BLOCK, hk, g, hd)
    kw = key_windows(k)
    vw = key_windows(v)
    s = jnp.einsum('nbqhgd,nbkhd->nbhgqk', qb, kw,
                   preferred_element_type=jnp.float32) * (hd ** -0.5) + bias
    mask = band_mask(nb, max_dist)[None, :, None, None]
    s = jnp.where(mask, s, -jnp.inf)
    m = jnp.max(s, axis=-1)
    if sink is not None:
        sink32 = sink.astype(jnp.float32)[None, None, :, :, None]
        m = jnp.maximum(m, sink32)
    p = jnp.exp(s - m[..., None])
    denom = jnp.sum(p, axis=-1)
    if sink is not None:
        denom = denom + jnp.exp(sink32 - m)
    o = jnp.einsum('nbhgqk,nbkhd->nbqhgd', p, vw.astype(jnp.float32))
    o = o / jnp.moveaxis(denom, -1, 2)[..., None]
    lse = jnp.moveaxis(m + jnp.log(denom), -1, 2)
    return o.reshape(n, l, hk, g, hd), lse.reshape(n, l, hk, g)


def dilated_group(q, k, v, rate, window, bias):
    b, s, h, hd = q.shape
    l = s // rate
    lp = -(-l // BLOCK) * BLOCK

    def to_sub(t):
        t = t.reshape(b, l, rate, h, hd).transpose(0, 2, 1, 3, 4).reshape(b * rate, l, h, hd)
        return jnp.pad(t, ((0, 0), (0, lp - l), (0, 0), (0, 0)))

    o, lse = banded_attention(to_sub(q)[:, :, :, None], to_sub(k), to_sub(v),
                              bias, window // rate, None)
    o = o[:, :l, :, 0].reshape(b, rate, l, h, hd).transpose(0, 2, 1, 3, 4).reshape(b, s, h, hd)
    lse = lse[:, :l, :, 0].reshape(b, rate, l, h).transpose(0, 2, 1, 3).reshape(b, s, h)
    return o, lse


def mixing_sublayer(h, mem, bias_a, bias_b, mem_gain, w_in, sinks, w_mem_kv,
                    w_gate, b_gate, w_br_a, w_br_b, w_br_c, w_o):
    b, s, _ = h.shape
    z = h @ w_in
    qa, ka, va, qb, kb, vb, qc = jnp.split(
        z, np.cumsum([A_Q, A_KV, A_KV, B_W, B_W, B_W])[:].tolist(), axis=-1)

    oa, _ = banded_attention(
        qa.reshape(b, s, SWA_KV_HEADS, SWA_GROUP, HEAD_DIM),
        ka.reshape(b, s, SWA_KV_HEADS, HEAD_DIM),
        va.reshape(b, s, SWA_KV_HEADS, HEAD_DIM),
        bias_a, SWA_WINDOW - 1, sinks.reshape(SWA_KV_HEADS, SWA_GROUP))
    oa = oa.reshape(b, s, A_Q).astype(h.dtype)

    qb = qb.reshape(b, s, DIL_GROUPS, DIL_HEADS_PER_GROUP, HEAD_DIM)
    kb = kb.reshape(b, s, DIL_GROUPS, DIL_HEADS_PER_GROUP, HEAD_DIM)
    vb = vb.reshape(b, s, DIL_GROUPS, DIL_HEADS_PER_GROUP, HEAD_DIM)
    outs, lses = [], []
    for gi in range(DIL_GROUPS):
        o_g, l_g = dilated_group(qb[:, :, gi], kb[:, :, gi], vb[:, :, gi],
                                 DIL_RATES[gi], DIL_WINDOWS[gi], bias_b[gi])
        outs.append(o_g)
        lses.append(l_g)
    o_stack = jnp.stack(outs, axis=2)
    alpha = jax.nn.softmax(jnp.stack(lses, axis=2), axis=2)
    ob = (o_stack * alpha[..., None]).reshape(b, s, B_W).astype(h.dtype)

    mh = rms_norm(mem, mem_gain)
    kc, vc = jnp.split(mh @ w_mem_kv, 2, axis=-1)
    kc = kc.reshape(b, MEM_LEN, MEM_HEADS, HEAD_DIM)
    vc = vc.reshape(b, MEM_LEN, MEM_HEADS, HEAD_DIM)
    qc = qc.reshape(b, s, MEM_HEADS, HEAD_DIM)
    sc = jnp.einsum('bshd,bmhd->bhsm', qc, kc,
                    preferred_element_type=jnp.float32) * (HEAD_DIM ** -0.5)
    pc = jax.nn.softmax(sc, axis=-1)
    oc = jnp.einsum('bhsm,bmhd->bshd', pc, vc.astype(jnp.float32))
    oc = oc.reshape(b, s, C_W).astype(h.dtype)

    gates = jax.nn.sigmoid(h @ w_gate + b_gate.reshape(-1)).reshape(b, s, N_BRANCH, D_MODEL)
    merged = (gates[:, :, 0] * (oa @ w_br_a)
              + gates[:, :, 1] * (ob @ w_br_b)
              + gates[:, :, 2] * (oc @ w_br_c))
    return merged @ w_o


def setup_inputs(seed: int = 0) -> dict:
    key = jax.random.key(seed)
    ks = jax.random.split(key, 20)
    f32 = jnp.float32

    def nrm(k, shape, fan_in):
        return jax.random.normal(k, shape, f32) * (fan_in ** -0.5)

    return {
        "x": jax.random.normal(ks[0], (BATCH, SEQ, D_MODEL), f32),
        "mem": jax.random.normal(ks[1], (BATCH, MEM_LEN, D_MODEL), f32),
        "rel_bias": 0.5 * jax.random.normal(ks[2], (N_BUCKETS, SELF_HEADS), f32),
        "norm_gain": 1.0 + 0.05 * jax.random.normal(ks[3], (DEPTH, N_SANDWICH, D_MODEL), f32),
        "mem_norm_gain": 1.0 + 0.05 * jax.random.normal(ks[4], (DEPTH, D_MODEL), f32),
        "w_ffn1_in": nrm(ks[5], (DEPTH, D_MODEL, 2 * FFN_DIM), D_MODEL),
        "w_ffn1_out": nrm(ks[6], (DEPTH, FFN_DIM, D_MODEL), FFN_DIM),
        "w_in": nrm(ks[7], (DEPTH, D_MODEL, P_IN), D_MODEL),
        "sinks": 0.5 * jax.random.normal(ks[8], (DEPTH, SWA_HEADS), f32),
        "w_mem_kv": nrm(ks[9], (DEPTH, D_MODEL, 2 * C_W), D_MODEL),
        "w_gate": nrm(ks[10], (DEPTH, D_MODEL, N_BRANCH * D_MODEL), D_MODEL),
        "b_gate": 0.01 * jax.random.normal(ks[11], (DEPTH, N_BRANCH, D_MODEL), f32),
        "w_br_a": nrm(ks[12], (DEPTH, A_Q, D_MODEL), A_Q),
        "w_br_b": nrm(ks[13], (DEPTH, B_W, D_MODEL), B_W),
        "w_br_c": nrm(ks[14], (DEPTH, C_W, D_MODEL), C_W),
        "w_o": nrm(ks[15], (DEPTH, D_MODEL, D_MODEL), D_MODEL),
        "w_ffn2_in": nrm(ks[16], (DEPTH, D_MODEL, 2 * FFN_DIM), D_MODEL),
        "w_ffn2_out": nrm(ks[17], (DEPTH, FFN_DIM, D_MODEL), FFN_DIM),
    }


def reference(x, mem, rel_bias, norm_gain, mem_norm_gain, w_ffn1_in, w_ffn1_out,
              w_in, sinks, w_mem_kv, w_gate, b_gate, w_br_a, w_br_b, w_br_c, w_o,
              w_ffn2_in, w_ffn2_out):
    table = rel_bias.astype(jnp.float32)
    dist = band_distance()
    bias_a = table[t5_bucket(dist)][..., :SWA_HEADS].transpose(2, 0, 1)
    bias_a = bias_a.reshape(SWA_KV_HEADS, SWA_GROUP, BLOCK, 2 * BLOCK)
    bias_b = []
    for gi in range(DIL_GROUPS):
        h0 = SWA_HEADS + gi * DIL_HEADS_PER_GROUP
        bg = table[t5_bucket(dist * DIL_RATES[gi])][..., h0:h0 + DIL_HEADS_PER_GROUP]
        bias_b.append(bg.transpose(2, 0, 1)[:, None])

    for l in range(DEPTH):
        g = norm_gain[l]
        x = x + 0.5 * rms_norm(swiglu(rms_norm(x, g[0]), w_ffn1_in[l], w_ffn1_out[l]), g[1])
        y = mixing_sublayer(rms_norm(x, g[2]), mem, bias_a, bias_b, mem_norm_gain[l],
                            w_in[l], sinks[l], w_mem_kv[l], w_gate[l], b_gate[l],
                            w_br_a[l], w_br_b[l], w_br_c[l], w_o[l])
        x = x + rms_norm(y, g[3])
        x = x + 0.5 * rms_norm(swiglu(rms_norm(x, g[4]), w_ffn2_in[l], w_ffn2_out[l]), g[5])
    return x
```

```python
import functools
import math

import jax
import jax.numpy as jnp
from jax import lax
from jax.experimental import pallas as pl
from jax.experimental.pallas import tpu as pltpu

HEAD_DIM = 64
SWA_HEADS = 6
SWA_KV_HEADS = 2
SWA_GROUP = SWA_HEADS // SWA_KV_HEADS
SWA_WINDOW = 128
DIL_WINDOWS = (128, 512, 2048)
DIL_RATES = (1, 4, 16)
DIL_GROUPS = 3
DIL_HEADS_PER_GROUP = 2
MEM_HEADS = 4
N_BUCKETS = 32
MAX_DISTANCE = 2048
N_BRANCH = 3
EPS = 1e-6
BLOCK = 128

A_Q = SWA_HEADS * HEAD_DIM
A_KV = SWA_KV_HEADS * HEAD_DIM
B_G = DIL_HEADS_PER_GROUP * HEAD_DIM
B_W = DIL_GROUPS * B_G
C_W = MEM_HEADS * HEAD_DIM
A_COLS = A_Q + 2 * A_KV
B_SLABS = 3 * DIL_GROUPS

Q_SCALE = HEAD_DIM ** -0.5
MASKED = -1e30

ROW_TILE = 512
FFN_CHUNK = 512
AC_TILE = 512
DIL_TILE = BLOCK * max(DIL_RATES)
VMEM_LIMIT = 56 * 1024 * 1024

_F32 = jnp.float32
_BF16 = jnp.bfloat16


def _dot(a, b):
    return jnp.dot(a, b, preferred_element_type=_F32)


def _dot_nt(a, b):
    return lax.dot_general(a, b, (((1,), (1,)), ((), ())),
                           preferred_element_type=_F32)


def _rms(x, gain):
    ms = jnp.mean(x * x, axis=-1, keepdims=True)
    return x * lax.rsqrt(ms + EPS) * gain


def _resident(shape):
    return pl.BlockSpec(shape, lambda *_: (0,) * len(shape),
                        pipeline_mode=pl.Buffered(1))


def _params(n_axes):
    return pltpu.CompilerParams(
        dimension_semantics=("arbitrary",) * n_axes,
        vmem_limit_bytes=VMEM_LIMIT)


def _ffn_kernel(x_ref, g_ref, w_in_ref, w_out_ref, o_ref, *, ffn_dim):
    x = x_ref[...]
    h = _rms(x, g_ref[0:1, :]).astype(_BF16)
    y = None
    for c0 in range(0, ffn_dim, FFN_CHUNK):
        c1 = min(c0 + FFN_CHUNK, ffn_dim)
        a = _dot(h, w_in_ref[:, c0:c1])
        b = _dot(h, w_in_ref[:, ffn_dim + c0:ffn_dim + c1])
        act = (a * jax.nn.sigmoid(a) * b).astype(_BF16)
        part = _dot(act, w_out_ref[c0:c1, :])
        y = part if y is None else y + part
    o_ref[...] = x + 0.5 * _rms(y, g_ref[1:2, :])


def _ffn(x2d, gains, w_in, w_out):
    t, d = x2d.shape
    ffn_dim = w_out.shape[0]
    return pl.pallas_call(
        functools.partial(_ffn_kernel, ffn_dim=ffn_dim),
        grid=(t // ROW_TILE,),
        in_specs=[
            pl.BlockSpec((ROW_TILE, d), lambda i: (i, 0)),
            _resident(gains.shape),
            _resident(w_in.shape),
            _resident(w_out.shape),
        ],
        out_specs=pl.BlockSpec((ROW_TILE, d), lambda i: (i, 0)),
        out_shape=jax.ShapeDtypeStruct((t, d), _F32),
        compiler_params=_params(1),
        name="ffn",
    )(x2d, gains, w_in, w_out)


def _pre_kernel(x_ref, g_ref, w_ref, a_ref, c_ref, b_ref):
    h = _rms(x_ref[...], g_ref[...]).astype(_BF16)
    a_ref[...] = _dot(h, w_ref[:, 0:A_COLS]).astype(_BF16)
    c_ref[...] = _dot(h, w_ref[:, A_COLS:A_COLS + C_W]).astype(_BF16)
    zb = _dot(h, w_ref[:, A_COLS + C_W:])
    for n in range(B_SLABS):
        b_ref[n] = zb[:, n * B_G:(n + 1) * B_G]


def _pre(x2d, gain, w_perm):
    t, d = x2d.shape
    row = lambda i: (i, 0)
    return pl.pallas_call(
        _pre_kernel,
        grid=(t // ROW_TILE,),
        in_specs=[
            pl.BlockSpec((ROW_TILE, d), row),
            _resident(gain.shape),
            _resident(w_perm.shape),
        ],
        out_specs=[
            pl.BlockSpec((ROW_TILE, A_COLS), row),
            pl.BlockSpec((ROW_TILE, C_W), row),
            pl.BlockSpec((B_SLABS, ROW_TILE, B_G), lambda i: (0, i, 0)),
        ],
        out_shape=[
            jax.ShapeDtypeStruct((t, A_COLS), _BF16),
            jax.ShapeDtypeStruct((t, C_W), _BF16),
            jax.ShapeDtypeStruct((B_SLABS, t, B_G), _F32),
        ],
        compiler_params=_params(1),
        name="mix_in_proj",
    )(x2d, gain, w_perm)


def _mem_kernel(m_ref, g_ref, w_ref, o_ref):
    h = _rms(m_ref[...], g_ref[...]).astype(_BF16)
    o_ref[...] = _dot(h, w_ref[...]).astype(_BF16)


def _mem_kv(mem2d, gain, w):
    rows = mem2d.shape[0]
    return pl.pallas_call(
        _mem_kernel,
        grid=(1,),
        in_specs=[_resident(mem2d.shape), _resident(gain.shape), _resident(w.shape)],
        out_specs=pl.BlockSpec((rows, 2 * C_W), lambda i: (0, 0)),
        out_shape=jax.ShapeDtypeStruct((rows, 2 * C_W), _BF16),
        compiler_params=_params(1),
        name="mem_kv",
    )(mem2d, gain, w)


def _softmax_pv(s, v, sink=None):
    m = jnp.max(s, axis=-1, keepdims=True)
    if sink is not None:
        m = jnp.maximum(m, sink)
    p = jnp.exp(s - m)
    denom = jnp.sum(p, axis=-1, keepdims=True)
    if sink is not None:
        denom = denom + jnp.exp(sink - m)
    o = _dot(p.astype(_BF16), v) / denom
    return o, m, denom


def _ac_kernel(sink_ref, cur_ref, prev_ref, qc_ref, kvc_ref, bias0_ref, bias_ref,
               oa_ref, oc_ref):
    hd = HEAD_DIM
    for j in range(AC_TILE // BLOCK):
        rows = slice(j * BLOCK, (j + 1) * BLOCK)
        if j == 0:
            kv_prev = prev_ref[...]
            bias = bias0_ref
        else:
            kv_prev = cur_ref[(j - 1) * BLOCK:j * BLOCK, 0:2 * A_KV]
            bias = bias_ref
        kv_win = jnp.concatenate([kv_prev, cur_ref[rows, 0:2 * A_KV]], axis=0)
        outs = []
        for kh in range(SWA_KV_HEADS):
            k_win = kv_win[:, kh * hd:(kh + 1) * hd]
            v_win = kv_win[:, A_KV + kh * hd:A_KV + (kh + 1) * hd]
            for g in range(SWA_GROUP):
                h = kh * SWA_GROUP + g
                q = cur_ref[rows, 2 * A_KV + h * hd:2 * A_KV + (h + 1) * hd] * Q_SCALE
                s = _dot_nt(q, k_win) + bias[h]
                o, _, _ = _softmax_pv(s, v_win, sink_ref[h])
                outs.append(o)
        oa_ref[rows, :] = jnp.concatenate(outs, axis=-1).astype(_BF16)

        outs = []
        for h in range(MEM_HEADS):
            q = qc_ref[rows, h * hd:(h + 1) * hd] * Q_SCALE
            s = _dot_nt(q, kvc_ref[:, h * hd:(h + 1) * hd])
            o, _, _ = _softmax_pv(s, kvc_ref[:, C_W + h * hd:C_W + (h + 1) * hd])
            outs.append(o)
        oc_ref[rows, :] = jnp.concatenate(outs, axis=-1).astype(_BF16)


def _attn_ac(za, qc, kvc, bias_a, sinks, batch, seq):
    t = za.shape[0]
    tiles = seq // AC_TILE
    blocks_per_tile = AC_TILE // BLOCK
    mem_len = kvc.shape[0] // batch
    row = lambda b, i: (b * tiles + i, 0)
    prev = lambda b, i: (jnp.maximum((b * tiles + i) * blocks_per_tile - 1, 0), 0)
    bias_shape = (None,) + bias_a.shape[1:]
    return pl.pallas_call(
        _ac_kernel,
        grid=(batch, tiles),
        in_specs=[
            pl.BlockSpec(memory_space=pltpu.SMEM),
            pl.BlockSpec((AC_TILE, A_COLS), row),
            pl.BlockSpec((BLOCK, 2 * A_KV), prev),
            pl.BlockSpec((AC_TILE, C_W), row),
            pl.BlockSpec((mem_len, 2 * C_W), lambda b, i: (b, 0)),
            pl.BlockSpec(bias_shape, lambda b, i: (jnp.minimum(i, 1), 0, 0, 0)),
            pl.BlockSpec(bias_shape, lambda b, i: (1, 0, 0, 0)),
        ],
        out_specs=[
            pl.BlockSpec((AC_TILE, A_Q), row),
            pl.BlockSpec((AC_TILE, C_W), row),
        ],
        out_shape=[
            jax.ShapeDtypeStruct((t, A_Q), _BF16),
            jax.ShapeDtypeStruct((t, C_W), _BF16),
        ],
        compiler_params=_params(2),
        name="attn_window_cross",
    )(sinks, za, za, qc, kvc, bias_a, bias_a)


def _dil_kernel(cur_ref, prev_ref, bias0_ref, bias_ref, ob_ref, o_scr, l_scr):
    hd = HEAD_DIM
    for g, rate in enumerate(DIL_RATES):
        span = BLOCK * rate
        n_span = DIL_TILE // span
        q_slab, k_slab, v_slab = 3 * g, 3 * g + 1, 3 * g + 2

        def strided(ref, slab, start):
            if rate == 1:
                return ref[slab, pl.ds(start, BLOCK), :]
            return ref[slab, pl.ds(start, BLOCK, stride=rate), :]

        def window(slab, sb, c):
            cur = strided(cur_ref, slab, sb * span + c)
            if sb == 0:
                prev = strided(prev_ref, slab, (n_span - 1) * span + c)
            else:
                prev = strided(cur_ref, slab, (sb - 1) * span + c)
            return jnp.concatenate([prev, cur], axis=0).astype(_BF16)

        for sb in range(n_span):
            bias = bias0_ref if sb == 0 else bias_ref
            for c in range(rate):
                start = sb * span + c
                q = (strided(cur_ref, q_slab, start) * Q_SCALE).astype(_BF16)
                k_pair = window(k_slab, sb, c)
                v_pair = window(v_slab, sb, c)
                outs, lses = [], []
                for h in range(DIL_HEADS_PER_GROUP):
                    k_win = k_pair[:, h * hd:(h + 1) * hd]
                    v_win = v_pair[:, h * hd:(h + 1) * hd]
                    s = _dot_nt(q[:, h * hd:(h + 1) * hd], k_win) + bias[g, h]
                    o, m, denom = _softmax_pv(s, v_win)
                    outs.append(o)
                    lses.append(jnp.broadcast_to(m + jnp.log(denom), (BLOCK, hd)))
                o_pair = jnp.concatenate(outs, axis=-1)
                l_pair = jnp.concatenate(lses, axis=-1)
                if rate == 1:
                    o_scr[g, pl.ds(start, BLOCK), :] = o_pair
                    l_scr[g, pl.ds(start, BLOCK), :] = l_pair
                else:
                    o_scr[g, pl.ds(start, BLOCK, stride=rate), :] = o_pair
                    l_scr[g, pl.ds(start, BLOCK, stride=rate), :] = l_pair

    lse = [l_scr[g] for g in range(DIL_GROUPS)]
    top = jnp.maximum(jnp.maximum(lse[0], lse[1]), lse[2])
    w = [jnp.exp(l - top) for l in lse]
    total = w[0] + w[1] + w[2]
    for g in range(DIL_GROUPS):
        ob_ref[:, g * B_G:(g + 1) * B_G] = (o_scr[g] * (w[g] / total)).astype(_BF16)


def _attn_dil(zb, bias_b, batch, seq):
    t = zb.shape[1]
    tiles = seq // DIL_TILE
    row = lambda b, i: (b * tiles + i, 0)
    bias_shape = (None,) + bias_b.shape[1:]
    return pl.pallas_call(
        _dil_kernel,
        grid=(batch, tiles),
        in_specs=[
            pl.BlockSpec((B_SLABS, DIL_TILE, B_G), lambda b, i: (0, b * tiles + i, 0)),
            pl.BlockSpec((B_SLABS, DIL_TILE, B_G),
                         lambda b, i: (0, jnp.maximum(b * tiles + i - 1, 0), 0)),
            pl.BlockSpec(bias_shape, lambda b, i: (jnp.minimum(i, 1), 0, 0, 0, 0)),
            pl.BlockSpec(bias_shape, lambda b, i: (1, 0, 0, 0, 0)),
        ],
        out_specs=pl.BlockSpec((DIL_TILE, B_W), row),
        out_shape=jax.ShapeDtypeStruct((t, B_W), _BF16),
        scratch_shapes=[
            pltpu.VMEM((DIL_GROUPS, DIL_TILE, B_G), _F32),
            pltpu.VMEM((DIL_GROUPS, DIL_TILE, B_G), _F32),
        ],
        compiler_params=_params(2),
        name="attn_dilated",
    )(zb, zb, bias_b, bias_b)


def _post_kernel(x_ref, oa_ref, ob_ref, oc_ref, g_ref, wg_ref, bg_ref,
                 wa_ref, wb_ref, wc_ref, wo_ref, o_ref):
    x = x_ref[...]
    d = x.shape[-1]
    h = _rms(x, g_ref[0:1, :]).astype(_BF16)
    branches = (_dot(oa_ref[...], wa_ref[...]),
                _dot(ob_ref[...], wb_ref[...]),
                _dot(oc_ref[...], wc_ref[...]))
    merged = None
    for n, br in enumerate(branches):
        gate = jax.nn.sigmoid(_dot(h, wg_ref[:, n * d:(n + 1) * d])
                              + bg_ref[:, n * d:(n + 1) * d])
        merged = gate * br if merged is None else merged + gate * br
    y = _dot(merged.astype(_BF16), wo_ref[...])
    o_ref[...] = x + _rms(y, g_ref[1:2, :])


def _post(x2d, oa, ob, oc, gains, w_gate, b_gate, w_a, w_b, w_c, w_o):
    t, d = x2d.shape
    row = lambda i: (i, 0)
    return pl.pallas_call(
        _post_kernel,
        grid=(t // ROW_TILE,),
        in_specs=[
            pl.BlockSpec((ROW_TILE, d), row),
            pl.BlockSpec((ROW_TILE, A_Q), row),
            pl.BlockSpec((ROW_TILE, B_W), row),
            pl.BlockSpec((ROW_TILE, C_W), row),
            _resident(gains.shape),
            _resident(w_gate.shape),
            _resident(b_gate.shape),
            _resident(w_a.shape),
            _resident(w_b.shape),
            _resident(w_c.shape),
            _resident(w_o.shape),
        ],
        out_specs=pl.BlockSpec((ROW_TILE, d), row),
        out_shape=jax.ShapeDtypeStruct((t, d), _F32),
        compiler_params=_params(1),
        name="mix_out_proj",
    )(x2d, oa, ob, oc, gains, w_gate, b_gate, w_a, w_b, w_c, w_o)


def _t5_bucket(dist):
    max_exact = N_BUCKETS // 2
    d = jnp.maximum(dist, 1).astype(_F32)
    large = max_exact + (jnp.log(d / max_exact) / math.log(MAX_DISTANCE / max_exact)
                         * (N_BUCKETS - max_exact)).astype(jnp.int32)
    large = jnp.minimum(large, N_BUCKETS - 1)
    return jnp.where(dist < max_exact, dist, large)


def _band_bias(table, heads, rate, max_dist):
    row = jnp.arange(BLOCK)[:, None]
    col = jnp.arange(2 * BLOCK)[None, :]
    dist = row + BLOCK - col
    bias = table[_t5_bucket(jnp.maximum(dist, 0) * rate)][..., heads].transpose(2, 0, 1)
    valid = (dist >= 0) & (dist <= max_dist)
    later = jnp.where(valid[None], bias, MASKED)
    first = jnp.where((valid & (col >= BLOCK))[None], bias, MASKED)
    return jnp.stack([first, later])


def _permute_in_proj(w):
    qa, ka, va = w[:, 0:A_Q], w[:, A_Q:A_Q + A_KV], w[:, A_Q + A_KV:A_Q + 2 * A_KV]
    off = A_Q + 2 * A_KV
    qb, kb, vb = (w[:, off + n * B_W:off + (n + 1) * B_W] for n in range(3))
    qc = w[:, off + 3 * B_W:]
    cols = [ka, va, qa, qc]
    for g in range(DIL_GROUPS):
        sl = slice(g * B_G, (g + 1) * B_G)
        cols += [qb[:, sl], kb[:, sl], vb[:, sl]]
    return jnp.concatenate(cols, axis=1)


def kernel(x, mem, rel_bias, norm_gain, mem_norm_gain, w_ffn1_in, w_ffn1_out, w_in,
           sinks, w_mem_kv, w_gate, b_gate, w_br_a, w_br_b, w_br_c, w_o,
           w_ffn2_in, w_ffn2_out):
    batch, seq, d = x.shape
    depth = norm_gain.shape[0]
    assert seq % DIL_TILE == 0 and (batch * seq) % ROW_TILE == 0

    table = rel_bias.astype(_F32)
    bias_a = _band_bias(table, slice(0, SWA_HEADS), 1, SWA_WINDOW - 1)
    bias_b = jnp.stack([
        _band_bias(table,
                   slice(SWA_HEADS + g * DIL_HEADS_PER_GROUP,
                         SWA_HEADS + (g + 1) * DIL_HEADS_PER_GROUP),
                   DIL_RATES[g], DIL_WINDOWS[g] // DIL_RATES[g])
        for g in range(DIL_GROUPS)], axis=1)

    bf = lambda w: w.astype(_BF16)
    x2d = x.reshape(batch * seq, d)
    mem2d = mem.reshape(-1, d)
    for l in range(depth):
        g = norm_gain[l]
        x2d = _ffn(x2d, g[0:2], bf(w_ffn1_in[l]), bf(w_ffn1_out[l]))
        za, qc, zb = _pre(x2d, g[2:3], bf(_permute_in_proj(w_in[l])))
        kvc = _mem_kv(mem2d, mem_norm_gain[l][None], bf(w_mem_kv[l]))
        oa, oc = _attn_ac(za, qc, kvc, bias_a, sinks[l], batch, seq)
        ob = _attn_dil(zb, bias_b, batch, seq)
        x2d = _post(x2d, oa, ob, oc, g[2:4], bf(w_gate[l]), b_gate[l].reshape(1, -1),
                    bf(w_br_a[l]), bf(w_br_b[l]), bf(w_br_c[l]), bf(w_o[l]))
        x2d = _ffn(x2d, g[4:6], bf(w_ffn2_in[l]), bf(w_ffn2_out[l]))
    return x2d.reshape(batch, seq, d)
```

```python
import functools
import math

import jax
import jax.numpy as jnp
from jax import lax
from jax.experimental import pallas as pl
from jax.experimental.pallas import tpu as pltpu

HEAD_DIM = 64
SWA_HEADS = 6
SWA_KV_HEADS = 2
SWA_GROUP = SWA_HEADS // SWA_KV_HEADS
SWA_WINDOW = 128
DIL_WINDOWS = (128, 512, 2048)
DIL_RATES = (1, 4, 16)
DIL_GROUPS = 3
DIL_HEADS_PER_GROUP = 2
MEM_HEADS = 4
N_BUCKETS = 32
MAX_DISTANCE = 2048
N_BRANCH = 3
EPS = 1e-6
BLOCK = 128
LANES = 128

A_Q = SWA_HEADS * HEAD_DIM
A_KV = SWA_KV_HEADS * HEAD_DIM
B_G = DIL_HEADS_PER_GROUP * HEAD_DIM
B_W = DIL_GROUPS * B_G
C_W = MEM_HEADS * HEAD_DIM
A_COLS = A_Q + 2 * A_KV
BD_SLABS = 3 * (DIL_GROUPS - 1)
BD_K = {1: 2, 2: 0}
BD_Q = {1: 4, 2: 5}
A_PAIR_ORDER = tuple(h for p in range(SWA_GROUP) for h in (p, p + SWA_GROUP))

Q_SCALE = HEAD_DIM ** -0.5
MASKED = -1e30

ROW_TILE = 512
FFN_CHUNK = 512
AC_TILE = 512
DIL_TILE = BLOCK * max(DIL_RATES)
VMEM_LIMIT = 56 * 1024 * 1024

_F32 = jnp.float32
_BF16 = jnp.bfloat16


def _dot(a, b):
    return jnp.dot(a, b, preferred_element_type=_F32)


def _dot_nt(a, b):
    return lax.dot_general(a, b, (((1,), (1,)), ((), ())),
                           preferred_element_type=_F32)


def _rms(x, gain):
    ms = jnp.mean(x * x, axis=-1, keepdims=True)
    return x * lax.rsqrt(ms + EPS) * gain


def _resident(shape):
    return pl.BlockSpec(shape, lambda *_: (0,) * len(shape),
                        pipeline_mode=pl.Buffered(1))


def _layer_resident(shape, layer):
    zeros = (0,) * (len(shape) - 1)
    return pl.BlockSpec((None,) + tuple(shape[1:]), lambda *_: (layer,) + zeros,
                        pipeline_mode=pl.Buffered(1))


def _params(n_axes):
    return pltpu.CompilerParams(
        dimension_semantics=("arbitrary",) * n_axes,
        vmem_limit_bytes=VMEM_LIMIT)


def _ffn_kernel(x_ref, g_ref, w_in_ref, w_out_ref, o_ref, *, ffn_dim, g0):
    x = x_ref[...]
    h = _rms(x, g_ref[g0:g0 + 1, :]).astype(_BF16)
    y = None
    for c0 in range(0, ffn_dim, FFN_CHUNK):
        c1 = min(c0 + FFN_CHUNK, ffn_dim)
        a = _dot(h, w_in_ref[:, c0:c1])
        b = _dot(h, w_in_ref[:, ffn_dim + c0:ffn_dim + c1])
        act = (a * jax.nn.sigmoid(a) * b).astype(_BF16)
        part = _dot(act, w_out_ref[c0:c1, :])
        y = part if y is None else y + part
    o_ref[...] = x + 0.5 * _rms(y, g_ref[g0 + 1:g0 + 2, :])


def _ffn(x2d, gains, w_in, w_out, layer, g0):
    t, d = x2d.shape
    ffn_dim = w_out.shape[1]
    return pl.pallas_call(
        functools.partial(_ffn_kernel, ffn_dim=ffn_dim, g0=g0),
        grid=(t // ROW_TILE,),
        in_specs=[
            pl.BlockSpec((ROW_TILE, d), lambda i: (i, 0)),
            _layer_resident(gains.shape, layer),
            _layer_resident(w_in.shape, layer),
            _layer_resident(w_out.shape, layer),
        ],
        out_specs=pl.BlockSpec((ROW_TILE, d), lambda i: (i, 0)),
        out_shape=jax.ShapeDtypeStruct((t, d), _F32),
        compiler_params=_params(1),
        name="ffn",
    )(x2d, gains, w_in, w_out)


def _pre_kernel(x_ref, g_ref, w_ref, a_ref, c_ref, b0_ref, bd_ref):
    h = _rms(x_ref[...], g_ref[2:3, :]).astype(_BF16)
    col = 0
    for ref in (a_ref, c_ref, b0_ref):
        width = ref.shape[-1]
        ref[...] = _dot(h, w_ref[:, col:col + width]).astype(_BF16)
        col += width
    zd = _dot(h, w_ref[:, col:])
    for n in range(BD_SLABS):
        bd_ref[n] = zd[:, n * B_G:(n + 1) * B_G]


def _pre(x2d, gains, w_perm, layer):
    t, d = x2d.shape
    row = lambda i: (i, 0)
    return pl.pallas_call(
        _pre_kernel,
        grid=(t // ROW_TILE,),
        in_specs=[
            pl.BlockSpec((ROW_TILE, d), row),
            _layer_resident(gains.shape, layer),
            _layer_resident(w_perm.shape, layer),
        ],
        out_specs=[
            pl.BlockSpec((ROW_TILE, A_COLS), row),
            pl.BlockSpec((ROW_TILE, C_W), row),
            pl.BlockSpec((ROW_TILE, 3 * B_G), row),
            pl.BlockSpec((BD_SLABS, ROW_TILE, B_G), lambda i: (0, i, 0)),
        ],
        out_shape=[
            jax.ShapeDtypeStruct((t, A_COLS), _BF16),
            jax.ShapeDtypeStruct((t, C_W), _BF16),
            jax.ShapeDtypeStruct((t, 3 * B_G), _BF16),
            jax.ShapeDtypeStruct((BD_SLABS, t, B_G), _F32),
        ],
        compiler_params=_params(1),
        name="mix_in_proj",
    )(x2d, gains, w_perm)


def _mem_kernel(m_ref, g_ref, w_ref, o_ref):
    h = _rms(m_ref[...], g_ref[...]).astype(_BF16)
    o_ref[...] = _dot(h, w_ref[...]).astype(_BF16)


def _mem_kv(mem2d, gains, w, layer):
    rows = mem2d.shape[0]
    return pl.pallas_call(
        _mem_kernel,
        grid=(1,),
        in_specs=[_resident(mem2d.shape),
                  _layer_resident(gains.shape, layer),
                  _layer_resident(w.shape, layer)],
        out_specs=pl.BlockSpec((rows, 2 * C_W), lambda i: (0, 0)),
        out_shape=jax.ShapeDtypeStruct((rows, 2 * C_W), _BF16),
        compiler_params=_params(1),
        name="mem_kv",
    )(mem2d, gains, w)


def _low_lanes():
    return lax.broadcasted_iota(jnp.int32, (1, LANES), 1) < HEAD_DIM


def _pair_attention(q, k, v, bias_lo=None, bias_hi=None, sink_lo=None, sink_hi=None):
    low = _low_lanes()
    zero = jnp.zeros((), k.dtype)
    one = jnp.ones((), v.dtype)
    s_lo = _dot_nt(q, jnp.where(low, k, zero))
    s_hi = _dot_nt(q, jnp.where(low, zero, k))
    if bias_lo is not None:
        s_lo = s_lo + bias_lo
        s_hi = s_hi + bias_hi
    m_lo = jnp.max(s_lo, axis=-1, keepdims=True)
    m_hi = jnp.max(s_hi, axis=-1, keepdims=True)
    if sink_lo is not None:
        m_lo = jnp.maximum(m_lo, sink_lo)
        m_hi = jnp.maximum(m_hi, sink_hi)
    p_lo = jnp.exp(s_lo - m_lo).astype(_BF16)
    p_hi = jnp.exp(s_hi - m_hi).astype(_BF16)
    o_lo = _dot(p_lo, jnp.where(low, v, one))
    o_hi = _dot(p_hi, jnp.where(low, one, v))
    out = jnp.where(low, o_lo, o_hi)
    den = jnp.where(low, o_hi, o_lo)
    if sink_lo is not None:
        den = den + jnp.where(low, jnp.exp(sink_hi - m_hi), jnp.exp(sink_lo - m_lo))
    den = pltpu.roll(den, HEAD_DIM, axis=1)
    return out, den, m_lo, m_hi


def _ac_kernel(sink_ref, cur_ref, prev_ref, qc_ref, kvc_ref, bias0_ref, bias_ref,
               oa_ref, oc_ref):
    rows_a = SWA_GROUP * BLOCK
    sink_lo = jnp.concatenate(
        [jnp.full((BLOCK, 1), sink_ref[h], _F32) for h in range(SWA_GROUP)], axis=0)
    sink_hi = jnp.concatenate(
        [jnp.full((BLOCK, 1), sink_ref[SWA_GROUP + h], _F32) for h in range(SWA_GROUP)],
        axis=0)
    for j in range(AC_TILE // BLOCK):
        rows = slice(j * BLOCK, (j + 1) * BLOCK)
        if j == 0:
            kv_prev = prev_ref[...]
            bias = bias0_ref
        else:
            kv_prev = cur_ref[(j - 1) * BLOCK:j * BLOCK, 0:2 * A_KV]
            bias = bias_ref
        kv_win = jnp.concatenate([kv_prev, cur_ref[rows, 0:2 * A_KV]], axis=0)
        q = jnp.concatenate(
            [cur_ref[rows, 2 * A_KV + p * LANES:2 * A_KV + (p + 1) * LANES]
             for p in range(SWA_GROUP)], axis=0) * Q_SCALE
        out, den, _, _ = _pair_attention(
            q, kv_win[:, 0:A_KV], kv_win[:, A_KV:2 * A_KV],
            bias[0:SWA_GROUP].reshape(rows_a, 2 * BLOCK),
            bias[SWA_GROUP:SWA_HEADS].reshape(rows_a, 2 * BLOCK),
            sink_lo, sink_hi)
        o = out / den
        for p in range(SWA_GROUP):
            oa_ref[rows, p * LANES:(p + 1) * LANES] = (
                o[p * BLOCK:(p + 1) * BLOCK]).astype(_BF16)

    for p in range(MEM_HEADS // 2):
        cols = slice(p * LANES, (p + 1) * LANES)
        out, den, _, _ = _pair_attention(
            qc_ref[:, cols] * Q_SCALE,
            kvc_ref[:, p * LANES:(p + 1) * LANES],
            kvc_ref[:, C_W + p * LANES:C_W + (p + 1) * LANES])
        oc_ref[:, cols] = (out / den).astype(_BF16)


def _attn_ac(za, qc, kvc, bias_a, sinks, batch, seq):
    t = za.shape[0]
    tiles = seq // AC_TILE
    blocks_per_tile = AC_TILE // BLOCK
    mem_len = kvc.shape[0] // batch
    row = lambda b, i: (b * tiles + i, 0)
    prev = lambda b, i: (jnp.maximum((b * tiles + i) * blocks_per_tile - 1, 0), 0)
    bias_shape = (None,) + bias_a.shape[1:]
    return pl.pallas_call(
        _ac_kernel,
        grid=(batch, tiles),
        in_specs=[
            pl.BlockSpec(memory_space=pltpu.SMEM),
            pl.BlockSpec((AC_TILE, A_COLS), row),
            pl.BlockSpec((BLOCK, 2 * A_KV), prev),
            pl.BlockSpec((AC_TILE, C_W), row),
            pl.BlockSpec((mem_len, 2 * C_W), lambda b, i: (b, 0)),
            pl.BlockSpec(bias_shape, lambda b, i: (jnp.minimum(i, 1), 0, 0, 0)),
            pl.BlockSpec(bias_shape, lambda b, i: (1, 0, 0, 0)),
        ],
        out_specs=[
            pl.BlockSpec((AC_TILE, A_Q), row),
            pl.BlockSpec((AC_TILE, C_W), row),
        ],
        out_shape=[
            jax.ShapeDtypeStruct((t, A_Q), _BF16),
            jax.ShapeDtypeStruct((t, C_W), _BF16),
        ],
        compiler_params=_params(2),
        name="attn_window_cross",
    )(sinks, za, za, qc, kvc, bias_a, bias_a)


def _dil_kernel(b0_ref, b0_prev_ref, bd_ref, prev16_ref, prev4_ref, bias0_ref, bias_ref,
                ob_ref, o_scr, l_scr):
    low = _low_lanes()

    def finish(g, rows, out, den, m_lo, m_hi):
        o_scr[g, rows, :] = out / den
        l_scr[g, rows, :] = jnp.where(low, m_lo, m_hi) + jnp.log(den)

    for sb in range(DIL_TILE // BLOCK):
        rows = pl.ds(sb * BLOCK, BLOCK)
        if sb == 0:
            kv_prev, bias = b0_prev_ref[...], bias0_ref
        else:
            kv_prev, bias = b0_ref[pl.ds((sb - 1) * BLOCK, BLOCK), 0:2 * B_G], bias_ref
        kv_win = jnp.concatenate([kv_prev, b0_ref[rows, 0:2 * B_G]], axis=0)
        finish(0, rows, *_pair_attention(
            b0_ref[rows, 2 * B_G:3 * B_G] * Q_SCALE, kv_win[:, 0:B_G], kv_win[:, B_G:2 * B_G],
            bias[0, 0], bias[0, 1]))

    for g, prev_ref in ((1, prev4_ref), (2, prev16_ref)):
        rate = DIL_RATES[g]
        span = BLOCK * rate

        def window(slab, sb, c):
            cur = bd_ref[BD_K[g] + slab, pl.ds(sb * span + c, BLOCK, stride=rate), :]
            if sb == 0:
                prev = prev_ref[slab, pl.ds(c, BLOCK, stride=rate), :]
            else:
                prev = bd_ref[BD_K[g] + slab,
                              pl.ds((sb - 1) * span + c, BLOCK, stride=rate), :]
            return jnp.concatenate([prev, cur], axis=0).astype(_BF16)

        for sb in range(DIL_TILE // span):
            bias = bias0_ref if sb == 0 else bias_ref
            for c in range(rate):
                rows = pl.ds(sb * span + c, BLOCK, stride=rate)
                q = (bd_ref[BD_Q[g], rows, :] * Q_SCALE).astype(_BF16)
                finish(g, rows, *_pair_attention(
                    q, window(0, sb, c), window(1, sb, c), bias[g, 0], bias[g, 1]))

    lse =[l_scr[g] for g in range(DIL_GROUPS)]
    top = jnp.maximum(jnp.maximum(lse[0], lse[1]), lse[2])
    w = [jnp.exp(l - top) for l in lse]
    total = w[0] + w[1] + w[2]
    for g in range(DIL_GROUPS):
        ob_ref[:, g * B_G:(g + 1) * B_G] = (o_scr[g] * (w[g] / total)).astype(_BF16)


def _attn_dil(zb0, zbd, bias_b, batch, seq):
    t = zb0.shape[0]
    tiles = seq // DIL_TILE
    span4 = BLOCK * DIL_RATES[1]
    row = lambda b, i: (b * tiles + i, 0)
    bias_shape = (None,) + bias_b.shape[1:]
    return pl.pallas_call(
        _dil_kernel,
        grid=(batch, tiles),
        in_specs=[
            pl.BlockSpec((DIL_TILE, 3 * B_G), row),
            pl.BlockSpec((BLOCK, 2 * B_G), lambda b, i: (
                jnp.maximum((b * tiles + i) * (DIL_TILE // BLOCK) - 1, 0), 0)),
            pl.BlockSpec((BD_SLABS, DIL_TILE, B_G), lambda b, i: (0, b * tiles + i, 0)),
            pl.BlockSpec((2, DIL_TILE, B_G),
                         lambda b, i: (BD_K[2] // 2, jnp.maximum(b * tiles + i - 1, 0), 0)),
            pl.BlockSpec((2, span4, B_G), lambda b, i: (
                BD_K[1] // 2, jnp.maximum((b * tiles + i) * (DIL_TILE // span4) - 1, 0), 0)),
            pl.BlockSpec(bias_shape, lambda b, i: (jnp.minimum(i, 1), 0, 0, 0, 0)),
            pl.BlockSpec(bias_shape, lambda b, i: (1, 0, 0, 0, 0)),
        ],
        out_specs=pl.BlockSpec((DIL_TILE, B_W), row),
        out_shape=jax.ShapeDtypeStruct((t, B_W), _BF16),
        scratch_shapes=[
            pltpu.VMEM((DIL_GROUPS, DIL_TILE, B_G), _F32),
            pltpu.VMEM((DIL_GROUPS, DIL_TILE, B_G), _F32),
        ],
        compiler_params=_params(2),
        name="attn_dilated",
    )(zb0, zb0, zbd, zbd, zbd, bias_b, bias_b)


def _post_kernel(x_ref, oa_ref, ob_ref, oc_ref, g_ref, wg_ref, bg_ref,
                 wa_ref, wb_ref, wc_ref, wo_ref, o_ref):
    x = x_ref[...]
    d = x.shape[-1]
    h = _rms(x, g_ref[2:3, :]).astype(_BF16)
    branches = (_dot(oa_ref[...], wa_ref[...]),
                _dot(ob_ref[...], wb_ref[...]),
                _dot(oc_ref[...], wc_ref[...]))
    merged = None
    for n, br in enumerate(branches):
        gate = jax.nn.sigmoid(_dot(h, wg_ref[:, n * d:(n + 1) * d]) + bg_ref[n:n + 1, :])
        merged = gate * br if merged is None else merged + gate * br
    y = _dot(merged.astype(_BF16), wo_ref[...])
    o_ref[...] = x + _rms(y, g_ref[3:4, :])


def _post(x2d, oa, ob, oc, gains, w_gate, b_gate, w_a, w_b, w_c, w_o, layer):
    t, d = x2d.shape
    row = lambda i: (i, 0)
    return pl.pallas_call(
        _post_kernel,
        grid=(t // ROW_TILE,),
        in_specs=[
            pl.BlockSpec((ROW_TILE, d), row),
            pl.BlockSpec((ROW_TILE, A_Q), row),
            pl.BlockSpec((ROW_TILE, B_W), row),
            pl.BlockSpec((ROW_TILE, C_W), row),
            _layer_resident(gains.shape, layer),
            _layer_resident(w_gate.shape, layer),
            _layer_resident(b_gate.shape, layer),
            _layer_resident(w_a.shape, layer),
            _layer_resident(w_b.shape, layer),
            _layer_resident(w_c.shape, layer),
            _layer_resident(w_o.shape, layer),
        ],
        out_specs=pl.BlockSpec((ROW_TILE, d), row),
        out_shape=jax.ShapeDtypeStruct((t, d), _F32),
        compiler_params=_params(1),
        name="mix_out_proj",
    )(x2d, oa, ob, oc, gains, w_gate, b_gate, w_a, w_b, w_c, w_o)


def _t5_bucket(dist):
    max_exact = N_BUCKETS // 2
    d = jnp.maximum(dist, 1).astype(_F32)
    large = max_exact + (jnp.log(d / max_exact) / math.log(MAX_DISTANCE / max_exact)
                         * (N_BUCKETS - max_exact)).astype(jnp.int32)
    large = jnp.minimum(large, N_BUCKETS - 1)
    return jnp.where(dist < max_exact, dist, large)


def _band_bias(table, head0, n_heads, rate, max_dist):
    row = jnp.arange(BLOCK)[:, None]
    col = jnp.arange(2 * BLOCK)[None, :]
    dist = row + BLOCK - col
    bucket = _t5_bucket(jnp.maximum(dist, 0) * rate)
    bias = jnp.zeros((n_heads, BLOCK, 2 * BLOCK), _F32)
    for n in range(N_BUCKETS):
        bias = jnp.where(bucket[None] == n,
                         table[n, head0:head0 + n_heads][:, None, None], bias)
    valid = (dist >= 0) & (dist <= max_dist)
    later = jnp.where(valid[None], bias, MASKED)
    first = jnp.where((valid & (col >= BLOCK))[None], bias, MASKED)
    return jnp.stack([first, later])


def _permute_in_proj(w):
    hd = HEAD_DIM
    qa, ka, va = w[..., 0:A_Q], w[..., A_Q:A_Q + A_KV], w[..., A_Q + A_KV:A_Q + 2 * A_KV]
    off = A_Q + 2 * A_KV
    qb, kb, vb = (w[..., off + n * B_W:off + (n + 1) * B_W] for n in range(3))
    qc = w[..., off + 3 * B_W:]
    grp = lambda w3, g: w3[..., g * B_G:(g + 1) * B_G]
    cols = [ka, va] + [qa[..., h * hd:(h + 1) * hd] for h in A_PAIR_ORDER] + [qc]
    cols += [grp(kb, 0), grp(vb, 0), grp(qb, 0)]
    slabs = [None] * BD_SLABS
    for g in BD_K:
        slabs[BD_K[g]], slabs[BD_K[g] + 1], slabs[BD_Q[g]] = grp(kb, g), grp(vb, g), grp(qb, g)
    return jnp.concatenate(cols + slabs, axis=-1)


def kernel(x, mem, rel_bias, norm_gain, mem_norm_gain, w_ffn1_in, w_ffn1_out, w_in,
           sinks, w_mem_kv, w_gate, b_gate, w_br_a, w_br_b, w_br_c, w_o,
           w_ffn2_in, w_ffn2_out):
    batch, seq, d = x.shape
    depth = norm_gain.shape[0]
    assert seq % DIL_TILE == 0 and (batch * seq) % ROW_TILE == 0

    table = rel_bias.astype(_F32)
    bias_a = _band_bias(table, 0, SWA_HEADS, 1, SWA_WINDOW - 1)
    bias_b = jnp.stack([
        _band_bias(table, SWA_HEADS + g * DIL_HEADS_PER_GROUP, DIL_HEADS_PER_GROUP,
                   DIL_RATES[g], DIL_WINDOWS[g] // DIL_RATES[g])
        for g in range(DIL_GROUPS)], axis=1)

    bf = lambda w: w.astype(_BF16)
    w1_in, w1_out, w2_in, w2_out = bf(w_ffn1_in), bf(w_ffn1_out), bf(w_ffn2_in), bf(w_ffn2_out)
    w_in_p = bf(_permute_in_proj(w_in))
    w_kv, w_g, w_b, w_c, w_out = bf(w_mem_kv), bf(w_gate), bf(w_br_b), bf(w_br_c), bf(w_o)
    w_a = bf(jnp.concatenate(
        [w_br_a[:, h * HEAD_DIM:(h + 1) * HEAD_DIM] for h in A_PAIR_ORDER], axis=1))

    x2d = x.reshape(batch * seq, d)
    mem2d = mem.reshape(-1, d)
    for l in range(depth):
        x2d = _ffn(x2d, norm_gain, w1_in, w1_out, l, 0)
        za, qc, zb0, zbd = _pre(x2d, norm_gain, w_in_p, l)
        kvc = _mem_kv(mem2d, mem_norm_gain[:, None, :], w_kv, l)
        oa, oc = _attn_ac(za, qc, kvc, bias_a, sinks[l], batch, seq)
        ob = _attn_dil(zb0, zbd, bias_b, batch, seq)
        x2d = _post(x2d, oa, ob, oc, norm_gain, w_g, b_gate, w_a, w_b, w_c, w_out, l)
        x2d = _ffn(x2d, norm_gain, w2_in, w2_out, l, 4)
    return x2d.reshape(batch, seq, d)
```

```python
import functools
import math

import jax
import jax.numpy as jnp
from jax import lax
from jax.experimental import pallas as pl
from jax.experimental.pallas import tpu as pltpu

HEAD_DIM = 64
SWA_HEADS = 6
SWA_KV_HEADS = 2
SWA_GROUP = SWA_HEADS // SWA_KV_HEADS
SWA_WINDOW = 128
DIL_WINDOWS = (128, 512, 2048)
DIL_RATES = (1, 4, 16)
DIL_GROUPS = 3
DIL_HEADS_PER_GROUP = 2
MEM_HEADS = 4
N_BUCKETS = 32
MAX_DISTANCE = 2048
N_BRANCH = 3
EPS = 1e-6
BLOCK = 128
LANES = 128

A_Q = SWA_HEADS * HEAD_DIM
A_KV = SWA_KV_HEADS * HEAD_DIM
B_G = DIL_HEADS_PER_GROUP * HEAD_DIM
B_W = DIL_GROUPS * B_G
C_W = MEM_HEADS * HEAD_DIM
A_COLS = A_Q + 2 * A_KV
BD_SLABS = 3 * (DIL_GROUPS - 1)
BD_K = {1: 2, 2: 0}
BD_Q = {1: 4, 2: 5}
A_PAIR_ORDER = tuple(h for p in range(SWA_GROUP) for h in (p, p + SWA_GROUP))

Q_SCALE = HEAD_DIM ** -0.5
MASKED = -1e30

ROW_TILE = 512
FFN_CHUNK = 512
AC_TILE = 512
DIL_TILE = BLOCK * max(DIL_RATES)
VMEM_LIMIT = 56 * 1024 * 1024

_F32 = jnp.float32
_BF16 = jnp.bfloat16


def _dot(a, b):
    return jnp.dot(a, b, preferred_element_type=_F32)


def _dot_nt(a, b):
    return lax.dot_general(a, b, (((1,), (1,)), ((), ())),
                           preferred_element_type=_F32)


def _rms(x, gain):
    ms = jnp.mean(x * x, axis=-1, keepdims=True)
    return x * lax.rsqrt(ms + EPS) * gain


def _resident(shape):
    return pl.BlockSpec(shape, lambda *_: (0,) * len(shape),
                        pipeline_mode=pl.Buffered(1))


def _layer_resident(shape, layer):
    zeros = (0,) * (len(shape) - 1)
    return pl.BlockSpec((None,) + tuple(shape[1:]), lambda *_: (layer,) + zeros,
                        pipeline_mode=pl.Buffered(1))


def _params(n_axes):
    return pltpu.CompilerParams(
        dimension_semantics=("arbitrary",) * n_axes,
        vmem_limit_bytes=VMEM_LIMIT)


def _ffn_kernel(x_ref, g_ref, w_in_ref, w_out_ref, o_ref, *, ffn_dim, g0):
    x = x_ref[...]
    h = _rms(x, g_ref[g0:g0 + 1, :]).astype(_BF16)
    y = None
    for c0 in range(0, ffn_dim, FFN_CHUNK):
        c1 = min(c0 + FFN_CHUNK, ffn_dim)
        a = _dot(h, w_in_ref[:, c0:c1])
        b = _dot(h, w_in_ref[:, ffn_dim + c0:ffn_dim + c1])
        act = (a * jax.nn.sigmoid(a) * b).astype(_BF16)
        part = _dot(act, w_out_ref[c0:c1, :])
        y = part if y is None else y + part
    o_ref[...] = x + 0.5 * _rms(y, g_ref[g0 + 1:g0 + 2, :])


def _ffn(x2d, gains, w_in, w_out, layer, g0):
    t, d = x2d.shape
    ffn_dim = w_out.shape[1]
    return pl.pallas_call(
        functools.partial(_ffn_kernel, ffn_dim=ffn_dim, g0=g0),
        grid=(t // ROW_TILE,),
        in_specs=[
            pl.BlockSpec((ROW_TILE, d), lambda i: (i, 0)),
            _layer_resident(gains.shape, layer),
            _layer_resident(w_in.shape, layer),
            _layer_resident(w_out.shape, layer),
        ],
        out_specs=pl.BlockSpec((ROW_TILE, d), lambda i: (i, 0)),
        out_shape=jax.ShapeDtypeStruct((t, d), _F32),
        compiler_params=_params(1),
        name="ffn",
    )(x2d, gains, w_in, w_out)


def _pre_kernel(x_ref, g_ref, w_ref, a_ref, c_ref, b0_ref, bd_ref):
    h = _rms(x_ref[...], g_ref[2:3, :]).astype(_BF16)
    col = 0
    for ref in (a_ref, c_ref, b0_ref):
        width = ref.shape[-1]
        ref[...] = _dot(h, w_ref[:, col:col + width]).astype(_BF16)
        col += width
    zd = _dot(h, w_ref[:, col:])
    for n in range(BD_SLABS):
        bd_ref[n] = zd[:, n * B_G:(n + 1) * B_G]


def _pre(x2d, gains, w_perm, layer):
    t, d = x2d.shape
    row = lambda i: (i, 0)
    return pl.pallas_call(
        _pre_kernel,
        grid=(t // ROW_TILE,),
        in_specs=[
            pl.BlockSpec((ROW_TILE, d), row),
            _layer_resident(gains.shape, layer),
            _layer_resident(w_perm.shape, layer),
        ],
        out_specs=[
            pl.BlockSpec((ROW_TILE, A_COLS), row),
            pl.BlockSpec((ROW_TILE, C_W), row),
            pl.BlockSpec((ROW_TILE, 3 * B_G), row),
            pl.BlockSpec((BD_SLABS, ROW_TILE, B_G), lambda i: (0, i, 0)),
        ],
        out_shape=[
            jax.ShapeDtypeStruct((t, A_COLS), _BF16),
            jax.ShapeDtypeStruct((t, C_W), _BF16),
            jax.ShapeDtypeStruct((t, 3 * B_G), _BF16),
            jax.ShapeDtypeStruct((BD_SLABS, t, B_G), _F32),
        ],
        compiler_params=_params(1),
        name="mix_in_proj",
    )(x2d, gains, w_perm)


def _mem_kernel(m_ref, g_ref, w_ref, o_ref):
    h = _rms(m_ref[...], g_ref[...]).astype(_BF16)
    o_ref[...] = _dot(h, w_ref[...]).astype(_BF16)


def _mem_kv(mem2d, gains, w, layer):
    rows = mem2d.shape[0]
    return pl.pallas_call(
        _mem_kernel,
        grid=(1,),
        in_specs=[_resident(mem2d.shape),
                  _layer_resident(gains.shape, layer),
                  _layer_resident(w.shape, layer)],
        out_specs=pl.BlockSpec((rows, 2 * C_W), lambda i: (0, 0)),
        out_shape=jax.ShapeDtypeStruct((rows, 2 * C_W), _BF16),
        compiler_params=_params(1),
        name="mem_kv",
    )(mem2d, gains, w)


def _low_lanes():
    return lax.broadcasted_iota(jnp.int32, (1, LANES), 1) < HEAD_DIM


def _pair_scores(q, k, bias_lo=None, bias_hi=None):
    low = _low_lanes()
    zero = jnp.zeros((), k.dtype)
    s_lo = _dot_nt(q, jnp.where(low, k, zero))
    s_hi = _dot_nt(q, jnp.where(low, zero, k))
    if bias_lo is not None:
        s_lo = s_lo + bias_lo
        s_hi = s_hi + bias_hi
    return s_lo, s_hi


def _pair_softmax_pv(scores, v):
    s_lo, s_hi = scores
    low = _low_lanes()
    one = jnp.ones((), v.dtype)
    m_lo = jnp.max(s_lo, axis=-1, keepdims=True)
    m_hi = jnp.max(s_hi, axis=-1, keepdims=True)
    p_lo = jnp.exp(s_lo - m_lo).astype(_BF16)
    p_hi = jnp.exp(s_hi - m_hi).astype(_BF16)
    o_lo = _dot(p_lo, jnp.where(low, v, one))
    o_hi = _dot(p_hi, jnp.where(low, one, v))
    out = jnp.where(low, o_lo, o_hi)
    den = jnp.where(low, o_hi, o_lo)
    den = pltpu.roll(den, HEAD_DIM, axis=1)
    return out, den, m_lo, m_hi


PIPELINE_LOOKAHEAD = 2


def _software_pipeline(tiles):
    pending = []
    for n, (scores_fn, _) in enumerate(tiles):
        pending.append(scores_fn())
        if n >= PIPELINE_LOOKAHEAD:
            done = n - PIPELINE_LOOKAHEAD
            tiles[done][1](pending[done])
            pending[done] = None
    for done in range(max(len(tiles) - PIPELINE_LOOKAHEAD, 0), len(tiles)):
        tiles[done][1](pending[done])


def _ac_kernel(cur_ref, prev_ref, qc_ref, kvc_ref, bias0_ref, bias_ref, oa_ref, oc_ref):
    rows_a = SWA_GROUP * BLOCK
    sink_row = lax.broadcasted_iota(jnp.int32, (BLOCK, A_KV), 0) == 0

    def window_tile(j):
        rows = slice(j * BLOCK, (j + 1) * BLOCK)
        prev_rows = slice((j - 1) * BLOCK, j * BLOCK)
        bias = bias0_ref if j == 0 else bias_ref

        def kv_window(c0, c1):
            prev = prev_ref[:, c0:c1] if j == 0 else cur_ref[prev_rows, c0:c1]
            prev = jnp.where(sink_row, jnp.zeros((), prev.dtype), prev)
            return jnp.concatenate([prev, cur_ref[rows, c0:c1]], axis=0)

        def scores():
            q = jnp.concatenate(
                [cur_ref[rows, 2 * A_KV + p * LANES:2 * A_KV + (p + 1) * LANES]
                 for p in range(SWA_GROUP)], axis=0) * Q_SCALE
            return _pair_scores(q, kv_window(0, A_KV),
                                bias[0:SWA_GROUP].reshape(rows_a, 2 * BLOCK),
                                bias[SWA_GROUP:SWA_HEADS].reshape(rows_a, 2 * BLOCK))

        def finish(s):
            out, den, _, _ = _pair_softmax_pv(s, kv_window(A_KV, 2 * A_KV))
            o = out / den
            for p in range(SWA_GROUP):
                oa_ref[rows, p * LANES:(p + 1) * LANES] = (
                    o[p * BLOCK:(p + 1) * BLOCK]).astype(_BF16)

        return scores, finish

    def cross_tile(p):
        cols = slice(p * LANES, (p + 1) * LANES)

        def scores():
            return _pair_scores(qc_ref[:, cols] * Q_SCALE, kvc_ref[:, cols])

        def finish(s):
            out, den, _, _ = _pair_softmax_pv(
                s, kvc_ref[:, C_W + p * LANES:C_W + (p + 1) * LANES])
            oc_ref[:, cols] = (out / den).astype(_BF16)

        return scores, finish

    _software_pipeline([window_tile(j) for j in range(AC_TILE // BLOCK)]
                       + [cross_tile(p) for p in range(MEM_HEADS // 2)])


def _attn_ac(za, qc, kvc, bias_a, batch, seq):
    t = za.shape[0]
    tiles = seq // AC_TILE
    blocks_per_tile = AC_TILE // BLOCK
    mem_len = kvc.shape[0] // batch
    row = lambda b, i: (b * tiles + i, 0)
    prev = lambda b, i: (jnp.maximum((b * tiles + i) * blocks_per_tile - 1, 0), 0)
    bias_shape = (None,) + bias_a.shape[1:]
    return pl.pallas_call(
        _ac_kernel,
        grid=(batch, tiles),
        in_specs=[
            pl.BlockSpec((AC_TILE, A_COLS), row),
            pl.BlockSpec((BLOCK, 2 * A_KV), prev),
            pl.BlockSpec((AC_TILE, C_W), row),
            pl.BlockSpec((mem_len, 2 * C_W), lambda b, i: (b, 0)),
            pl.BlockSpec(bias_shape, lambda b, i: (jnp.minimum(i, 1), 0, 0, 0)),
            pl.BlockSpec(bias_shape, lambda b, i: (1, 0, 0, 0)),
        ],
        out_specs=[
            pl.BlockSpec((AC_TILE, A_Q), row),
            pl.BlockSpec((AC_TILE, C_W), row),
        ],
        out_shape=[
            jax.ShapeDtypeStruct((t, A_Q), _BF16),
            jax.ShapeDtypeStruct((t, C_W), _BF16),
        ],
        compiler_params=_params(2),
        name="attn_window_cross",
    )(za, za, qc, kvc, bias_a, bias_a)


def _dil_kernel(b0_ref, b0_prev_ref, bd_ref, prev16_ref, prev4_ref, bias0_ref, bias_ref,
                ob_ref, o_scr, l_scr):
    low = _low_lanes()

    def store(g, rows, out, den, m_lo, m_hi):
        o_scr[g, rows, :] = out / den
        l_scr[g, rows, :] = jnp.where(low, m_lo, m_hi) + jnp.log(den)

    def dense_tile(sb):
        rows = pl.ds(sb * BLOCK, BLOCK)
        bias = bias0_ref if sb == 0 else bias_ref

        def window(c0, c1):
            if sb == 0:
                prev = b0_prev_ref[:, c0:c1]
            else:
                prev = b0_ref[pl.ds((sb - 1) * BLOCK, BLOCK), c0:c1]
            return jnp.concatenate([prev, b0_ref[rows, c0:c1]], axis=0)

        def scores():
            return _pair_scores(b0_ref[rows, 2 * B_G:3 * B_G] * Q_SCALE, window(0, B_G),
                                bias[0, 0], bias[0, 1])

        def finish(s):
            store(0, rows, *_pair_softmax_pv(s, window(B_G, 2 * B_G)))

        return scores, finish

    def strided_tile(g, prev_ref, sb, c):
        rate = DIL_RATES[g]
        span = BLOCK * rate
        rows = pl.ds(sb * span + c, BLOCK, stride=rate)
        bias = bias0_ref if sb == 0 else bias_ref

        def window(slab):
            if sb == 0:
                prev = prev_ref[slab, pl.ds(c, BLOCK, stride=rate), :]
            else:
                prev = bd_ref[BD_K[g] + slab,
                              pl.ds((sb - 1) * span + c, BLOCK, stride=rate), :]
            return jnp.concatenate([prev, bd_ref[BD_K[g] + slab, rows, :]],
                                   axis=0).astype(_BF16)

        def scores():
            q = (bd_ref[BD_Q[g], rows, :] * Q_SCALE).astype(_BF16)
            return _pair_scores(q, window(0), bias[g, 0], bias[g, 1])

        def finish(s):
            store(g, rows, *_pair_softmax_pv(s, window(1)))

        return scores, finish

    tiles = [dense_tile(sb) for sb in range(DIL_TILE // BLOCK)]
    for g, prev_ref in ((1, prev4_ref), (2, prev16_ref)):
        tiles += [strided_tile(g, prev_ref, sb, c)
                  for sb in range(DIL_TILE // (BLOCK * DIL_RATES[g]))
                  for c in range(DIL_RATES[g])]
    _software_pipeline(tiles)

    lse =[l_scr[g] for g in range(DIL_GROUPS)]
    top = jnp.maximum(jnp.maximum(lse[0], lse[1]), lse[2])
    w = [jnp.exp(l - top) for l in lse]
    total = w[0] + w[1] + w[2]
    for g in range(DIL_GROUPS):
        ob_ref[:, g * B_G:(g + 1) * B_G] = (o_scr[g] * (w[g] / total)).astype(_BF16)


def _attn_dil(zb0, zbd, bias_b, batch, seq):
    t = zb0.shape[0]
    tiles = seq // DIL_TILE
    span4 = BLOCK * DIL_RATES[1]
    row = lambda b, i: (b * tiles + i, 0)
    bias_shape = (None,) + bias_b.shape[1:]
    return pl.pallas_call(
        _dil_kernel,
        grid=(batch, tiles),
        in_specs=[
            pl.BlockSpec((DIL_TILE, 3 * B_G), row),
            pl.BlockSpec((BLOCK, 2 * B_G), lambda b, i: (
                jnp.maximum((b * tiles + i) * (DIL_TILE // BLOCK) - 1, 0), 0)),
            pl.BlockSpec((BD_SLABS, DIL_TILE, B_G), lambda b, i: (0, b * tiles + i, 0)),
            pl.BlockSpec((2, DIL_TILE, B_G),
                         lambda b, i: (BD_K[2] // 2, jnp.maximum(b * tiles + i - 1, 0), 0)),
            pl.BlockSpec((2, span4, B_G), lambda b, i: (
                BD_K[1] // 2, jnp.maximum((b * tiles + i) * (DIL_TILE // span4) - 1, 0), 0)),
            pl.BlockSpec(bias_shape, lambda b, i: (jnp.minimum(i, 1), 0, 0, 0, 0)),
            pl.BlockSpec(bias_shape, lambda b, i: (1, 0, 0, 0, 0)),
        ],
        out_specs=pl.BlockSpec((DIL_TILE, B_W), row),
        out_shape=jax.ShapeDtypeStruct((t, B_W), _BF16),
        scratch_shapes=[
            pltpu.VMEM((DIL_GROUPS, DIL_TILE, B_G), _F32),
            pltpu.VMEM((DIL_GROUPS, DIL_TILE, B_G), _F32),
        ],
        compiler_params=_params(2),
        name="attn_dilated",
    )(zb0, zb0, zbd, zbd, zbd, bias_b, bias_b)


def _post_kernel(x_ref, oa_ref, ob_ref, oc_ref, g_ref, wg_ref, bg_ref,
                 wa_ref, wb_ref, wc_ref, wo_ref, o_ref):
    x = x_ref[...]
    d = x.shape[-1]
    h = _rms(x, g_ref[2:3, :]).astype(_BF16)
    branches = (_dot(oa_ref[...], wa_ref[...]),
                _dot(ob_ref[...], wb_ref[...]),
                _dot(oc_ref[...], wc_ref[...]))
    merged = None
    for n, br in enumerate(branches):
        gate = jax.nn.sigmoid(_dot(h, wg_ref[:, n * d:(n + 1) * d]) + bg_ref[n:n + 1, :])
        merged = gate * br if merged is None else merged + gate * br
    y = _dot(merged.astype(_BF16), wo_ref[...])
    o_ref[...] = x + _rms(y, g_ref[3:4, :])


def _post(x2d, oa, ob, oc, gains, w_gate, b_gate, w_a, w_b, w_c, w_o, layer):
    t, d = x2d.shape
    row = lambda i: (i, 0)
    return pl.pallas_call(
        _post_kernel,
        grid=(t // ROW_TILE,),
        in_specs=[
            pl.BlockSpec((ROW_TILE, d), row),
            pl.BlockSpec((ROW_TILE, A_Q), row),
            pl.BlockSpec((ROW_TILE, B_W), row),
            pl.BlockSpec((ROW_TILE, C_W), row),
            _layer_resident(gains.shape, layer),
            _layer_resident(w_gate.shape, layer),
            _layer_resident(b_gate.shape, layer),
            _layer_resident(w_a.shape, layer),
            _layer_resident(w_b.shape, layer),
            _layer_resident(w_c.shape, layer),
            _layer_resident(w_o.shape, layer),
        ],
        out_specs=pl.BlockSpec((ROW_TILE, d), row),
        out_shape=jax.ShapeDtypeStruct((t, d), _F32),
        compiler_params=_params(1),
        name="mix_out_proj",
    )(x2d, oa, ob, oc, gains, w_gate, b_gate, w_a, w_b, w_c, w_o)


def _t5_bucket(dist):
    max_exact = N_BUCKETS // 2
    d = jnp.maximum(dist, 1).astype(_F32)
    large = max_exact + (jnp.log(d / max_exact) / math.log(MAX_DISTANCE / max_exact)
                         * (N_BUCKETS - max_exact)).astype(jnp.int32)
    large = jnp.minimum(large, N_BUCKETS - 1)
    return jnp.where(dist < max_exact, dist, large)


def _band_bias(table, head0, n_heads, rate, max_dist):
    row = jnp.arange(BLOCK)[:, None]
    col = jnp.arange(2 * BLOCK)[None, :]
    dist = row + BLOCK - col
    bucket = _t5_bucket(jnp.maximum(dist, 0) * rate)
    bias = jnp.zeros((n_heads, BLOCK, 2 * BLOCK), _F32)
    for n in range(N_BUCKETS):
        bias = jnp.where(bucket[None] == n,
                         table[n, head0:head0 + n_heads][:, None, None], bias)
    valid = (dist >= 0) & (dist <= max_dist)
    later = jnp.where(valid[None], bias, MASKED)
    first = jnp.where((valid & (col >= BLOCK))[None], bias, MASKED)
    return jnp.stack([first, later])


def _permute_in_proj(w):
    hd = HEAD_DIM
    qa, ka, va = w[..., 0:A_Q], w[..., A_Q:A_Q + A_KV], w[..., A_Q + A_KV:A_Q + 2 * A_KV]
    off = A_Q + 2 * A_KV
    qb, kb, vb = (w[..., off + n * B_W:off + (n + 1) * B_W] for n in range(3))
    qc = w[..., off + 3 * B_W:]
    grp = lambda w3, g: w3[..., g * B_G:(g + 1) * B_G]
    cols = [ka, va] + [qa[..., h * hd:(h + 1) * hd] for h in A_PAIR_ORDER] + [qc]
    cols += [grp(kb, 0), grp(vb, 0), grp(qb, 0)]
    slabs = [None] * BD_SLABS
    for g in BD_K:
        slabs[BD_K[g]], slabs[BD_K[g] + 1], slabs[BD_Q[g]] = grp(kb, g), grp(vb, g), grp(qb, g)
    return jnp.concatenate(cols + slabs, axis=-1)


def kernel(x, mem, rel_bias, norm_gain, mem_norm_gain, w_ffn1_in, w_ffn1_out, w_in,
           sinks, w_mem_kv, w_gate, b_gate, w_br_a, w_br_b, w_br_c, w_o,
           w_ffn2_in, w_ffn2_out):
    batch, seq, d = x.shape
    depth = norm_gain.shape[0]
    assert seq % DIL_TILE == 0 and (batch * seq) % ROW_TILE == 0

    table = rel_bias.astype(_F32)
    bias_a = _band_bias(table, 0, SWA_HEADS, 1, SWA_WINDOW - 1)
    bias_b = jnp.stack([
        _band_bias(table, SWA_HEADS + g * DIL_HEADS_PER_GROUP, DIL_HEADS_PER_GROUP,
                   DIL_RATES[g], DIL_WINDOWS[g] // DIL_RATES[g])
        for g in range(DIL_GROUPS)], axis=1)

    bf = lambda w: w.astype(_BF16)
    w1_in, w1_out, w2_in, w2_out = bf(w_ffn1_in), bf(w_ffn1_out), bf(w_ffn2_in), bf(w_ffn2_out)
    w_in_p = bf(_permute_in_proj(w_in))
    w_kv, w_g, w_b, w_c, w_out = bf(w_mem_kv), bf(w_gate), bf(w_br_b), bf(w_br_c), bf(w_o)
    w_a = bf(jnp.concatenate(
        [w_br_a[:, h * HEAD_DIM:(h + 1) * HEAD_DIM] for h in A_PAIR_ORDER], axis=1))

    x2d = x.reshape(batch * seq, d)
    mem2d = mem.reshape(-1, d)
    for l in range(depth):
        x2d = _ffn(x2d, norm_gain, w1_in, w1_out, l, 0)
        za, qc, zb0, zbd = _pre(x2d, norm_gain, w_in_p, l)
        kvc = _mem_kv(mem2d, mem_norm_gain[:, None, :], w_kv, l)
        bias_a_l = bias_a.at[:, :, :, 0].set(sinks[l].astype(_F32)[None, :, None])
        oa, oc = _attn_ac(za, qc, kvc, bias_a_l, batch, seq)
        ob = _attn_dil(zb0, zbd, bias_b, batch, seq)
        x2d = _post(x2d, oa, ob, oc, norm_gain, w_g, b_gate, w_a, w_b, w_c, w_out, l)
        x2d = _ffn(x2d, norm_gain, w2_in, w2_out, l, 4)
    return x2d.reshape(batch, seq, d)
```

```python
import functools
import math

import jax
import jax.numpy as jnp
from jax import lax
from jax.experimental import pallas as pl
from jax.experimental.pallas import tpu as pltpu

HEAD_DIM = 64
SWA_HEADS = 6
SWA_KV_HEADS = 2
SWA_GROUP = SWA_HEADS // SWA_KV_HEADS
SWA_WINDOW = 128
DIL_WINDOWS = (128, 512, 2048)
DIL_RATES = (1, 4, 16)
DIL_GROUPS = 3
DIL_HEADS_PER_GROUP = 2
MEM_HEADS = 4
N_BUCKETS = 32
MAX_DISTANCE = 2048
N_BRANCH = 3
EPS = 1e-6
BLOCK = 128
LANES = 128

A_Q = SWA_HEADS * HEAD_DIM
A_KV = SWA_KV_HEADS * HEAD_DIM
B_G = DIL_HEADS_PER_GROUP * HEAD_DIM
B_W = DIL_GROUPS * B_G
C_W = MEM_HEADS * HEAD_DIM
A_COLS = A_Q + 2 * A_KV
BD_SLABS = 3 * (DIL_GROUPS - 1)
BD_K = {1: 2, 2: 0}
BD_Q = {1: 4, 2: 5}
A_PAIR_ORDER = tuple(h for p in range(SWA_GROUP) for h in (p, p + SWA_GROUP))

Q_SCALE = HEAD_DIM ** -0.5
MASKED = -1e30

ROW_TILE = 1024
SUB_ROWS = 256
FFN_CHUNK = 512
AC_TILE = 512
DIL_TILE = BLOCK * max(DIL_RATES)
VMEM_LIMIT = 56 * 1024 * 1024

_F32 = jnp.float32
_BF16 = jnp.bfloat16


def _dot(a, b):
    return jnp.dot(a, b, preferred_element_type=_F32)


def _dot_nt(a, b):
    return lax.dot_general(a, b, (((1,), (1,)), ((), ())),
                           preferred_element_type=_F32)


def _rms(x, gain):
    ms = jnp.mean(x * x, axis=-1, keepdims=True)
    return x * lax.rsqrt(ms + EPS) * gain


def _resident(shape):
    return pl.BlockSpec(shape, lambda *_: (0,) * len(shape),
                        pipeline_mode=pl.Buffered(1))


def _layer_resident(shape, layer):
    zeros = (0,) * (len(shape) - 1)
    return pl.BlockSpec((None,) + tuple(shape[1:]), lambda *_: (layer,) + zeros,
                        pipeline_mode=pl.Buffered(1))


def _params(n_axes):
    return pltpu.CompilerParams(
        dimension_semantics=("arbitrary",) * n_axes,
        vmem_limit_bytes=VMEM_LIMIT)


def _staggered(n_sub, stages):
    state = [{} for _ in range(n_sub)]
    for t in range(n_sub + len(stages) - 1):
        for s in range(n_sub):
            if 0 <= t - s < len(stages):
                stages[t - s](s, state[s])


def _sub_rows(s):
    return slice(s * SUB_ROWS, (s + 1) * SUB_ROWS)


def _ffn_kernel(x_ref, g_ref, w_in_ref, w_out_ref, o_ref, *, ffn_dim, g0):
    def norm_in(s, st):
        st["h"] = _rms(x_ref[_sub_rows(s), :], g_ref[g0:g0 + 1, :]).astype(_BF16)

    bounds = [(c0, min(c0 + FFN_CHUNK, ffn_dim)) for c0 in range(0, ffn_dim, FFN_CHUNK)]

    def step(c):
        def run(s, st):
            prev = st.pop("ab", None)
            if c < len(bounds):
                c0, c1 = bounds[c]
                st["ab"] = (_dot(st["h"], w_in_ref[:, c0:c1]),
                            _dot(st["h"], w_in_ref[:, ffn_dim + c0:ffn_dim + c1]))
            if prev is not None:
                c0, c1 = bounds[c - 1]
                a, b = prev
                act = (a * jax.nn.sigmoid(a) * b).astype(_BF16)
                part = _dot(act, w_out_ref[c0:c1, :])
                st["y"] = part if "y" not in st else st["y"] + part
        return run

    def norm_out(s, st):
        rows = _sub_rows(s)
        o_ref[rows, :] = x_ref[rows, :] + 0.5 * _rms(st["y"], g_ref[g0 + 1:g0 + 2, :])

    _staggered(ROW_TILE // SUB_ROWS,
               [norm_in] + [step(c) for c in range(len(bounds) + 1)] + [norm_out])


def _ffn(x2d, gains, w_in, w_out, layer, g0):
    t, d = x2d.shape
    ffn_dim = w_out.shape[1]
    return pl.pallas_call(
        functools.partial(_ffn_kernel, ffn_dim=ffn_dim, g0=g0),
        grid=(t // ROW_TILE,),
        in_specs=[
            pl.BlockSpec((ROW_TILE, d), lambda i: (i, 0)),
            _layer_resident(gains.shape, layer),
            _layer_resident(w_in.shape, layer),
            _layer_resident(w_out.shape, layer),
        ],
        out_specs=pl.BlockSpec((ROW_TILE, d), lambda i: (i, 0)),
        out_shape=jax.ShapeDtypeStruct((t, d), _F32),
        compiler_params=_params(1),
        name="ffn",
    )(x2d, gains, w_in, w_out)


def _pre_kernel(x_ref, g_ref, w_ref, a_ref, c_ref, b0_ref, bd_ref):
    def norm_in(s, st):
        st["h"] = _rms(x_ref[_sub_rows(s), :], g_ref[2:3, :]).astype(_BF16)

    bf16_refs = (a_ref, c_ref, b0_ref)
    bf16_cols = sum(ref.shape[-1] for ref in bf16_refs)

    def project_bf16(s, st):
        z = _dot(st["h"], w_ref[:, 0:bf16_cols])
        col = 0
        for ref in bf16_refs:
            ref[_sub_rows(s), :] = z[:, col:col + ref.shape[-1]].astype(_BF16)
            col += ref.shape[-1]

    def project_slabs(s, st):
        zd = _dot(st["h"], w_ref[:, bf16_cols:])
        for n in range(BD_SLABS):
            bd_ref[n, _sub_rows(s), :] = zd[:, n * B_G:(n + 1) * B_G]

    _staggered(ROW_TILE // SUB_ROWS, [norm_in, project_bf16, project_slabs])


def _pre(x2d, gains, w_perm, layer):
    t, d = x2d.shape
    row = lambda i: (i, 0)
    return pl.pallas_call(
        _pre_kernel,
        grid=(t // ROW_TILE,),
        in_specs=[
            pl.BlockSpec((ROW_TILE, d), row),
            _layer_resident(gains.shape, layer),
            _layer_resident(w_perm.shape, layer),
        ],
        out_specs=[
            pl.BlockSpec((ROW_TILE, A_COLS), row),
            pl.BlockSpec((ROW_TILE, C_W), row),
            pl.BlockSpec((ROW_TILE, 3 * B_G), row),
            pl.BlockSpec((BD_SLABS, ROW_TILE, B_G), lambda i: (0, i, 0)),
        ],
        out_shape=[
            jax.ShapeDtypeStruct((t, A_COLS), _BF16),
            jax.ShapeDtypeStruct((t, C_W), _BF16),
            jax.ShapeDtypeStruct((t, 3 * B_G), _BF16),
            jax.ShapeDtypeStruct((BD_SLABS, t, B_G), _F32),
        ],
        compiler_params=_params(1),
        name="mix_in_proj",
    )(x2d, gains, w_perm)


def _mem_kernel(m_ref, g_ref, w_ref, o_ref):
    h = _rms(m_ref[...], g_ref[...]).astype(_BF16)
    o_ref[...] = _dot(h, w_ref[...]).astype(_BF16)


def _mem_kv(mem2d, gains, w, layer):
    rows = mem2d.shape[0]
    return pl.pallas_call(
        _mem_kernel,
        grid=(1,),
        in_specs=[_resident(mem2d.shape),
                  _layer_resident(gains.shape, layer),
                  _layer_resident(w.shape, layer)],
        out_specs=pl.BlockSpec((rows, 2 * C_W), lambda i: (0, 0)),
        out_shape=jax.ShapeDtypeStruct((rows, 2 * C_W), _BF16),
        compiler_params=_params(1),
        name="mem_kv",
    )(mem2d, gains, w)


def _low_lanes():
    return lax.broadcasted_iota(jnp.int32, (1, LANES), 1) < HEAD_DIM


def _pair_scores(q, k, bias_lo=None, bias_hi=None):
    low = _low_lanes()
    zero = jnp.zeros((), k.dtype)
    s_lo = _dot_nt(q, jnp.where(low, k, zero))
    s_hi = _dot_nt(q, jnp.where(low, zero, k))
    if bias_lo is not None:
        s_lo = s_lo + bias_lo
        s_hi = s_hi + bias_hi
    return s_lo, s_hi


def _pair_softmax_pv(scores, v):
    s_lo, s_hi = scores
    low = _low_lanes()
    one = jnp.ones((), v.dtype)
    m_lo = jnp.max(s_lo, axis=-1, keepdims=True)
    m_hi = jnp.max(s_hi, axis=-1, keepdims=True)
    p_lo = jnp.exp(s_lo - m_lo).astype(_BF16)
    p_hi = jnp.exp(s_hi - m_hi).astype(_BF16)
    o_lo = _dot(p_lo, jnp.where(low, v, one))
    o_hi = _dot(p_hi, jnp.where(low, one, v))
    out = jnp.where(low, o_lo, o_hi)
    den = jnp.where(low, o_hi, o_lo)
    den = pltpu.roll(den, HEAD_DIM, axis=1)
    return out, den, m_lo, m_hi


PIPELINE_LOOKAHEAD = 2


def _software_pipeline(tiles):
    pending = []
    for n, (scores_fn, _) in enumerate(tiles):
        pending.append(scores_fn())
        if n >= PIPELINE_LOOKAHEAD:
            done = n - PIPELINE_LOOKAHEAD
            tiles[done][1](pending[done])
            pending[done] = None
    for done in range(max(len(tiles) - PIPELINE_LOOKAHEAD, 0), len(tiles)):
        tiles[done][1](pending[done])


def _ac_kernel(cur_ref, prev_ref, qc_ref, kvc_ref, bias0_ref, bias_ref, oa_ref, oc_ref):
    rows_a = SWA_GROUP * BLOCK
    sink_row = lax.broadcasted_iota(jnp.int32, (BLOCK, A_KV), 0) == 0

    def window_tile(j):
        rows = slice(j * BLOCK, (j + 1) * BLOCK)
        prev_rows = slice((j - 1) * BLOCK, j * BLOCK)
        bias = bias0_ref if j == 0 else bias_ref

        def kv_window(c0, c1):
            prev = prev_ref[:, c0:c1] if j == 0 else cur_ref[prev_rows, c0:c1]
            prev = jnp.where(sink_row, jnp.zeros((), prev.dtype), prev)
            return jnp.concatenate([prev, cur_ref[rows, c0:c1]], axis=0)

        def scores():
            q = jnp.concatenate(
                [cur_ref[rows, 2 * A_KV + p * LANES:2 * A_KV + (p + 1) * LANES]
                 for p in range(SWA_GROUP)], axis=0) * Q_SCALE
            return _pair_scores(q, kv_window(0, A_KV),
                                bias[0:SWA_GROUP].reshape(rows_a, 2 * BLOCK),
                                bias[SWA_GROUP:SWA_HEADS].reshape(rows_a, 2 * BLOCK))

        def finish(s):
            out, den, _, _ = _pair_softmax_pv(s, kv_window(A_KV, 2 * A_KV))
            o = out / den
            for p in range(SWA_GROUP):
                oa_ref[rows, p * LANES:(p + 1) * LANES] = (
                    o[p * BLOCK:(p + 1) * BLOCK]).astype(_BF16)

        return scores, finish

    def cross_tile(p):
        cols = slice(p * LANES, (p + 1) * LANES)

        def scores():
            return _pair_scores(qc_ref[:, cols] * Q_SCALE, kvc_ref[:, cols])

        def finish(s):
            out, den, _, _ = _pair_softmax_pv(
                s, kvc_ref[:, C_W + p * LANES:C_W + (p + 1) * LANES])
            oc_ref[:, cols] = (out / den).astype(_BF16)

        return scores, finish

    _software_pipeline([window_tile(j) for j in range(AC_TILE // BLOCK)]
                       + [cross_tile(p) for p in range(MEM_HEADS // 2)])


def _attn_ac(za, qc, kvc, bias_a, batch, seq):
    t = za.shape[0]
    tiles = seq // AC_TILE
    blocks_per_tile = AC_TILE // BLOCK
    mem_len = kvc.shape[0] // batch
    row = lambda b, i: (b * tiles + i, 0)
    prev = lambda b, i: (jnp.maximum((b * tiles + i) * blocks_per_tile - 1, 0), 0)
    bias_shape = (None,) + bias_a.shape[1:]
    return pl.pallas_call(
        _ac_kernel,
        grid=(batch, tiles),
        in_specs=[
            pl.BlockSpec((AC_TILE, A_COLS), row),
            pl.BlockSpec((BLOCK, 2 * A_KV), prev),
            pl.BlockSpec((AC_TILE, C_W), row),
            pl.BlockSpec((mem_len, 2 * C_W), lambda b, i: (b, 0)),
            pl.BlockSpec(bias_shape, lambda b, i: (jnp.minimum(i, 1), 0, 0, 0)),
            pl.BlockSpec(bias_shape, lambda b, i: (1, 0, 0, 0)),
        ],
        out_specs=[
            pl.BlockSpec((AC_TILE, A_Q), row),
            pl.BlockSpec((AC_TILE, C_W), row),
        ],
        out_shape=[
            jax.ShapeDtypeStruct((t, A_Q), _BF16),
            jax.ShapeDtypeStruct((t, C_W), _BF16),
        ],
        compiler_params=_params(2),
        name="attn_window_cross",
    )(za, za, qc, kvc, bias_a, bias_a)


def _dil_kernel(b0_ref, b0_prev_ref, bd_ref, prev16_ref, prev4_ref, bias0_ref, bias_ref,
                ob_ref, o_scr, l_scr):
    low = _low_lanes()

    def store(g, rows, out, den, m_lo, m_hi):
        o_scr[g, rows, :] = out / den
        l_scr[g, rows, :] = jnp.where(low, m_lo, m_hi) + jnp.log(den)

    def dense_tile(sb):
        rows = pl.ds(sb * BLOCK, BLOCK)
        bias = bias0_ref if sb == 0 else bias_ref

        def window(c0, c1):
            if sb == 0:
                prev = b0_prev_ref[:, c0:c1]
            else:
                prev = b0_ref[pl.ds((sb - 1) * BLOCK, BLOCK), c0:c1]
            return jnp.concatenate([prev, b0_ref[rows, c0:c1]], axis=0)

        def scores():
            return _pair_scores(b0_ref[rows, 2 * B_G:3 * B_G] * Q_SCALE, window(0, B_G),
                                bias[0, 0], bias[0, 1])

        def finish(s):
            store(0, rows, *_pair_softmax_pv(s, window(B_G, 2 * B_G)))

        return scores, finish

    def strided_tile(g, prev_ref, sb, c):
        rate = DIL_RATES[g]
        span = BLOCK * rate
        rows = pl.ds(sb * span + c, BLOCK, stride=rate)
        bias = bias0_ref if sb == 0 else bias_ref

        def window(slab):
            if sb == 0:
                prev = prev_ref[slab, pl.ds(c, BLOCK, stride=rate), :]
            else:
                prev = bd_ref[BD_K[g] + slab,
                              pl.ds((sb - 1) * span + c, BLOCK, stride=rate), :]
            return jnp.concatenate([prev, bd_ref[BD_K[g] + slab, rows, :]],
                                   axis=0).astype(_BF16)

        def scores():
            q = (bd_ref[BD_Q[g], rows, :] * Q_SCALE).astype(_BF16)
            return _pair_scores(q, window(0), bias[g, 0], bias[g, 1])

        def finish(s):
            store(g, rows, *_pair_softmax_pv(s, window(1)))

        return scores, finish

    tiles = [dense_tile(sb) for sb in range(DIL_TILE // BLOCK)]
    for g, prev_ref in ((1, prev4_ref), (2, prev16_ref)):
        tiles += [strided_tile(g, prev_ref, sb, c)
                  for sb in range(DIL_TILE // (BLOCK * DIL_RATES[g]))
                  for c in range(DIL_RATES[g])]
    _software_pipeline(tiles)

    lse =[l_scr[g] for g in range(DIL_GROUPS)]
    top = jnp.maximum(jnp.maximum(lse[0], lse[1]), lse[2])
    w = [jnp.exp(l - top) for l in lse]
    total = w[0] + w[1] + w[2]
    for g in range(DIL_GROUPS):
        ob_ref[:, g * B_G:(g + 1) * B_G] = (o_scr[g] * (w[g] / total)).astype(_BF16)


def _attn_dil(zb0, zbd, bias_b, batch, seq):
    t = zb0.shape[0]
    tiles = seq // DIL_TILE
    span4 = BLOCK * DIL_RATES[1]
    row = lambda b, i: (b * tiles + i, 0)
    bias_shape = (None,) + bias_b.shape[1:]
    return pl.pallas_call(
        _dil_kernel,
        grid=(batch, tiles),
        in_specs=[
            pl.BlockSpec((DIL_TILE, 3 * B_G), row),
            pl.BlockSpec((BLOCK, 2 * B_G), lambda b, i: (
                jnp.maximum((b * tiles + i) * (DIL_TILE // BLOCK) - 1, 0), 0)),
            pl.BlockSpec((BD_SLABS, DIL_TILE, B_G), lambda b, i: (0, b * tiles + i, 0)),
            pl.BlockSpec((2, DIL_TILE, B_G),
                         lambda b, i: (BD_K[2] // 2, jnp.maximum(b * tiles + i - 1, 0), 0)),
            pl.BlockSpec((2, span4, B_G), lambda b, i: (
                BD_K[1] // 2, jnp.maximum((b * tiles + i) * (DIL_TILE // span4) - 1, 0), 0)),
            pl.BlockSpec(bias_shape, lambda b, i: (jnp.minimum(i, 1), 0, 0, 0, 0)),
            pl.BlockSpec(bias_shape, lambda b, i: (1, 0, 0, 0, 0)),
        ],
        out_specs=pl.BlockSpec((DIL_TILE, B_W), row),
        out_shape=jax.ShapeDtypeStruct((t, B_W), _BF16),
        scratch_shapes=[
            pltpu.VMEM((DIL_GROUPS, DIL_TILE, B_G), _F32),
            pltpu.VMEM((DIL_GROUPS, DIL_TILE, B_G), _F32),
        ],
        compiler_params=_params(2),
        name="attn_dilated",
    )(zb0, zb0, zbd, zbd, zbd, bias_b, bias_b)


def _post_kernel(x_ref, oa_ref, ob_ref, oc_ref, g_ref, wg_ref, bg_ref,
                 wa_ref, wb_ref, wc_ref, wo_ref, o_ref):
    d = x_ref.shape[-1]

    def norm_in(s, st):
        st["h"] = _rms(x_ref[_sub_rows(s), :], g_ref[2:3, :]).astype(_BF16)

    def branch(n, o_branch_ref, w_branch_ref):
        def run(s, st):
            gate = jax.nn.sigmoid(_dot(st["h"], wg_ref[:, n * d:(n + 1) * d])
                                  + bg_ref[n:n + 1, :])
            term = gate * _dot(o_branch_ref[_sub_rows(s), :], w_branch_ref[...])
            st["merged"] = term if n == 0 else st["merged"] + term
        return run

    def project(s, st):
        st["y"] = _dot(st["merged"].astype(_BF16), wo_ref[...])

    def norm_out(s, st):
        rows = _sub_rows(s)
        o_ref[rows, :] = x_ref[rows, :] + _rms(st["y"], g_ref[3:4, :])

    _staggered(ROW_TILE // SUB_ROWS,
               [norm_in, branch(0, oa_ref, wa_ref), branch(1, ob_ref, wb_ref),
                branch(2, oc_ref, wc_ref), project, norm_out])


def _post(x2d, oa, ob, oc, gains, w_gate, b_gate, w_a, w_b, w_c, w_o, layer):
    t, d = x2d.shape
    row = lambda i: (i, 0)
    return pl.pallas_call(
        _post_kernel,
        grid=(t // ROW_TILE,),
        in_specs=[
            pl.BlockSpec((ROW_TILE, d), row),
            pl.BlockSpec((ROW_TILE, A_Q), row),
            pl.BlockSpec((ROW_TILE, B_W), row),
            pl.BlockSpec((ROW_TILE, C_W), row),
            _layer_resident(gains.shape, layer),
            _layer_resident(w_gate.shape, layer),
            _layer_resident(b_gate.shape, layer),
            _layer_resident(w_a.shape, layer),
            _layer_resident(w_b.shape, layer),
            _layer_resident(w_c.shape, layer),
            _layer_resident(w_o.shape, layer),
        ],
        out_specs=pl.BlockSpec((ROW_TILE, d), row),
        out_shape=jax.ShapeDtypeStruct((t, d), _F32),
        compiler_params=_params(1),
        name="mix_out_proj",
    )(x2d, oa, ob, oc, gains, w_gate, b_gate, w_a, w_b, w_c, w_o)


def _t5_bucket(dist):
    max_exact = N_BUCKETS // 2
    d = jnp.maximum(dist, 1).astype(_F32)
    large = max_exact + (jnp.log(d / max_exact) / math.log(MAX_DISTANCE / max_exact)
                         * (N_BUCKETS - max_exact)).astype(jnp.int32)
    large = jnp.minimum(large, N_BUCKETS - 1)
    return jnp.where(dist < max_exact, dist, large)


def _band_bias(table, head0, n_heads, rate, max_dist):
    row = jnp.arange(BLOCK)[:, None]
    col = jnp.arange(2 * BLOCK)[None, :]
    dist = row + BLOCK - col
    bucket = _t5_bucket(jnp.maximum(dist, 0) * rate)
    bias = jnp.zeros((n_heads, BLOCK, 2 * BLOCK), _F32)
    for n in range(N_BUCKETS):
        bias = jnp.where(bucket[None] == n,
                         table[n, head0:head0 + n_heads][:, None, None], bias)
    valid = (dist >= 0) & (dist <= max_dist)
    later = jnp.where(valid[None], bias, MASKED)
    first = jnp.where((valid & (col >= BLOCK))[None], bias, MASKED)
    return jnp.stack([first, later])


def _permute_in_proj(w):
    hd = HEAD_DIM
    qa, ka, va = w[..., 0:A_Q], w[..., A_Q:A_Q + A_KV], w[..., A_Q + A_KV:A_Q + 2 * A_KV]
    off = A_Q + 2 * A_KV
    qb, kb, vb = (w[..., off + n * B_W:off + (n + 1) * B_W] for n in range(3))
    qc = w[..., off + 3 * B_W:]
    grp = lambda w3, g: w3[..., g * B_G:(g + 1) * B_G]
    cols = [ka, va] + [qa[..., h * hd:(h + 1) * hd] for h in A_PAIR_ORDER] + [qc]
    cols += [grp(kb, 0), grp(vb, 0), grp(qb, 0)]
    slabs = [None] * BD_SLABS
    for g in BD_K:
        slabs[BD_K[g]], slabs[BD_K[g] + 1], slabs[BD_Q[g]] = grp(kb, g), grp(vb, g), grp(qb, g)
    return jnp.concatenate(cols + slabs, axis=-1)


def kernel(x, mem, rel_bias, norm_gain, mem_norm_gain, w_ffn1_in, w_ffn1_out, w_in,
           sinks, w_mem_kv, w_gate, b_gate, w_br_a, w_br_b, w_br_c, w_o,
           w_ffn2_in, w_ffn2_out):
    batch, seq, d = x.shape
    depth = norm_gain.shape[0]
    assert seq % DIL_TILE == 0 and (batch * seq) % ROW_TILE == 0

    table = rel_bias.astype(_F32)
    bias_a = _band_bias(table, 0, SWA_HEADS, 1, SWA_WINDOW - 1)
    bias_b = jnp.stack([
        _band_bias(table, SWA_HEADS + g * DIL_HEADS_PER_GROUP, DIL_HEADS_PER_GROUP,
                   DIL_RATES[g], DIL_WINDOWS[g] // DIL_RATES[g])
        for g in range(DIL_GROUPS)], axis=1)

    bf = lambda w: w.astype(_BF16)
    w1_in, w1_out, w2_in, w2_out = bf(w_ffn1_in), bf(w_ffn1_out), bf(w_ffn2_in), bf(w_ffn2_out)
    w_in_p = bf(_permute_in_proj(w_in))
    w_kv, w_g, w_b, w_c, w_out = bf(w_mem_kv), bf(w_gate), bf(w_br_b), bf(w_br_c), bf(w_o)
    w_a = bf(jnp.concatenate(
        [w_br_a[:, h * HEAD_DIM:(h + 1) * HEAD_DIM] for h in A_PAIR_ORDER], axis=1))

    x2d = x.reshape(batch * seq, d)
    mem2d = mem.reshape(-1, d)
    for l in range(depth):
        x2d = _ffn(x2d, norm_gain, w1_in, w1_out, l, 0)
        za, qc, zb0, zbd = _pre(x2d, norm_gain, w_in_p, l)
        kvc = _mem_kv(mem2d, mem_norm_gain[:, None, :], w_kv, l)
        bias_a_l = bias_a.at[:, :, :, 0].set(sinks[l].astype(_F32)[None, :, None])
        oa, oc = _attn_ac(za, qc, kvc, bias_a_l, batch, seq)
        ob = _attn_dil(zb0, zbd, bias_b, batch, seq)
        x2d = _post(x2d, oa, ob, oc, norm_gain, w_g, b_gate, w_a, w_b, w_c, w_out, l)
        x2d = _ffn(x2d, norm_gain, w2_in, w2_out, l, 4)
    return x2d.reshape(batch, seq, d)
```

```python
import functools
import math

import jax
import jax.numpy as jnp
from jax import lax
from jax.experimental import pallas as pl
from jax.experimental.pallas import tpu as pltpu

HEAD_DIM = 64
SWA_HEADS = 6
SWA_KV_HEADS = 2
SWA_GROUP = SWA_HEADS // SWA_KV_HEADS
SWA_WINDOW = 128
DIL_WINDOWS = (128, 512, 2048)
DIL_RATES = (1, 4, 16)
DIL_GROUPS = 3
DIL_HEADS_PER_GROUP = 2
MEM_HEADS = 4
N_BUCKETS = 32
MAX_DISTANCE = 2048
N_BRANCH = 3
EPS = 1e-6
BLOCK = 128
LANES = 128

A_Q = SWA_HEADS * HEAD_DIM
A_KV = SWA_KV_HEADS * HEAD_DIM
B_G = DIL_HEADS_PER_GROUP * HEAD_DIM
B_W = DIL_GROUPS * B_G
C_W = MEM_HEADS * HEAD_DIM
A_COLS = A_Q + 2 * A_KV
B_COLS = 3 * B_G
A_PAIR_ORDER = tuple(h for p in range(SWA_GROUP) for h in (p, p + SWA_GROUP))

LOG2_E = math.log2(math.e)
Q_LOG2_SCALE = HEAD_DIM ** -0.5 * LOG2_E
MASKED = -1e30

ROW_TILE = 1024
SUB_ROWS = 256
FFN_CHUNK = 512
AC_TILE = 512
DIL_TILE = BLOCK * max(DIL_RATES)
VMEM_LIMIT = 56 * 1024 * 1024

_F32 = jnp.float32
_BF16 = jnp.bfloat16


def _dot(a, b):
    return jnp.dot(a, b, preferred_element_type=_F32)


def _dot_nt(a, b):
    return lax.dot_general(a, b, (((1,), (1,)), ((), ())),
                           preferred_element_type=_F32)


def _rms(x, gain):
    ms = jnp.mean(x * x, axis=-1, keepdims=True)
    return x * lax.rsqrt(ms + EPS) * gain


def _resident(shape):
    return pl.BlockSpec(shape, lambda *_: (0,) * len(shape),
                        pipeline_mode=pl.Buffered(1))


def _layer_resident(shape, layer):
    zeros = (0,) * (len(shape) - 1)
    return pl.BlockSpec((None,) + tuple(shape[1:]), lambda *_: (layer,) + zeros,
                        pipeline_mode=pl.Buffered(1))


def _params(n_axes):
    return pltpu.CompilerParams(
        dimension_semantics=("arbitrary",) * n_axes,
        vmem_limit_bytes=VMEM_LIMIT)


def _staggered(n_sub, stages):
    state = [{} for _ in range(n_sub)]
    for t in range(n_sub + len(stages) - 1):
        for s in range(n_sub):
            if 0 <= t - s < len(stages):
                stages[t - s](s, state[s])


def _sub_rows(s):
    return slice(s * SUB_ROWS, (s + 1) * SUB_ROWS)


def _ffn_kernel(x_ref, g_ref, w_in_ref, w_out_ref, o_ref, *, ffn_dim, g0):
    def norm_in(s, st):
        st["h"] = _rms(x_ref[_sub_rows(s), :], g_ref[g0:g0 + 1, :]).astype(_BF16)

    bounds = [(c0, min(c0 + FFN_CHUNK, ffn_dim)) for c0 in range(0, ffn_dim, FFN_CHUNK)]

    def step(c):
        def run(s, st):
            prev = st.pop("ab", None)
            if c < len(bounds):
                c0, c1 = bounds[c]
                st["ab"] = (_dot(st["h"], w_in_ref[:, c0:c1]),
                            _dot(st["h"], w_in_ref[:, ffn_dim + c0:ffn_dim + c1]))
            if prev is not None:
                c0, c1 = bounds[c - 1]
                a, b = prev
                act = (a * jax.nn.sigmoid(a) * b).astype(_BF16)
                part = _dot(act, w_out_ref[c0:c1, :])
                st["y"] = part if "y" not in st else st["y"] + part
        return run

    def norm_out(s, st):
        rows = _sub_rows(s)
        o_ref[rows, :] = x_ref[rows, :] + 0.5 * _rms(st["y"], g_ref[g0 + 1:g0 + 2, :])

    _staggered(ROW_TILE // SUB_ROWS,
               [norm_in] + [step(c) for c in range(len(bounds) + 1)] + [norm_out])


def _ffn(x2d, gains, w_in, w_out, layer, g0):
    t, d = x2d.shape
    ffn_dim = w_out.shape[1]
    return pl.pallas_call(
        functools.partial(_ffn_kernel, ffn_dim=ffn_dim, g0=g0),
        grid=(t // ROW_TILE,),
        in_specs=[
            pl.BlockSpec((ROW_TILE, d), lambda i: (i, 0)),
            _layer_resident(gains.shape, layer),
            _layer_resident(w_in.shape, layer),
            _layer_resident(w_out.shape, layer),
        ],
        out_specs=pl.BlockSpec((ROW_TILE, d), lambda i: (i, 0)),
        out_shape=jax.ShapeDtypeStruct((t, d), _F32),
        compiler_params=_params(1),
        name="ffn",
    )(x2d, gains, w_in, w_out)


def _pre_kernel(x_ref, g_ref, w_ref, qs_ref, a_ref, c_ref, b0_ref, b1_ref, b2_ref):
    def norm_in(s, st):
        st["h"] = _rms(x_ref[_sub_rows(s), :], g_ref[2:3, :]).astype(_BF16)

    def project(refs, col0):
        width = sum(ref.shape[-1] for ref in refs)

        def run(s, st):
            z = _dot(st["h"], w_ref[:, col0:col0 + width]) * qs_ref[:, col0:col0 + width]
            col = 0
            for ref in refs:
                ref[_sub_rows(s), :] = z[:, col:col + ref.shape[-1]].astype(_BF16)
                col += ref.shape[-1]
        return run

    first = (a_ref, c_ref, b0_ref)
    first_cols = sum(ref.shape[-1] for ref in first)
    _staggered(ROW_TILE // SUB_ROWS,
               [norm_in, project(first, 0), project((b1_ref, b2_ref), first_cols)])


def _pre(x2d, gains, w_perm, q_scale, layer):
    t, d = x2d.shape
    row = lambda i: (i, 0)
    return pl.pallas_call(
        _pre_kernel,
        grid=(t // ROW_TILE,),
        in_specs=[
            pl.BlockSpec((ROW_TILE, d), row),
            _layer_resident(gains.shape, layer),
            _layer_resident(w_perm.shape, layer),
            _resident(q_scale.shape),
        ],
        out_specs=[
            pl.BlockSpec((ROW_TILE, A_COLS), row),
            pl.BlockSpec((ROW_TILE, C_W), row),
        ] + [pl.BlockSpec((ROW_TILE, B_COLS), row)] * DIL_GROUPS,
        out_shape=[
            jax.ShapeDtypeStruct((t, A_COLS), _BF16),
            jax.ShapeDtypeStruct((t, C_W), _BF16),
        ] + [jax.ShapeDtypeStruct((t, B_COLS), _BF16)] * DIL_GROUPS,
        compiler_params=_params(1),
        name="mix_in_proj",
    )(x2d, gains, w_perm, q_scale)


def _mem_kernel(m_ref, g_ref, w_ref, o_ref):
    h = _rms(m_ref[...], g_ref[...]).astype(_BF16)
    o_ref[...] = _dot(h, w_ref[...]).astype(_BF16)


def _mem_kv(mem2d, gains, w, layer):
    rows = mem2d.shape[0]
    return pl.pallas_call(
        _mem_kernel,
        grid=(1,),
        in_specs=[_resident(mem2d.shape),
                  _layer_resident(gains.shape, layer),
                  _layer_resident(w.shape, layer)],
        out_specs=pl.BlockSpec((rows, 2 * C_W), lambda i: (0, 0)),
        out_shape=jax.ShapeDtypeStruct((rows, 2 * C_W), _BF16),
        compiler_params=_params(1),
        name="mem_kv",
    )(mem2d, gains, w)


def _low_lanes():
    return lax.broadcasted_iota(jnp.int32, (1, LANES), 1) < HEAD_DIM


def _pair_scores(q, k, bias_lo=None, bias_hi=None):
    low = _low_lanes()
    zero = jnp.zeros((), k.dtype)
    if q.shape[0] <= k.shape[0]:
        s_lo = _dot_nt(jnp.where(low, q, zero), k)
        s_hi = _dot_nt(jnp.where(low, zero, q), k)
    else:
        s_lo = _dot_nt(q, jnp.where(low, k, zero))
        s_hi = _dot_nt(q, jnp.where(low, zero, k))
    if bias_lo is not None:
        s_lo = s_lo + bias_lo
        s_hi = s_hi + bias_hi
    return s_lo, s_hi


def _pair_softmax_pv(scores, v):
    s_lo, s_hi = scores
    low = _low_lanes()
    one = jnp.ones((), v.dtype)
    m_lo = jnp.max(s_lo, axis=-1, keepdims=True)
    m_hi = jnp.max(s_hi, axis=-1, keepdims=True)
    p_lo = jnp.exp2(s_lo - m_lo).astype(_BF16)
    p_hi = jnp.exp2(s_hi - m_hi).astype(_BF16)
    o_lo = _dot(p_lo, jnp.where(low, v, one))
    o_hi = _dot(p_hi, jnp.where(low, one, v))
    out = jnp.where(low, o_lo, o_hi)
    den = jnp.where(low, o_hi, o_lo)
    den = pltpu.roll(den, HEAD_DIM, axis=1)
    return out, den, m_lo, m_hi


PIPELINE_LOOKAHEAD = 2


def _software_pipeline(tiles):
    pending = []
    for n, (scores_fn, _) in enumerate(tiles):
        pending.append(scores_fn())
        if n >= PIPELINE_LOOKAHEAD:
            done = n - PIPELINE_LOOKAHEAD
            tiles[done][1](pending[done])
            pending[done] = None
    for done in range(max(len(tiles) - PIPELINE_LOOKAHEAD, 0), len(tiles)):
        tiles[done][1](pending[done])


def _ac_kernel(cur_ref, prev_ref, qc_ref, kvc_ref, bias0_ref, bias_ref, oa_ref, oc_ref):
    rows_a = SWA_GROUP * BLOCK
    sink_row = lax.broadcasted_iota(jnp.int32, (BLOCK, A_KV), 0) == 0

    def window_tile(j):
        rows = slice(j * BLOCK, (j + 1) * BLOCK)
        prev_rows = slice((j - 1) * BLOCK, j * BLOCK)
        bias = bias0_ref if j == 0 else bias_ref

        def kv_window(c0, c1):
            prev = prev_ref[:, c0:c1] if j == 0 else cur_ref[prev_rows, c0:c1]
            prev = jnp.where(sink_row, jnp.zeros((), prev.dtype), prev)
            return jnp.concatenate([prev, cur_ref[rows, c0:c1]], axis=0)

        def scores():
            q = jnp.concatenate(
                [cur_ref[rows, 2 * A_KV + p * LANES:2 * A_KV + (p + 1) * LANES]
                 for p in range(SWA_GROUP)], axis=0)
            return _pair_scores(q, kv_window(0, A_KV),
                                bias[0:SWA_GROUP].reshape(rows_a, 2 * BLOCK),
                                bias[SWA_GROUP:SWA_HEADS].reshape(rows_a, 2 * BLOCK))

        def finish(s):
            out, den, _, _ = _pair_softmax_pv(s, kv_window(A_KV, 2 * A_KV))
            o = out / den
            for p in range(SWA_GROUP):
                oa_ref[rows, p * LANES:(p + 1) * LANES] = (
                    o[p * BLOCK:(p + 1) * BLOCK]).astype(_BF16)

        return scores, finish

    def cross_tile(p):
        cols = slice(p * LANES, (p + 1) * LANES)

        def scores():
            return _pair_scores(qc_ref[:, cols], kvc_ref[:, cols])

        def finish(s):
            out, den, _, _ = _pair_softmax_pv(
                s, kvc_ref[:, C_W + p * LANES:C_W + (p + 1) * LANES])
            oc_ref[:, cols] = (out / den).astype(_BF16)

        return scores, finish

    _software_pipeline([window_tile(j) for j in range(AC_TILE // BLOCK)]
                       + [cross_tile(p) for p in range(MEM_HEADS // 2)])


def _attn_ac(za, qc, kvc, bias_a, batch, seq):
    t = za.shape[0]
    tiles = seq // AC_TILE
    blocks_per_tile = AC_TILE // BLOCK
    mem_len = kvc.shape[0] // batch
    row = lambda b, i: (b * tiles + i, 0)
    prev = lambda b, i: (jnp.maximum((b * tiles + i) * blocks_per_tile - 1, 0), 0)
    bias_shape = (None,) + bias_a.shape[1:]
    return pl.pallas_call(
        _ac_kernel,
        grid=(batch, tiles),
        in_specs=[
            pl.BlockSpec((AC_TILE, A_COLS), row),
            pl.BlockSpec((BLOCK, 2 * A_KV), prev),
            pl.BlockSpec((AC_TILE, C_W), row),
            pl.BlockSpec((mem_len, 2 * C_W), lambda b, i: (b, 0)),
            pl.BlockSpec(bias_shape, lambda b, i: (jnp.minimum(i, 1), 0, 0, 0)),
            pl.BlockSpec(bias_shape, lambda b, i: (1, 0, 0, 0)),
        ],
        out_specs=[
            pl.BlockSpec((AC_TILE, A_Q), row),
            pl.BlockSpec((AC_TILE, C_W), row),
        ],
        out_shape=[
            jax.ShapeDtypeStruct((t, A_Q), _BF16),
            jax.ShapeDtypeStruct((t, C_W), _BF16),
        ],
        compiler_params=_params(2),
        name="attn_window_cross",
    )(za, za, qc, kvc, bias_a, bias_a)


def _dil_kernel(*refs):
    cur_refs, prev_refs = refs[0:2 * DIL_GROUPS:2], refs[1:2 * DIL_GROUPS:2]
    bias0_ref, bias_ref, ob_ref, o_scr, l_scr = refs[2 * DIL_GROUPS:]
    low = _low_lanes()

    def tile(g, sb, c):
        rate = DIL_RATES[g]
        cur_ref, prev_ref = cur_refs[g], prev_refs[g]
        rows = slice(sb * BLOCK, (sb + 1) * BLOCK)
        bias = bias0_ref if sb == 0 else bias_ref
        k_cols = slice(c * B_COLS, c * B_COLS + B_G)
        v_cols = slice(c * B_COLS + B_G, c * B_COLS + 2 * B_G)
        q_cols = slice(c * B_COLS + 2 * B_G, (c + 1) * B_COLS)
        if rate == 1:
            out_rows = pl.ds(sb * BLOCK, BLOCK)
        else:
            out_rows = pl.ds(sb * BLOCK * rate + c, BLOCK, stride=rate)

        def window(cols):
            if sb == 0:
                prev = prev_ref[:, cols]
            else:
                prev = cur_ref[(sb - 1) * BLOCK:sb * BLOCK, cols]
            return jnp.concatenate([prev, cur_ref[rows, cols]], axis=0)

        def scores():
            return _pair_scores(cur_ref[rows, q_cols], window(k_cols), bias[g, 0], bias[g, 1])

        def finish(s):
            out, den, m_lo, m_hi = _pair_softmax_pv(s, window(v_cols))
            o_scr[g, out_rows, :] = out / den
            l_scr[g, out_rows, :] = jnp.where(low, m_lo, m_hi) + jnp.log2(den)

        return scores, finish

    _software_pipeline([tile(g, sb, c)
                        for g, rate in enumerate(DIL_RATES)
                        for sb in range(DIL_TILE // (BLOCK * rate))
                        for c in range(rate)])

    lse =[l_scr[g] for g in range(DIL_GROUPS)]
    top = jnp.maximum(jnp.maximum(lse[0], lse[1]), lse[2])
    w = [jnp.exp2(l - top) for l in lse]
    total = w[0] + w[1] + w[2]
    for g in range(DIL_GROUPS):
        ob_ref[:, g * B_G:(g + 1) * B_G] = (o_scr[g] * (w[g] / total)).astype(_BF16)


def _attn_dil(zb, bias_b, batch, seq):
    t = zb[0].shape[0]
    tiles = seq // DIL_TILE
    row = lambda b, i: (b * tiles + i, 0)
    bias_shape = (None,) + bias_b.shape[1:]
    views, specs = [], []
    for z, rate in zip(zb, DIL_RATES):
        view = z.reshape(t // rate, rate * B_COLS)
        blocks = DIL_TILE // rate // BLOCK
        views += [view, view]
        specs += [
            pl.BlockSpec((DIL_TILE // rate, rate * B_COLS), row),
            pl.BlockSpec((BLOCK, rate * B_COLS),
                         lambda b, i, blocks=blocks: (
                             jnp.maximum((b * tiles + i) * blocks - 1, 0), 0)),
        ]
    return pl.pallas_call(
        _dil_kernel,
        grid=(batch, tiles),
        in_specs=specs + [
            pl.BlockSpec(bias_shape, lambda b, i: (jnp.minimum(i, 1), 0, 0, 0, 0)),
            pl.BlockSpec(bias_shape, lambda b, i: (1, 0, 0, 0, 0)),
        ],
        out_specs=pl.BlockSpec((DIL_TILE, B_W), row),
        out_shape=jax.ShapeDtypeStruct((t, B_W), _BF16),
        scratch_shapes=[
            pltpu.VMEM((DIL_GROUPS, DIL_TILE, B_G), _F32),
            pltpu.VMEM((DIL_GROUPS, DIL_TILE, B_G), _F32),
        ],
        compiler_params=_params(2),
        name="attn_dilated",
    )(*views, bias_b, bias_b)


def _post_kernel(x_ref, oa_ref, ob_ref, oc_ref, g_ref, wg_ref, bg_ref,
                 wa_ref, wb_ref, wc_ref, wo_ref, o_ref):
    d = x_ref.shape[-1]

    def norm_in(s, st):
        st["h"] = _rms(x_ref[_sub_rows(s), :], g_ref[2:3, :]).astype(_BF16)

    def branch(n, o_branch_ref, w_branch_ref):
        def run(s, st):
            gate = jax.nn.sigmoid(_dot(st["h"], wg_ref[:, n * d:(n + 1) * d])
                                  + bg_ref[n:n + 1, :])
            term = gate * _dot(o_branch_ref[_sub_rows(s), :], w_branch_ref[...])
            st["merged"] = term if n == 0 else st["merged"] + term
        return run

    def project(s, st):
        st["y"] = _dot(st["merged"].astype(_BF16), wo_ref[...])

    def norm_out(s, st):
        rows = _sub_rows(s)
        o_ref[rows, :] = x_ref[rows, :] + _rms(st["y"], g_ref[3:4, :])

    _staggered(ROW_TILE // SUB_ROWS,
               [norm_in, branch(0, oa_ref, wa_ref), branch(1, ob_ref, wb_ref),
                branch(2, oc_ref, wc_ref), project, norm_out])


def _post(x2d, oa, ob, oc, gains, w_gate, b_gate, w_a, w_b, w_c, w_o, layer):
    t, d = x2d.shape
    row = lambda i: (i, 0)
    return pl.pallas_call(
        _post_kernel,
        grid=(t // ROW_TILE,),
        in_specs=[
            pl.BlockSpec((ROW_TILE, d), row),
            pl.BlockSpec((ROW_TILE, A_Q), row),
            pl.BlockSpec((ROW_TILE, B_W), row),
            pl.BlockSpec((ROW_TILE, C_W), row),
            _layer_resident(gains.shape, layer),
            _layer_resident(w_gate.shape, layer),
            _layer_resident(b_gate.shape, layer),
            _layer_resident(w_a.shape, layer),
            _layer_resident(w_b.shape, layer),
            _layer_resident(w_c.shape, layer),
            _layer_resident(w_o.shape, layer),
        ],
        out_specs=pl.BlockSpec((ROW_TILE, d), row),
        out_shape=jax.ShapeDtypeStruct((t, d), _F32),
        compiler_params=_params(1),
        name="mix_out_proj",
    )(x2d, oa, ob, oc, gains, w_gate, b_gate, w_a, w_b, w_c, w_o)


def _t5_bucket(dist):
    max_exact = N_BUCKETS // 2
    d = jnp.maximum(dist, 1).astype(_F32)
    large = max_exact + (jnp.log(d / max_exact) / math.log(MAX_DISTANCE / max_exact)
                         * (N_BUCKETS - max_exact)).astype(jnp.int32)
    large = jnp.minimum(large, N_BUCKETS - 1)
    return jnp.where(dist < max_exact, dist, large)


def _band_bias(table, head0, n_heads, rate, max_dist):
    row = jnp.arange(BLOCK)[:, None]
    col = jnp.arange(2 * BLOCK)[None, :]
    dist = row + BLOCK - col
    bucket = _t5_bucket(jnp.maximum(dist, 0) * rate)
    bias = jnp.zeros((n_heads, BLOCK, 2 * BLOCK), _F32)
    for n in range(N_BUCKETS):
        bias = jnp.where(bucket[None] == n,
                         table[n, head0:head0 + n_heads][:, None, None], bias)
    bias = bias * LOG2_E
    valid = (dist >= 0) & (dist <= max_dist)
    later = jnp.where(valid[None], bias, MASKED)
    first = jnp.where((valid & (col >= BLOCK))[None], bias, MASKED)
    return jnp.stack([first, later])


def _permute_in_proj(w):
    hd = HEAD_DIM
    qa, ka, va = w[..., 0:A_Q], w[..., A_Q:A_Q + A_KV], w[..., A_Q + A_KV:A_Q + 2 * A_KV]
    off = A_Q + 2 * A_KV
    qb, kb, vb = (w[..., off + n * B_W:off + (n + 1) * B_W] for n in range(3))
    qc = w[..., off + 3 * B_W:]
    grp = lambda w3, g: w3[..., g * B_G:(g + 1) * B_G]
    cols = [ka, va] + [qa[..., h * hd:(h + 1) * hd] for h in A_PAIR_ORDER] + [qc]
    for g in range(DIL_GROUPS):
        cols += [grp(kb, g), grp(vb, g), grp(qb, g)]
    return jnp.concatenate(cols, axis=-1)


def kernel(x, mem, rel_bias, norm_gain, mem_norm_gain, w_ffn1_in, w_ffn1_out, w_in,
           sinks, w_mem_kv, w_gate, b_gate, w_br_a, w_br_b, w_br_c, w_o,
           w_ffn2_in, w_ffn2_out):
    batch, seq, d = x.shape
    depth = norm_gain.shape[0]
    assert seq % DIL_TILE == 0 and (batch * seq) % ROW_TILE == 0

    table = rel_bias.astype(_F32)
    bias_a = _band_bias(table, 0, SWA_HEADS, 1, SWA_WINDOW - 1)
    bias_b = jnp.stack([
        _band_bias(table, SWA_HEADS + g * DIL_HEADS_PER_GROUP, DIL_HEADS_PER_GROUP,
                   DIL_RATES[g], DIL_WINDOWS[g] // DIL_RATES[g])
        for g in range(DIL_GROUPS)], axis=1)

    bf = lambda w: w.astype(_BF16)
    w1_in, w1_out, w2_in, w2_out = bf(w_ffn1_in), bf(w_ffn1_out), bf(w_ffn2_in), bf(w_ffn2_out)
    w_in_p = bf(_permute_in_proj(w_in))
    w_kv, w_g, w_b, w_c, w_out = bf(w_mem_kv), bf(w_gate), bf(w_br_b), bf(w_br_c), bf(w_o)
    w_a = bf(jnp.concatenate(
        [w_br_a[:, h * HEAD_DIM:(h + 1) * HEAD_DIM] for h in A_PAIR_ORDER], axis=1))

    group0 = A_COLS + C_W
    q_cols = [(2 * A_KV, group0)]
    q_cols += [(group0 + g * B_COLS + 2 * B_G, group0 + (g + 1) * B_COLS)
               for g in range(DIL_GROUPS)]
    q_scale = jnp.ones((1, w_in.shape[-1]), _F32)
    for c0, c1 in q_cols:
        q_scale = q_scale.at[:, c0:c1].set(Q_LOG2_SCALE)

    x2d = x.reshape(batch * seq, d)
    mem2d = mem.reshape(-1, d)
    for l in range(depth):
        x2d = _ffn(x2d, norm_gain, w1_in, w1_out, l, 0)
        za, qc, *zb = _pre(x2d, norm_gain, w_in_p, q_scale, l)
        kvc = _mem_kv(mem2d, mem_norm_gain[:, None, :], w_kv, l)
        bias_a_l = bias_a.at[:, :, :, 0].set(
            (sinks[l].astype(_F32) * LOG2_E)[None, :, None])
        oa, oc = _attn_ac(za, qc, kvc, bias_a_l, batch, seq)
        ob = _attn_dil(zb, bias_b, batch, seq)
        x2d = _post(x2d, oa, ob, oc, norm_gain, w_g, b_gate, w_a, w_b, w_c, w_out, l)
        x2d = _ffn(x2d, norm_gain, w2_in, w2_out, l, 4)
    return x2d.reshape(batch, seq, d)
```

```python
import functools
import math

import jax
import jax.numpy as jnp
from jax import lax
from jax.experimental import pallas as pl
from jax.experimental.pallas import tpu as pltpu

HEAD_DIM = 64
SWA_HEADS = 6
SWA_KV_HEADS = 2
SWA_GROUP = SWA_HEADS // SWA_KV_HEADS
SWA_WINDOW = 128
DIL_WINDOWS = (128, 512, 2048)
DIL_RATES = (1, 4, 16)
DIL_GROUPS = 3
DIL_HEADS_PER_GROUP = 2
MEM_HEADS = 4
N_BUCKETS = 32
MAX_DISTANCE = 2048
N_BRANCH = 3
EPS = 1e-6
BLOCK = 128
LANES = 128

A_Q = SWA_HEADS * HEAD_DIM
A_KV = SWA_KV_HEADS * HEAD_DIM
B_G = DIL_HEADS_PER_GROUP * HEAD_DIM
B_W = DIL_GROUPS * B_G
C_W = MEM_HEADS * HEAD_DIM
A_COLS = A_Q + 2 * A_KV
B_COLS = 3 * B_G
A_PAIR_ORDER = tuple(h for p in range(SWA_GROUP) for h in (p, p + SWA_GROUP))

LOG2_E = math.log2(math.e)
Q_LOG2_SCALE = HEAD_DIM ** -0.5 * LOG2_E
MASKED = -1e30

ROW_TILE = 1024
SUB_ROWS = 256
FFN_CHUNK = 512
AC_TILE = 512
DIL_TILE = BLOCK * max(DIL_RATES)
VMEM_LIMIT = 56 * 1024 * 1024

_F32 = jnp.float32
_BF16 = jnp.bfloat16


def _dot(a, b):
    return jnp.dot(a, b, preferred_element_type=_F32)


def _dot_nt(a, b):
    return lax.dot_general(a, b, (((1,), (1,)), ((), ())),
                           preferred_element_type=_F32)


def _rms(x, gain):
    ms = jnp.mean(x * x, axis=-1, keepdims=True)
    return x * lax.rsqrt(ms + EPS) * gain


def _resident(shape):
    return pl.BlockSpec(shape, lambda *_: (0,) * len(shape),
                        pipeline_mode=pl.Buffered(1))


def _layer_resident(shape, layer):
    zeros = (0,) * (len(shape) - 1)
    return pl.BlockSpec((None,) + tuple(shape[1:]), lambda *_: (layer,) + zeros,
                        pipeline_mode=pl.Buffered(1))


def _params(n_axes):
    return pltpu.CompilerParams(
        dimension_semantics=("arbitrary",) * n_axes,
        vmem_limit_bytes=VMEM_LIMIT)


def _staggered(n_sub, stages):
    state = [{} for _ in range(n_sub)]
    for t in range(n_sub + len(stages) - 1):
        for s in range(n_sub):
            if 0 <= t - s < len(stages):
                stages[t - s](s, state[s])


def _sub_rows(s):
    return slice(s * SUB_ROWS, (s + 1) * SUB_ROWS)


def _ffn_kernel(x_ref, g_ref, w_in_ref, w_out_ref, o_ref, *, ffn_dim, g0):
    def norm_in(s, st):
        st["h"] = _rms(x_ref[_sub_rows(s), :], g_ref[g0:g0 + 1, :]).astype(_BF16)

    bounds = [(c0, min(c0 + FFN_CHUNK, ffn_dim)) for c0 in range(0, ffn_dim, FFN_CHUNK)]

    def step(c):
        def run(s, st):
            prev = st.pop("ab", None)
            if c < len(bounds):
                c0, c1 = bounds[c]
                st["ab"] = (_dot(st["h"], w_in_ref[:, c0:c1]),
                            _dot(st["h"], w_in_ref[:, ffn_dim + c0:ffn_dim + c1]))
            if prev is not None:
                c0, c1 = bounds[c - 1]
                a, b = prev
                act = (a * jax.nn.sigmoid(a) * b).astype(_BF16)
                part = _dot(act, w_out_ref[c0:c1, :])
                st["y"] = part if "y" not in st else st["y"] + part
        return run

    def norm_out(s, st):
        rows = _sub_rows(s)
        o_ref[rows, :] = x_ref[rows, :] + 0.5 * _rms(st["y"], g_ref[g0 + 1:g0 + 2, :])

    _staggered(ROW_TILE // SUB_ROWS,
               [norm_in] + [step(c) for c in range(len(bounds) + 1)] + [norm_out])


def _ffn(x2d, gains, w_in, w_out, layer, g0):
    t, d = x2d.shape
    ffn_dim = w_out.shape[1]
    return pl.pallas_call(
        functools.partial(_ffn_kernel, ffn_dim=ffn_dim, g0=g0),
        grid=(t // ROW_TILE,),
        in_specs=[
            pl.BlockSpec((ROW_TILE, d), lambda i: (i, 0)),
            _layer_resident(gains.shape, layer),
            _layer_resident(w_in.shape, layer),
            _layer_resident(w_out.shape, layer),
        ],
        out_specs=pl.BlockSpec((ROW_TILE, d), lambda i: (i, 0)),
        out_shape=jax.ShapeDtypeStruct((t, d), _F32),
        compiler_params=_params(1),
        name="ffn",
    )(x2d, gains, w_in, w_out)


def _pre_kernel(x_ref, g_ref, w_ref, qs_ref, a_ref, c_ref, b0_ref, b1_ref, b2_ref, z_scr):
    def norm_in(s, st):
        st["h"] = _rms(x_ref[_sub_rows(s), :], g_ref[2:3, :]).astype(_BF16)

    first = (a_ref, c_ref, b0_ref)
    first_cols = sum(ref.shape[-1] for ref in first)

    def project_rows(s, st):
        z = _dot(st["h"], w_ref[:, 0:first_cols]) * qs_ref[:, 0:first_cols]
        col = 0
        for ref in first:
            ref[_sub_rows(s), :] = z[:, col:col + ref.shape[-1]].astype(_BF16)
            col += ref.shape[-1]

    def project_residues(s, st):
        z = _dot(st["h"], w_ref[:, first_cols:]) * qs_ref[:, first_cols:]
        n_slabs = z.shape[-1] // LANES
        for n in range(n_slabs):
            z_scr[s, n] = z[:, n * LANES:(n + 1) * LANES]
        for g, ref in ((1, b1_ref), (2, b2_ref)):
            rate = DIL_RATES[g]
            per = SUB_ROWS // rate
            for c in range(rate):
                for n in range(B_COLS // LANES):
                    slab = (g - 1) * (B_COLS // LANES) + n
                    ref[c, s * per:(s + 1) * per, n * LANES:(n + 1) * LANES] = (
                        z_scr[s, slab, pl.ds(c, per, stride=rate), :].astype(_BF16))

    _staggered(ROW_TILE // SUB_ROWS, [norm_in, project_residues, project_rows])


def _pre(x2d, gains, w_perm, q_scale, layer):
    t, d = x2d.shape
    row = lambda i: (i, 0)
    return pl.pallas_call(
        _pre_kernel,
        grid=(t // ROW_TILE,),
        in_specs=[
            pl.BlockSpec((ROW_TILE, d), row),
            _layer_resident(gains.shape, layer),
            _layer_resident(w_perm.shape, layer),
            _resident(q_scale.shape),
        ],
        out_specs=[
            pl.BlockSpec((ROW_TILE, A_COLS), row),
            pl.BlockSpec((ROW_TILE, C_W), row),
            pl.BlockSpec((ROW_TILE, B_COLS), row),
        ] + [pl.BlockSpec((r, ROW_TILE // r, B_COLS), lambda i: (0, i, 0))
             for r in DIL_RATES[1:]],
        out_shape=[
            jax.ShapeDtypeStruct((t, A_COLS), _BF16),
            jax.ShapeDtypeStruct((t, C_W), _BF16),
            jax.ShapeDtypeStruct((t, B_COLS), _BF16),
        ] + [jax.ShapeDtypeStruct((r, t // r, B_COLS), _BF16) for r in DIL_RATES[1:]],
        scratch_shapes=[
            pltpu.VMEM((ROW_TILE // SUB_ROWS, (DIL_GROUPS - 1) * B_COLS // LANES,
                        SUB_ROWS, LANES), _F32),
        ],
        compiler_params=_params(1),
        name="mix_in_proj",
    )(x2d, gains, w_perm, q_scale)


def _mem_kernel(m_ref, g_ref, w_ref, o_ref):
    h = _rms(m_ref[...], g_ref[...]).astype(_BF16)
    o_ref[...] = _dot(h, w_ref[...]).astype(_BF16)


def _mem_kv(mem2d, gains, w, layer):
    rows = mem2d.shape[0]
    return pl.pallas_call(
        _mem_kernel,
        grid=(1,),
        in_specs=[_resident(mem2d.shape),
                  _layer_resident(gains.shape, layer),
                  _layer_resident(w.shape, layer)],
        out_specs=pl.BlockSpec((rows, 2 * C_W), lambda i: (0, 0)),
        out_shape=jax.ShapeDtypeStruct((rows, 2 * C_W), _BF16),
        compiler_params=_params(1),
        name="mem_kv",
    )(mem2d, gains, w)


def _low_lanes():
    return lax.broadcasted_iota(jnp.int32, (1, LANES), 1) < HEAD_DIM


def _pair_scores(q, k, bias_lo=None, bias_hi=None):
    low = _low_lanes()
    zero = jnp.zeros((), k.dtype)
    if q.shape[0] <= k.shape[0]:
        s_lo = _dot_nt(jnp.where(low, q, zero), k)
        s_hi = _dot_nt(jnp.where(low, zero, q), k)
    else:
        s_lo = _dot_nt(q, jnp.where(low, k, zero))
        s_hi = _dot_nt(q, jnp.where(low, zero, k))
    if bias_lo is not None:
        s_lo = s_lo + bias_lo
        s_hi = s_hi + bias_hi
    return s_lo, s_hi


def _pair_softmax_pv(scores, v):
    s_lo, s_hi = scores
    low = _low_lanes()
    one = jnp.ones((), v.dtype)
    m_lo = jnp.max(s_lo, axis=-1, keepdims=True)
    m_hi = jnp.max(s_hi, axis=-1, keepdims=True)
    p_lo = jnp.exp2(s_lo - m_lo).astype(_BF16)
    p_hi = jnp.exp2(s_hi - m_hi).astype(_BF16)
    o_lo = _dot(p_lo, jnp.where(low, v, one))
    o_hi = _dot(p_hi, jnp.where(low, one, v))
    out = jnp.where(low, o_lo, o_hi)
    den = jnp.where(low, o_hi, o_lo)
    den = pltpu.roll(den, HEAD_DIM, axis=1)
    return out, den, m_lo, m_hi


PIPELINE_LOOKAHEAD = 2


def _software_pipeline(tiles):
    pending = []
    for n, (scores_fn, _) in enumerate(tiles):
        pending.append(scores_fn())
        if n >= PIPELINE_LOOKAHEAD:
            done = n - PIPELINE_LOOKAHEAD
            tiles[done][1](pending[done])
            pending[done] = None
    for done in range(max(len(tiles) - PIPELINE_LOOKAHEAD, 0), len(tiles)):
        tiles[done][1](pending[done])


def _ac_kernel(cur_ref, prev_ref, qc_ref, kvc_ref, bias0_ref, bias_ref, oa_ref, oc_ref):
    rows_a = SWA_GROUP * BLOCK
    sink_row = lax.broadcasted_iota(jnp.int32, (BLOCK, A_KV), 0) == 0

    def window_tile(j):
        rows = slice(j * BLOCK, (j + 1) * BLOCK)
        prev_rows = slice((j - 1) * BLOCK, j * BLOCK)
        bias = bias0_ref if j == 0 else bias_ref

        def kv_window(c0, c1):
            prev = prev_ref[:, c0:c1] if j == 0 else cur_ref[prev_rows, c0:c1]
            prev = jnp.where(sink_row, jnp.zeros((), prev.dtype), prev)
            return jnp.concatenate([prev, cur_ref[rows, c0:c1]], axis=0)

        def scores():
            q = jnp.concatenate(
                [cur_ref[rows, 2 * A_KV + p * LANES:2 * A_KV + (p + 1) * LANES]
                 for p in range(SWA_GROUP)], axis=0)
            return _pair_scores(q, kv_window(0, A_KV),
                                bias[0:SWA_GROUP].reshape(rows_a, 2 * BLOCK),
                                bias[SWA_GROUP:SWA_HEADS].reshape(rows_a, 2 * BLOCK))

        def finish(s):
            out, den, _, _ = _pair_softmax_pv(s, kv_window(A_KV, 2 * A_KV))
            o = out / den
            for p in range(SWA_GROUP):
                oa_ref[rows, p * LANES:(p + 1) * LANES] = (
                    o[p * BLOCK:(p + 1) * BLOCK]).astype(_BF16)

        return scores, finish

    def cross_tile(p):
        cols = slice(p * LANES, (p + 1) * LANES)

        def scores():
            return _pair_scores(qc_ref[:, cols], kvc_ref[:, cols])

        def finish(s):
            out, den, _, _ = _pair_softmax_pv(
                s, kvc_ref[:, C_W + p * LANES:C_W + (p + 1) * LANES])
            oc_ref[:, cols] = (out / den).astype(_BF16)

        return scores, finish

    _software_pipeline([window_tile(j) for j in range(AC_TILE // BLOCK)]
                       + [cross_tile(p) for p in range(MEM_HEADS // 2)])


def _attn_ac(za, qc, kvc, bias_a, batch, seq):
    t = za.shape[0]
    tiles = seq // AC_TILE
    blocks_per_tile = AC_TILE // BLOCK
    mem_len = kvc.shape[0] // batch
    row = lambda b, i: (b * tiles + i, 0)
    prev = lambda b, i: (jnp.maximum((b * tiles + i) * blocks_per_tile - 1, 0), 0)
    bias_shape = (None,) + bias_a.shape[1:]
    return pl.pallas_call(
        _ac_kernel,
        grid=(batch, tiles),
        in_specs=[
            pl.BlockSpec((AC_TILE, A_COLS), row),
            pl.BlockSpec((BLOCK, 2 * A_KV), prev),
            pl.BlockSpec((AC_TILE, C_W), row),
            pl.BlockSpec((mem_len, 2 * C_W), lambda b, i: (b, 0)),
            pl.BlockSpec(bias_shape, lambda b, i: (jnp.minimum(i, 1), 0, 0, 0)),
            pl.BlockSpec(bias_shape, lambda b, i: (1, 0, 0, 0)),
        ],
        out_specs=[
            pl.BlockSpec((AC_TILE, A_Q), row),
            pl.BlockSpec((AC_TILE, C_W), row),
        ],
        out_shape=[
            jax.ShapeDtypeStruct((t, A_Q), _BF16),
            jax.ShapeDtypeStruct((t, C_W), _BF16),
        ],
        compiler_params=_params(2),
        name="attn_window_cross",
    )(za, za, qc, kvc, bias_a, bias_a)


def _dil_kernel(*refs):
    cur_refs, prev_refs = refs[0:2 * DIL_GROUPS:2], refs[1:2 * DIL_GROUPS:2]
    bias0_ref, bias_ref, ob_ref, o_scr, l_scr = refs[2 * DIL_GROUPS:]
    low = _low_lanes()
    k_cols, v_cols, q_cols = (slice(n * B_G, (n + 1) * B_G) for n in range(3))

    def tile(g, sb, c):
        rate = DIL_RATES[g]
        rows = slice(sb * BLOCK, (sb + 1) * BLOCK)
        bias = bias0_ref if sb == 0 else bias_ref
        if rate == 1:
            out_rows = pl.ds(sb * BLOCK, BLOCK)
            cur = lambda r, cols: cur_refs[g][r, cols]
            prev = lambda cols: prev_refs[g][:, cols]
        else:
            out_rows = pl.ds(sb * BLOCK * rate + c, BLOCK, stride=rate)
            cur = lambda r, cols: cur_refs[g][c, r, cols]
            prev = lambda cols: prev_refs[g][c, :, cols]

        def window(cols):
            before = prev(cols) if sb == 0 else cur(slice((sb - 1) * BLOCK, sb * BLOCK), cols)
            return jnp.concatenate([before, cur(rows, cols)], axis=0)

        def scores():
            return _pair_scores(cur(rows, q_cols), window(k_cols), bias[g, 0], bias[g, 1])

        def finish(s):
            out, den, m_lo, m_hi = _pair_softmax_pv(s, window(v_cols))
            o_scr[g, out_rows, :] = out / den
            l_scr[g, out_rows, :] = jnp.where(low, m_lo, m_hi) + jnp.log2(den)

        return scores, finish

    _software_pipeline([tile(g, sb, c)
                        for g, rate in enumerate(DIL_RATES)
                        for sb in range(DIL_TILE // (BLOCK * rate))
                        for c in range(rate)])

    lse =[l_scr[g] for g in range(DIL_GROUPS)]
    top = jnp.maximum(jnp.maximum(lse[0], lse[1]), lse[2])
    w = [jnp.exp2(l - top) for l in lse]
    total = w[0] + w[1] + w[2]
    for g in range(DIL_GROUPS):
        ob_ref[:, g * B_G:(g + 1) * B_G] = (o_scr[g] * (w[g] / total)).astype(_BF16)


def _attn_dil(zb, bias_b, batch, seq):
    t = zb[0].shape[0]
    tiles = seq // DIL_TILE
    row = lambda b, i: (b * tiles + i, 0)
    bias_shape = (None,) + bias_b.shape[1:]
    operands, specs = [], []
    for z, rate in zip(zb, DIL_RATES):
        rows = DIL_TILE // rate
        prev_block = lambda b, i, n=rows // BLOCK: jnp.maximum((b * tiles + i) * n - 1, 0)
        operands += [z, z]
        if rate == 1:
            specs += [pl.BlockSpec((rows, B_COLS), row),
                      pl.BlockSpec((BLOCK, B_COLS),
                                   lambda b, i, f=prev_block: (f(b, i), 0))]
        else:
            specs += [pl.BlockSpec((rate, rows, B_COLS), lambda b, i: (0, b * tiles + i, 0)),
                      pl.BlockSpec((rate, BLOCK, B_COLS),
                                   lambda b, i, f=prev_block: (0, f(b, i), 0))]
    return pl.pallas_call(
        _dil_kernel,
        grid=(batch, tiles),
        in_specs=specs + [
            pl.BlockSpec(bias_shape, lambda b, i: (jnp.minimum(i, 1), 0, 0, 0, 0)),
            pl.BlockSpec(bias_shape, lambda b, i: (1, 0, 0, 0, 0)),
        ],
        out_specs=pl.BlockSpec((DIL_TILE, B_W), row),
        out_shape=jax.ShapeDtypeStruct((t, B_W), _BF16),
        scratch_shapes=[
            pltpu.VMEM((DIL_GROUPS, DIL_TILE, B_G), _F32),
            pltpu.VMEM((DIL_GROUPS, DIL_TILE, B_G), _F32),
        ],
        compiler_params=_params(2),
        name="attn_dilated",
    )(*operands, bias_b, bias_b)


def _post_kernel(x_ref, oa_ref, ob_ref, oc_ref, g_ref, wg_ref, bg_ref,
                 wa_ref, wb_ref, wc_ref, wo_ref, o_ref):
    d = x_ref.shape[-1]

    def norm_in(s, st):
        st["h"] = _rms(x_ref[_sub_rows(s), :], g_ref[2:3, :]).astype(_BF16)

    def branch(n, o_branch_ref, w_branch_ref):
        def run(s, st):
            gate = jax.nn.sigmoid(_dot(st["h"], wg_ref[:, n * d:(n + 1) * d])
                                  + bg_ref[n:n + 1, :])
            term = gate * _dot(o_branch_ref[_sub_rows(s), :], w_branch_ref[...])
            st["merged"] = term if n == 0 else st["merged"] + term
        return run

    def project(s, st):
        st["y"] = _dot(st["merged"].astype(_BF16), wo_ref[...])

    def norm_out(s, st):
        rows = _sub_rows(s)
        o_ref[rows, :] = x_ref[rows, :] + _rms(st["y"], g_ref[3:4, :])

    _staggered(ROW_TILE // SUB_ROWS,
               [norm_in, branch(0, oa_ref, wa_ref), branch(1, ob_ref, wb_ref),
                branch(2, oc_ref, wc_ref), project, norm_out])


def _post(x2d, oa, ob, oc, gains, w_gate, b_gate, w_a, w_b, w_c, w_o, layer):
    t, d = x2d.shape
    row = lambda i: (i, 0)
    return pl.pallas_call(
        _post_kernel,
        grid=(t // ROW_TILE,),
        in_specs=[
            pl.BlockSpec((ROW_TILE, d), row),
            pl.BlockSpec((ROW_TILE, A_Q), row),
            pl.BlockSpec((ROW_TILE, B_W), row),
            pl.BlockSpec((ROW_TILE, C_W), row),
            _layer_resident(gains.shape, layer),
            _layer_resident(w_gate.shape, layer),
            _layer_resident(b_gate.shape, layer),
            _layer_resident(w_a.shape, layer),
            _layer_resident(w_b.shape, layer),
            _layer_resident(w_c.shape, layer),
            _layer_resident(w_o.shape, layer),
        ],
        out_specs=pl.BlockSpec((ROW_TILE, d), row),
        out_shape=jax.ShapeDtypeStruct((t, d), _F32),
        compiler_params=_params(1),
        name="mix_out_proj",
    )(x2d, oa, ob, oc, gains, w_gate, b_gate, w_a, w_b, w_c, w_o)


def _t5_bucket(dist):
    max_exact = N_BUCKETS // 2
    d = jnp.maximum(dist, 1).astype(_F32)
    large = max_exact + (jnp.log(d / max_exact) / math.log(MAX_DISTANCE / max_exact)
                         * (N_BUCKETS - max_exact)).astype(jnp.int32)
    large = jnp.minimum(large, N_BUCKETS - 1)
    return jnp.where(dist < max_exact, dist, large)


def _band_bias(table, head0, n_heads, rate, max_dist):
    row = jnp.arange(BLOCK)[:, None]
    col = jnp.arange(2 * BLOCK)[None, :]
    dist = row + BLOCK - col
    bucket = _t5_bucket(jnp.maximum(dist, 0) * rate)
    bias = jnp.zeros((n_heads, BLOCK, 2 * BLOCK), _F32)
    for n in range(N_BUCKETS):
        bias = jnp.where(bucket[None] == n,
                         table[n, head0:head0 + n_heads][:, None, None], bias)
    bias = bias * LOG2_E
    valid = (dist >= 0) & (dist <= max_dist)
    later = jnp.where(valid[None], bias, MASKED)
    first = jnp.where((valid & (col >= BLOCK))[None], bias, MASKED)
    return jnp.stack([first, later])


def _permute_in_proj(w):
    hd = HEAD_DIM
    qa, ka, va = w[..., 0:A_Q], w[..., A_Q:A_Q + A_KV], w[..., A_Q + A_KV:A_Q + 2 * A_KV]
    off = A_Q + 2 * A_KV
    qb, kb, vb = (w[..., off + n * B_W:off + (n + 1) * B_W] for n in range(3))
    qc = w[..., off + 3 * B_W:]
    grp = lambda w3, g: w3[..., g * B_G:(g + 1) * B_G]
    cols = [ka, va] + [qa[..., h * hd:(h + 1) * hd] for h in A_PAIR_ORDER] + [qc]
    for g in range(DIL_GROUPS):
        cols += [grp(kb, g), grp(vb, g), grp(qb, g)]
    return jnp.concatenate(cols, axis=-1)


def kernel(x, mem, rel_bias, norm_gain, mem_norm_gain, w_ffn1_in, w_ffn1_out, w_in,
           sinks, w_mem_kv, w_gate, b_gate, w_br_a, w_br_b, w_br_c, w_o,
           w_ffn2_in, w_ffn2_out):
    batch, seq, d = x.shape
    depth = norm_gain.shape[0]
    assert seq % DIL_TILE == 0 and (batch * seq) % ROW_TILE == 0

    table = rel_bias.astype(_F32)
    bias_a = _band_bias(table, 0, SWA_HEADS, 1, SWA_WINDOW - 1)
    bias_b = jnp.stack([
        _band_bias(table, SWA_HEADS + g * DIL_HEADS_PER_GROUP, DIL_HEADS_PER_GROUP,
                   DIL_RATES[g], DIL_WINDOWS[g] // DIL_RATES[g])
        for g in range(DIL_GROUPS)], axis=1)

    bf = lambda w: w.astype(_BF16)
    w1_in, w1_out, w2_in, w2_out = bf(w_ffn1_in), bf(w_ffn1_out), bf(w_ffn2_in), bf(w_ffn2_out)
    w_in_p = bf(_permute_in_proj(w_in))
    w_kv, w_g, w_b, w_c, w_out = bf(w_mem_kv), bf(w_gate), bf(w_br_b), bf(w_br_c), bf(w_o)
    w_a = bf(jnp.concatenate(
        [w_br_a[:, h * HEAD_DIM:(h + 1) * HEAD_DIM] for h in A_PAIR_ORDER], axis=1))

    group0 = A_COLS + C_W
    q_cols = [(2 * A_KV, group0)]
    q_cols += [(group0 + g * B_COLS + 2 * B_G, group0 + (g + 1) * B_COLS)
               for g in range(DIL_GROUPS)]
    q_scale = jnp.ones((1, w_in.shape[-1]), _F32)
    for c0, c1 in q_cols:
        q_scale = q_scale.at[:, c0:c1].set(Q_LOG2_SCALE)

    x2d = x.reshape(batch * seq, d)
    mem2d = mem.reshape(-1, d)
    for l in range(depth):
        x2d = _ffn(x2d, norm_gain, w1_in, w1_out, l, 0)
        za, qc, *zb = _pre(x2d, norm_gain, w_in_p, q_scale, l)
        kvc = _mem_kv(mem2d, mem_norm_gain[:, None, :], w_kv, l)
        bias_a_l = bias_a.at[:, :, :, 0].set(
            (sinks[l].astype(_F32) * LOG2_E)[None, :, None])
        oa, oc = _attn_ac(za, qc, kvc, bias_a_l, batch, seq)
        ob = _attn_dil(zb, bias_b, batch, seq)
        x2d = _post(x2d, oa, ob, oc, norm_gain, w_g, b_gate, w_a, w_b, w_c, w_out, l)
        x2d = _ffn(x2d, norm_gain, w2_in, w2_out, l, 4)
    return x2d.reshape(batch, seq, d)
```

```python
import functools
import math

import jax
import jax.numpy as jnp
from jax import lax
from jax.experimental import pallas as pl
from jax.experimental.pallas import tpu as pltpu

HEAD_DIM = 64
SWA_HEADS = 6
SWA_KV_HEADS = 2
SWA_GROUP = SWA_HEADS // SWA_KV_HEADS
SWA_WINDOW = 128
DIL_WINDOWS = (128, 512, 2048)
DIL_RATES = (1, 4, 16)
DIL_GROUPS = 3
DIL_HEADS_PER_GROUP = 2
MEM_HEADS = 4
N_BUCKETS = 32
MAX_DISTANCE = 2048
N_BRANCH = 3
EPS = 1e-6
BLOCK = 128
LANES = 128

A_Q = SWA_HEADS * HEAD_DIM
A_KV = SWA_KV_HEADS * HEAD_DIM
B_G = DIL_HEADS_PER_GROUP * HEAD_DIM
B_W = DIL_GROUPS * B_G
C_W = MEM_HEADS * HEAD_DIM
A_COLS = A_Q + 2 * A_KV
B_COLS = 3 * B_G
A_PAIR_ORDER = tuple(h for p in range(SWA_GROUP) for h in (p, p + SWA_GROUP))

LOG2_E = math.log2(math.e)
Q_LOG2_SCALE = HEAD_DIM ** -0.5 * LOG2_E
MASKED = -1e30

ROW_TILE = 1024
SUB_ROWS = 256
FFN_CHUNK = 512
DIL_TILE = BLOCK * max(DIL_RATES)
VMEM_LIMIT = 56 * 1024 * 1024

_F32 = jnp.float32
_BF16 = jnp.bfloat16


def _dot(a, b):
    return jnp.dot(a, b, preferred_element_type=_F32)


def _dot_nt(a, b):
    return lax.dot_general(a, b, (((1,), (1,)), ((), ())),
                           preferred_element_type=_F32)


def _rms(x, gain):
    ms = jnp.mean(x * x, axis=-1, keepdims=True)
    return x * lax.rsqrt(ms + EPS) * gain


def _resident(shape):
    return pl.BlockSpec(shape, lambda *_: (0,) * len(shape),
                        pipeline_mode=pl.Buffered(1))


def _layer_resident(shape, layer):
    zeros = (0,) * (len(shape) - 1)
    return pl.BlockSpec((None,) + tuple(shape[1:]), lambda *_: (layer,) + zeros,
                        pipeline_mode=pl.Buffered(1))


def _params(n_axes):
    return pltpu.CompilerParams(
        dimension_semantics=("arbitrary",) * n_axes,
        vmem_limit_bytes=VMEM_LIMIT)


def _staggered(n_sub, stages):
    state = [{} for _ in range(n_sub)]
    for t in range(n_sub + len(stages) - 1):
        for s in range(n_sub):
            if 0 <= t - s < len(stages):
                stages[t - s](s, state[s])


def _sub_rows(s):
    return slice(s * SUB_ROWS, (s + 1) * SUB_ROWS)


def _ffn_kernel(x_ref, g_ref, w_in_ref, w_out_ref, o_ref, *, ffn_dim, g0):
    def norm_in(s, st):
        st["h"] = _rms(x_ref[_sub_rows(s), :], g_ref[g0:g0 + 1, :]).astype(_BF16)

    bounds = [(c0, min(c0 + FFN_CHUNK, ffn_dim)) for c0 in range(0, ffn_dim, FFN_CHUNK)]

    def step(c):
        def run(s, st):
            prev = st.pop("ab", None)
            if c < len(bounds):
                c0, c1 = bounds[c]
                st["ab"] = (_dot(st["h"], w_in_ref[:, c0:c1]),
                            _dot(st["h"], w_in_ref[:, ffn_dim + c0:ffn_dim + c1]))
            if prev is not None:
                c0, c1 = bounds[c - 1]
                a, b = prev
                act = (a * jax.nn.sigmoid(a) * b).astype(_BF16)
                part = _dot(act, w_out_ref[c0:c1, :])
                st["y"] = part if "y" not in st else st["y"] + part
        return run

    def norm_out(s, st):
        rows = _sub_rows(s)
        o_ref[rows, :] = x_ref[rows, :] + 0.5 * _rms(st["y"], g_ref[g0 + 1:g0 + 2, :])

    _staggered(ROW_TILE // SUB_ROWS,
               [norm_in] + [step(c) for c in range(len(bounds) + 1)] + [norm_out])


def _ffn(x2d, gains, w_in, w_out, layer, g0):
    t, d = x2d.shape
    ffn_dim = w_out.shape[1]
    return pl.pallas_call(
        functools.partial(_ffn_kernel, ffn_dim=ffn_dim, g0=g0),
        grid=(t // ROW_TILE,),
        in_specs=[
            pl.BlockSpec((ROW_TILE, d), lambda i: (i, 0)),
            _layer_resident(gains.shape, layer),
            _layer_resident(w_in.shape, layer),
            _layer_resident(w_out.shape, layer),
        ],
        out_specs=pl.BlockSpec((ROW_TILE, d), lambda i: (i, 0)),
        out_shape=jax.ShapeDtypeStruct((t, d), _F32),
        compiler_params=_params(1),
        name="ffn",
    )(x2d, gains, w_in, w_out)


def _pre_kernel(x_ref, g_ref, w_ref, qs_ref, a_ref, c_ref, b0_ref, b1_ref, b2_ref, z_scr):
    def norm_in(s, st):
        st["h"] = _rms(x_ref[_sub_rows(s), :], g_ref[2:3, :]).astype(_BF16)

    first = (a_ref, c_ref, b0_ref)
    first_cols = sum(ref.shape[-1] for ref in first)

    def project_rows(s, st):
        z = _dot(st["h"], w_ref[:, 0:first_cols]) * qs_ref[:, 0:first_cols]
        col = 0
        for ref in first:
            ref[_sub_rows(s), :] = z[:, col:col + ref.shape[-1]].astype(_BF16)
            col += ref.shape[-1]

    def project_residues(s, st):
        z = _dot(st["h"], w_ref[:, first_cols:]) * qs_ref[:, first_cols:]
        n_slabs = z.shape[-1] // LANES
        for n in range(n_slabs):
            z_scr[s, n] = z[:, n * LANES:(n + 1) * LANES]
        for g, ref in ((1, b1_ref), (2, b2_ref)):
            rate = DIL_RATES[g]
            per = SUB_ROWS // rate
            for c in range(rate):
                for n in range(B_COLS // LANES):
                    slab = (g - 1) * (B_COLS // LANES) + n
                    ref[c, s * per:(s + 1) * per, n * LANES:(n + 1) * LANES] = (
                        z_scr[s, slab, pl.ds(c, per, stride=rate), :].astype(_BF16))

    _staggered(ROW_TILE // SUB_ROWS, [norm_in, project_residues, project_rows])


def _pre(x2d, gains, w_perm, q_scale, layer):
    t, d = x2d.shape
    row = lambda i: (i, 0)
    return pl.pallas_call(
        _pre_kernel,
        grid=(t // ROW_TILE,),
        in_specs=[
            pl.BlockSpec((ROW_TILE, d), row),
            _layer_resident(gains.shape, layer),
            _layer_resident(w_perm.shape, layer),
            _resident(q_scale.shape),
        ],
        out_specs=[
            pl.BlockSpec((ROW_TILE, A_COLS), row),
            pl.BlockSpec((ROW_TILE, C_W), row),
            pl.BlockSpec((ROW_TILE, B_COLS), row),
        ] + [pl.BlockSpec((r, ROW_TILE // r, B_COLS), lambda i: (0, i, 0))
             for r in DIL_RATES[1:]],
        out_shape=[
            jax.ShapeDtypeStruct((t, A_COLS), _BF16),
            jax.ShapeDtypeStruct((t, C_W), _BF16),
            jax.ShapeDtypeStruct((t, B_COLS), _BF16),
        ] + [jax.ShapeDtypeStruct((r, t // r, B_COLS), _BF16) for r in DIL_RATES[1:]],
        scratch_shapes=[
            pltpu.VMEM((ROW_TILE // SUB_ROWS, (DIL_GROUPS - 1) * B_COLS // LANES,
                        SUB_ROWS, LANES), _F32),
        ],
        compiler_params=_params(1),
        name="mix_in_proj",
    )(x2d, gains, w_perm, q_scale)


def _mem_kernel(m_ref, g_ref, w_ref, o_ref):
    h = _rms(m_ref[...], g_ref[...]).astype(_BF16)
    o_ref[...] = _dot(h, w_ref[...]).astype(_BF16)


def _mem_kv(mem2d, gains, w, layer):
    rows = mem2d.shape[0]
    return pl.pallas_call(
        _mem_kernel,
        grid=(1,),
        in_specs=[_resident(mem2d.shape),
                  _layer_resident(gains.shape, layer),
                  _layer_resident(w.shape, layer)],
        out_specs=pl.BlockSpec((rows, 2 * C_W), lambda i: (0, 0)),
        out_shape=jax.ShapeDtypeStruct((rows, 2 * C_W), _BF16),
        compiler_params=_params(1),
        name="mem_kv",
    )(mem2d, gains, w)


def _low_lanes():
    return lax.broadcasted_iota(jnp.int32, (1, LANES), 1) < HEAD_DIM


def _pair_scores(q, k, bias_lo=None, bias_hi=None):
    low = _low_lanes()
    zero = jnp.zeros((), k.dtype)
    if q.shape[0] <= k.shape[0]:
        s_lo = _dot_nt(jnp.where(low, q, zero), k)
        s_hi = _dot_nt(jnp.where(low, zero, q), k)
    else:
        s_lo = _dot_nt(q, jnp.where(low, k, zero))
        s_hi = _dot_nt(q, jnp.where(low, zero, k))
    if bias_lo is not None:
        s_lo = s_lo + bias_lo
        s_hi = s_hi + bias_hi
    return s_lo, s_hi


def _pair_softmax_pv(scores, v):
    s_lo, s_hi = scores
    low = _low_lanes()
    one = jnp.ones((), v.dtype)
    m_lo = jnp.max(s_lo, axis=-1, keepdims=True)
    m_hi = jnp.max(s_hi, axis=-1, keepdims=True)
    p_lo = jnp.exp2(s_lo - m_lo).astype(_BF16)
    p_hi = jnp.exp2(s_hi - m_hi).astype(_BF16)
    o_lo = _dot(p_lo, jnp.where(low, v, one))
    o_hi = _dot(p_hi, jnp.where(low, one, v))
    out = jnp.where(low, o_lo, o_hi)
    den = jnp.where(low, o_hi, o_lo)
    den = pltpu.roll(den, HEAD_DIM, axis=1)
    return out, den, m_lo, m_hi


PIPELINE_LOOKAHEAD = 3


def _software_pipeline(tiles):
    pending = []
    for n, (scores_fn, _) in enumerate(tiles):
        pending.append(scores_fn())
        if n >= PIPELINE_LOOKAHEAD:
            done = n - PIPELINE_LOOKAHEAD
            tiles[done][1](pending[done])
            pending[done] = None
    for done in range(max(len(tiles) - PIPELINE_LOOKAHEAD, 0), len(tiles)):
        tiles[done][1](pending[done])


def _window_cross_tiles(cur_ref, prev_ref, qc_ref, kvc_ref, bias0_ref, bias_ref,
                        oa_ref, oc_ref):
    rows_a = SWA_GROUP * BLOCK
    sink_row = lax.broadcasted_iota(jnp.int32, (BLOCK, A_KV), 0) == 0

    def window_tile(j):
        rows = slice(j * BLOCK, (j + 1) * BLOCK)
        prev_rows = slice((j - 1) * BLOCK, j * BLOCK)
        bias = bias0_ref if j == 0 else bias_ref

        def kv_window(c0, c1):
            prev = prev_ref[:, c0:c1] if j == 0 else cur_ref[prev_rows, c0:c1]
            prev = jnp.where(sink_row, jnp.zeros((), prev.dtype), prev)
            return jnp.concatenate([prev, cur_ref[rows, c0:c1]], axis=0)

        def scores():
            q = jnp.concatenate(
                [cur_ref[rows, 2 * A_KV + p * LANES:2 * A_KV + (p + 1) * LANES]
                 for p in range(SWA_GROUP)], axis=0)
            return _pair_scores(q, kv_window(0, A_KV),
                                bias[0:SWA_GROUP].reshape(rows_a, 2 * BLOCK),
                                bias[SWA_GROUP:SWA_HEADS].reshape(rows_a, 2 * BLOCK))

        def finish(s):
            out, den, _, _ = _pair_softmax_pv(s, kv_window(A_KV, 2 * A_KV))
            o = out / den
            for p in range(SWA_GROUP):
                oa_ref[rows, p * LANES:(p + 1) * LANES] = (
                    o[p * BLOCK:(p + 1) * BLOCK]).astype(_BF16)

        return scores, finish

    def cross_tile(p, rows):
        cols = slice(p * LANES, (p + 1) * LANES)

        def scores():
            return _pair_scores(qc_ref[rows, cols], kvc_ref[:, cols])

        def finish(s):
            out, den, _, _ = _pair_softmax_pv(
                s, kvc_ref[:, C_W + p * LANES:C_W + (p + 1) * LANES])
            oc_ref[rows, cols] = (out / den).astype(_BF16)

        return scores, finish

    return window_tile, cross_tile


def _dil_kernel(*refs):
    cur_refs, prev_refs = refs[0:2 * DIL_GROUPS:2], refs[1:2 * DIL_GROUPS:2]
    bias0_ref, bias_ref, ob_ref, o_scr, d_scr, m_scr = refs[2 * DIL_GROUPS:]
    low = _low_lanes()
    k_cols, v_cols, q_cols = (slice(n * B_G, (n + 1) * B_G) for n in range(3))

    def tile(g, sb, c):
        rate = DIL_RATES[g]
        rows = slice(sb * BLOCK, (sb + 1) * BLOCK)
        bias = bias0_ref if sb == 0 else bias_ref
        if rate == 1:
            out_rows = pl.ds(sb * BLOCK, BLOCK)
            cur = lambda r, cols: cur_refs[g][r, cols]
            prev = lambda cols: prev_refs[g][:, cols]
        else:
            out_rows = pl.ds(sb * BLOCK * rate + c, BLOCK, stride=rate)
            cur = lambda r, cols: cur_refs[g][c, r, cols]
            prev = lambda cols: prev_refs[g][c, :, cols]

        def window(cols):
            before = prev(cols) if sb == 0 else cur(slice((sb - 1) * BLOCK, sb * BLOCK), cols)
            return jnp.concatenate([before, cur(rows, cols)], axis=0)

        def scores():
            return _pair_scores(cur(rows, q_cols), window(k_cols), bias[g, 0], bias[g, 1])

        def finish(s):
            out, den, m_lo, m_hi = _pair_softmax_pv(s, window(v_cols))
            o_scr[g, out_rows, :] = out
            d_scr[g, out_rows, :] = den
            m_scr[g, out_rows, :] = jnp.where(low, m_lo, m_hi)

        return scores, finish

    _software_pipeline([tile(g, sb, c)
                        for g, rate in enumerate(DIL_RATES)
                        for sb in range(DIL_TILE // (BLOCK * rate))
                        for c in range(rate)])

    top = functools.reduce(jnp.maximum, [m_scr[g] for g in range(DIL_GROUPS)])
    scale = [jnp.exp2(m_scr[g] - top) for g in range(DIL_GROUPS)]
    total = sum(d_scr[g] * scale[g] for g in range(DIL_GROUPS))
    for g in range(DIL_GROUPS):
        ob_ref[:, g * B_G:(g + 1) * B_G] = (o_scr[g] * (scale[g] / total)).astype(_BF16)


def _attn_dil(zb, bias_b, batch, seq):
    t = zb[0].shape[0]
    tiles = seq // DIL_TILE
    row = lambda b, i: (b * tiles + i, 0)
    bias_shape = (None,) + bias_b.shape[1:]
    operands, specs = [], []
    for z, rate in zip(zb, DIL_RATES):
        rows = DIL_TILE // rate
        prev_block = lambda b, i, n=rows // BLOCK: jnp.maximum((b * tiles + i) * n - 1, 0)
        operands += [z, z]
        if rate == 1:
            specs += [pl.BlockSpec((rows, B_COLS), row),
                      pl.BlockSpec((BLOCK, B_COLS),
                                   lambda b, i, f=prev_block: (f(b, i), 0))]
        else:
            specs += [pl.BlockSpec((rate, rows, B_COLS), lambda b, i: (0, b * tiles + i, 0)),
                      pl.BlockSpec((rate, BLOCK, B_COLS),
                                   lambda b, i, f=prev_block: (0, f(b, i), 0))]
    return pl.pallas_call(
        _dil_kernel,
        grid=(batch, tiles),
        in_specs=specs + [
            pl.BlockSpec(bias_shape, lambda b, i: (jnp.minimum(i, 1), 0, 0, 0, 0)),
            pl.BlockSpec(bias_shape, lambda b, i: (1, 0, 0, 0, 0)),
        ],
        out_specs=pl.BlockSpec((DIL_TILE, B_W), row),
        out_shape=jax.ShapeDtypeStruct((t, B_W), _BF16),
        scratch_shapes=[
            pltpu.VMEM((DIL_GROUPS, DIL_TILE, B_G), _F32),
            pltpu.VMEM((DIL_GROUPS, DIL_TILE, B_G), _F32),
            pltpu.VMEM((DIL_GROUPS, DIL_TILE, B_G), _F32),
        ],
        compiler_params=_params(2),
        name="attn_dilated",
    )(*operands, bias_b, bias_b)


def _post_kernel(x_ref, za_ref, za_prev_ref, qc_ref, kvc_ref, bias0_ref, bias_ref, ob_ref,
                 g_ref, wg_ref, bg_ref, wa_ref, wb_ref, wc_ref, wo_ref, o_ref,
                 oa_ref, oc_ref):
    d = x_ref.shape[-1]
    window_tile, cross_tile = _window_cross_tiles(
        za_ref, za_prev_ref, qc_ref, kvc_ref, bias0_ref, bias_ref, oa_ref, oc_ref)
    blocks_per_sub = SUB_ROWS // BLOCK

    def attend(s, st):
        _software_pipeline(
            [window_tile(j) for j in range(s * blocks_per_sub, (s + 1) * blocks_per_sub)]
            + [cross_tile(p, _sub_rows(s)) for p in range(MEM_HEADS // 2)])

    def norm_in(s, st):
        st["h"] = _rms(x_ref[_sub_rows(s), :], g_ref[2:3, :]).astype(_BF16)

    def branch(n, o_branch_ref, w_branch_ref):
        def run(s, st):
            gate = jax.nn.sigmoid(_dot(st["h"], wg_ref[:, n * d:(n + 1) * d])
                                  + bg_ref[n:n + 1, :])
            term = gate * _dot(o_branch_ref[_sub_rows(s), :], w_branch_ref[...])
            st["merged"] = term if n == 0 else st["merged"] + term
        return run

    def project(s, st):
        st["y"] = _dot(st["merged"].astype(_BF16), wo_ref[...])

    def norm_out(s, st):
        rows = _sub_rows(s)
        o_ref[rows, :] = x_ref[rows, :] + _rms(st["y"], g_ref[3:4, :])

    _staggered(ROW_TILE // SUB_ROWS,
               [attend, norm_in, branch(0, oa_ref, wa_ref), branch(1, ob_ref, wb_ref),
                branch(2, oc_ref, wc_ref), project, norm_out])


def _post(x2d, za, qc, kvc, bias_a, ob, gains, w_gate, b_gate, w_a, w_b, w_c, w_o,
          layer, batch, seq):
    t, d = x2d.shape
    tiles = seq // ROW_TILE
    mem_len = kvc.shape[0] // batch
    row = lambda b, i: (b * tiles + i, 0)
    bias_shape = (None,) + bias_a.shape[1:]
    return pl.pallas_call(
        _post_kernel,
        grid=(batch, tiles),
        in_specs=[
            pl.BlockSpec((ROW_TILE, d), row),
            pl.BlockSpec((ROW_TILE, A_COLS), row),
            pl.BlockSpec((BLOCK, 2 * A_KV), lambda b, i: (
                jnp.maximum((b * tiles + i) * (ROW_TILE // BLOCK) - 1, 0), 0)),
            pl.BlockSpec((ROW_TILE, C_W), row),
            pl.BlockSpec((mem_len, 2 * C_W), lambda b, i: (b, 0)),
            pl.BlockSpec(bias_shape, lambda b, i: (jnp.minimum(i, 1), 0, 0, 0)),
            pl.BlockSpec(bias_shape, lambda b, i: (1, 0, 0, 0)),
            pl.BlockSpec((ROW_TILE, B_W), row),
            _layer_resident(gains.shape, layer),
            _layer_resident(w_gate.shape, layer),
            _layer_resident(b_gate.shape, layer),
            _layer_resident(w_a.shape, layer),
            _layer_resident(w_b.shape, layer),
            _layer_resident(w_c.shape, layer),
            _layer_resident(w_o.shape, layer),
        ],
        out_specs=pl.BlockSpec((ROW_TILE, d), row),
        out_shape=jax.ShapeDtypeStruct((t, d), _F32),
        scratch_shapes=[pltpu.VMEM((ROW_TILE, A_Q), _BF16),
                        pltpu.VMEM((ROW_TILE, C_W), _BF16)],
        compiler_params=_params(2),
        name="attn_window_cross_out_proj",
    )(x2d, za, za, qc, kvc, bias_a, bias_a, ob, gains, w_gate, b_gate, w_a, w_b, w_c, w_o)


def _t5_bucket(dist):
    max_exact = N_BUCKETS // 2
    d = jnp.maximum(dist, 1).astype(_F32)
    large = max_exact + (jnp.log(d / max_exact) / math.log(MAX_DISTANCE / max_exact)
                         * (N_BUCKETS - max_exact)).astype(jnp.int32)
    large = jnp.minimum(large, N_BUCKETS - 1)
    return jnp.where(dist < max_exact, dist, large)


def _band_bias(table, head0, n_heads, rate, max_dist):
    row = jnp.arange(BLOCK)[:, None]
    col = jnp.arange(2 * BLOCK)[None, :]
    dist = row + BLOCK - col
    bucket = _t5_bucket(jnp.maximum(dist, 0) * rate)
    bias = jnp.zeros((n_heads, BLOCK, 2 * BLOCK), _F32)
    for n in range(N_BUCKETS):
        bias = jnp.where(bucket[None] == n,
                         table[n, head0:head0 + n_heads][:, None, None], bias)
    bias = bias * LOG2_E
    valid = (dist >= 0) & (dist <= max_dist)
    later = jnp.where(valid[None], bias, MASKED)
    first = jnp.where((valid & (col >= BLOCK))[None], bias, MASKED)
    return jnp.stack([first, later])


def _permute_in_proj(w):
    hd = HEAD_DIM
    qa, ka, va = w[..., 0:A_Q], w[..., A_Q:A_Q + A_KV], w[..., A_Q + A_KV:A_Q + 2 * A_KV]
    off = A_Q + 2 * A_KV
    qb, kb, vb = (w[..., off + n * B_W:off + (n + 1) * B_W] for n in range(3))
    qc = w[..., off + 3 * B_W:]
    grp = lambda w3, g: w3[..., g * B_G:(g + 1) * B_G]
    cols = [ka, va] + [qa[..., h * hd:(h + 1) * hd] for h in A_PAIR_ORDER] + [qc]
    for g in range(DIL_GROUPS):
        cols += [grp(kb, g), grp(vb, g), grp(qb, g)]
    return jnp.concatenate(cols, axis=-1)


def kernel(x, mem, rel_bias, norm_gain, mem_norm_gain, w_ffn1_in, w_ffn1_out, w_in,
           sinks, w_mem_kv, w_gate, b_gate, w_br_a, w_br_b, w_br_c, w_o,
           w_ffn2_in, w_ffn2_out):
    batch, seq, d = x.shape
    depth = norm_gain.shape[0]
    assert seq % DIL_TILE == 0 and (batch * seq) % ROW_TILE == 0

    table = rel_bias.astype(_F32)
    bias_a = _band_bias(table, 0, SWA_HEADS, 1, SWA_WINDOW - 1)
    bias_b = jnp.stack([
        _band_bias(table, SWA_HEADS + g * DIL_HEADS_PER_GROUP, DIL_HEADS_PER_GROUP,
                   DIL_RATES[g], DIL_WINDOWS[g] // DIL_RATES[g])
        for g in range(DIL_GROUPS)], axis=1)

    bf = lambda w: w.astype(_BF16)
    w1_in, w1_out, w2_in, w2_out = bf(w_ffn1_in), bf(w_ffn1_out), bf(w_ffn2_in), bf(w_ffn2_out)
    w_in_p = bf(_permute_in_proj(w_in))
    w_kv, w_g, w_b, w_c, w_out = bf(w_mem_kv), bf(w_gate), bf(w_br_b), bf(w_br_c), bf(w_o)
    w_a = bf(jnp.concatenate(
        [w_br_a[:, h * HEAD_DIM:(h + 1) * HEAD_DIM] for h in A_PAIR_ORDER], axis=1))

    group0 = A_COLS + C_W
    q_cols = [(2 * A_KV, group0)]
    q_cols += [(group0 + g * B_COLS + 2 * B_G, group0 + (g + 1) * B_COLS)
               for g in range(DIL_GROUPS)]
    q_scale = jnp.ones((1, w_in.shape[-1]), _F32)
    for c0, c1 in q_cols:
        q_scale = q_scale.at[:, c0:c1].set(Q_LOG2_SCALE)

    x2d = x.reshape(batch * seq, d)
    mem2d = mem.reshape(-1, d)
    for l in range(depth):
        x2d = _ffn(x2d, norm_gain, w1_in, w1_out, l, 0)
        za, qc, *zb = _pre(x2d, norm_gain, w_in_p, q_scale, l)
        kvc = _mem_kv(mem2d, mem_norm_gain[:, None, :], w_kv, l)
        bias_a_l = bias_a.at[:, :, :, 0].set(
            (sinks[l].astype(_F32) * LOG2_E)[None, :, None])
        ob = _attn_dil(zb, bias_b, batch, seq)
        x2d = _post(x2d, za, qc, kvc, bias_a_l, ob, norm_gain, w_g, b_gate,
                    w_a, w_b, w_c, w_out, l, batch, seq)
        x2d = _ffn(x2d, norm_gain, w2_in, w2_out, l, 4)
    return x2d.reshape(batch, seq, d)
```

```python
import functools
import math

import jax
import jax.numpy as jnp
from jax import lax
from jax.experimental import pallas as pl
from jax.experimental.pallas import tpu as pltpu

HEAD_DIM = 64
SWA_HEADS = 6
SWA_KV_HEADS = 2
SWA_GROUP = SWA_HEADS // SWA_KV_HEADS
SWA_WINDOW = 128
DIL_WINDOWS = (128, 512, 2048)
DIL_RATES = (1, 4, 16)
DIL_GROUPS = 3
DIL_HEADS_PER_GROUP = 2
MEM_HEADS = 4
N_BUCKETS = 32
MAX_DISTANCE = 2048
N_BRANCH = 3
EPS = 1e-6
BLOCK = 128
LANES = 128

A_Q = SWA_HEADS * HEAD_DIM
A_KV = SWA_KV_HEADS * HEAD_DIM
B_G = DIL_HEADS_PER_GROUP * HEAD_DIM
B_W = DIL_GROUPS * B_G
C_W = MEM_HEADS * HEAD_DIM
A_COLS = A_Q + 2 * A_KV
B_COLS = 3 * B_G
A_PAIR_ORDER = tuple(h for p in range(SWA_GROUP) for h in (p, p + SWA_GROUP))

LOG2_E = math.log2(math.e)
Q_LOG2_SCALE = HEAD_DIM ** -0.5 * LOG2_E
MASKED = -1e30

ROW_TILE = 1024
SUB_ROWS = 256
FFN_CHUNK = 512
DIL_TILE = BLOCK * max(DIL_RATES)
VMEM_LIMIT = 56 * 1024 * 1024

_F32 = jnp.float32
_BF16 = jnp.bfloat16


def _dot(a, b):
    return jnp.dot(a, b, preferred_element_type=_F32)


def _dot_nt(a, b):
    return lax.dot_general(a, b, (((1,), (1,)), ((), ())),
                           preferred_element_type=_F32)


def _rms(x, gain):
    ms = jnp.mean(x * x, axis=-1, keepdims=True)
    return x * lax.rsqrt(ms + EPS) * gain


def _resident(shape):
    return pl.BlockSpec(shape, lambda *_: (0,) * len(shape),
                        pipeline_mode=pl.Buffered(1))


def _layer_resident(shape, layer):
    zeros = (0,) * (len(shape) - 1)
    return pl.BlockSpec((None,) + tuple(shape[1:]), lambda *_: (layer,) + zeros,
                        pipeline_mode=pl.Buffered(1))


def _params(n_axes):
    return pltpu.CompilerParams(
        dimension_semantics=("arbitrary",) * n_axes,
        vmem_limit_bytes=VMEM_LIMIT)


def _staggered(n_sub, stages):
    state = [{} for _ in range(n_sub)]
    for t in range(n_sub + len(stages) - 1):
        for s in range(n_sub):
            if 0 <= t - s < len(stages):
                stages[t - s](s, state[s])


def _sub_rows(s):
    return slice(s * SUB_ROWS, (s + 1) * SUB_ROWS)


def _ffn_kernel(x_ref, g_ref, w_in_ref, w_out_ref, o_ref, *, ffn_dim, g0):
    def norm_in(s, st):
        st["h"] = _rms(x_ref[_sub_rows(s), :], g_ref[g0:g0 + 1, :]).astype(_BF16)

    bounds = [(c0, min(c0 + FFN_CHUNK, ffn_dim)) for c0 in range(0, ffn_dim, FFN_CHUNK)]

    def step(c):
        def run(s, st):
            prev = st.pop("ab", None)
            if c < len(bounds):
                c0, c1 = bounds[c]
                st["ab"] = (_dot(st["h"], w_in_ref[:, c0:c1]),
                            _dot(st["h"], w_in_ref[:, ffn_dim + c0:ffn_dim + c1]))
            if prev is not None:
                c0, c1 = bounds[c - 1]
                a, b = prev
                act = (a * jax.nn.sigmoid(a) * b).astype(_BF16)
                part = _dot(act, w_out_ref[c0:c1, :])
                st["y"] = part if "y" not in st else st["y"] + part
        return run

    def norm_out(s, st):
        rows = _sub_rows(s)
        o_ref[rows, :] = x_ref[rows, :] + 0.5 * _rms(st["y"], g_ref[g0 + 1:g0 + 2, :])

    _staggered(ROW_TILE // SUB_ROWS,
               [norm_in] + [step(c) for c in range(len(bounds) + 1)] + [norm_out])


def _ffn(x2d, gains, w_in, w_out, layer, g0):
    t, d = x2d.shape
    ffn_dim = w_out.shape[1]
    return pl.pallas_call(
        functools.partial(_ffn_kernel, ffn_dim=ffn_dim, g0=g0),
        grid=(t // ROW_TILE,),
        in_specs=[
            pl.BlockSpec((ROW_TILE, d), lambda i: (i, 0)),
            _layer_resident(gains.shape, layer),
            _layer_resident(w_in.shape, layer),
            _layer_resident(w_out.shape, layer),
        ],
        out_specs=pl.BlockSpec((ROW_TILE, d), lambda i: (i, 0)),
        out_shape=jax.ShapeDtypeStruct((t, d), _F32),
        compiler_params=_params(1),
        name="ffn",
    )(x2d, gains, w_in, w_out)


def _pre_kernel(x_ref, g_ref, w_ref, qs_ref, a_ref, c_ref, b0_ref, b1_ref, b2_ref, z_scr):
    def norm_in(s, st):
        st["h"] = _rms(x_ref[_sub_rows(s), :], g_ref[2:3, :]).astype(_BF16)

    first = (a_ref, c_ref, b0_ref)
    first_cols = sum(ref.shape[-1] for ref in first)

    def project_rows(s, st):
        z = _dot(st["h"], w_ref[:, 0:first_cols]) * qs_ref[:, 0:first_cols]
        col = 0
        for ref in first:
            ref[_sub_rows(s), :] = z[:, col:col + ref.shape[-1]].astype(_BF16)
            col += ref.shape[-1]

    def project_residues(s, st):
        z = _dot(st["h"], w_ref[:, first_cols:]) * qs_ref[:, first_cols:]
        n_slabs = z.shape[-1] // LANES
        for n in range(n_slabs):
            z_scr[s, n] = z[:, n * LANES:(n + 1) * LANES]
        for g, ref in ((1, b1_ref), (2, b2_ref)):
            rate = DIL_RATES[g]
            per = SUB_ROWS // rate
            for c in range(rate):
                for n in range(B_COLS // LANES):
                    slab = (g - 1) * (B_COLS // LANES) + n
                    ref[c, s * per:(s + 1) * per, n * LANES:(n + 1) * LANES] = (
                        z_scr[s, slab, pl.ds(c, per, stride=rate), :].astype(_BF16))

    _staggered(ROW_TILE // SUB_ROWS, [norm_in, project_residues, project_rows])


def _pre(x2d, gains, w_perm, q_scale, layer):
    t, d = x2d.shape
    row = lambda i: (i, 0)
    return pl.pallas_call(
        _pre_kernel,
        grid=(t // ROW_TILE,),
        in_specs=[
            pl.BlockSpec((ROW_TILE, d), row),
            _layer_resident(gains.shape, layer),
            _layer_resident(w_perm.shape, layer),
            _resident(q_scale.shape),
        ],
        out_specs=[
            pl.BlockSpec((ROW_TILE, A_COLS), row),
            pl.BlockSpec((ROW_TILE, C_W), row),
            pl.BlockSpec((ROW_TILE, B_COLS), row),
        ] + [pl.BlockSpec((r, ROW_TILE // r, B_COLS), lambda i: (0, i, 0))
             for r in DIL_RATES[1:]],
        out_shape=[
            jax.ShapeDtypeStruct((t, A_COLS), _BF16),
            jax.ShapeDtypeStruct((t, C_W), _BF16),
            jax.ShapeDtypeStruct((t, B_COLS), _BF16),
        ] + [jax.ShapeDtypeStruct((r, t // r, B_COLS), _BF16) for r in DIL_RATES[1:]],
        scratch_shapes=[
            pltpu.VMEM((ROW_TILE // SUB_ROWS, (DIL_GROUPS - 1) * B_COLS // LANES,
                        SUB_ROWS, LANES), _F32),
        ],
        compiler_params=_params(1),
        name="mix_in_proj",
    )(x2d, gains, w_perm, q_scale)


def _mem_kernel(m_ref, g_ref, w_ref, o_ref):
    h = _rms(m_ref[...], g_ref[...]).astype(_BF16)
    o_ref[...] = _dot(h, w_ref[...]).astype(_BF16)


def _mem_kv(mem2d, gains, w, layer):
    rows = mem2d.shape[0]
    return pl.pallas_call(
        _mem_kernel,
        grid=(1,),
        in_specs=[_resident(mem2d.shape),
                  _layer_resident(gains.shape, layer),
                  _layer_resident(w.shape, layer)],
        out_specs=pl.BlockSpec((rows, 2 * C_W), lambda i: (0, 0)),
        out_shape=jax.ShapeDtypeStruct((rows, 2 * C_W), _BF16),
        compiler_params=_params(1),
        name="mem_kv",
    )(mem2d, gains, w)


def _low_lanes():
    return lax.broadcasted_iota(jnp.int32, (1, LANES), 1) < HEAD_DIM


def _pair_scores(q, k, bias_lo=None, bias_hi=None):
    low = _low_lanes()
    zero = jnp.zeros((), k.dtype)
    if q.shape[0] <= k.shape[0]:
        s_lo = _dot_nt(jnp.where(low, q, zero), k)
        s_hi = _dot_nt(jnp.where(low, zero, q), k)
    else:
        s_lo = _dot_nt(q, jnp.where(low, k, zero))
        s_hi = _dot_nt(q, jnp.where(low, zero, k))
    if bias_lo is not None:
        s_lo = s_lo + bias_lo
        s_hi = s_hi + bias_hi
    return s_lo, s_hi


def _pair_softmax_pv(scores, v):
    s_lo, s_hi = scores
    low = _low_lanes()
    zero = jnp.zeros((), v.dtype)
    lane = lax.broadcasted_iota(jnp.int32, v.shape, 1)
    ones_lo = jnp.where(lane < HEAD_DIM, 1.0, 0.0).astype(v.dtype)
    ones_hi = jnp.where(lane < HEAD_DIM, 0.0, 1.0).astype(v.dtype)
    m_lo = jnp.max(s_lo, axis=-1, keepdims=True)
    m_hi = jnp.max(s_hi, axis=-1, keepdims=True)
    p_lo = jnp.exp2(s_lo - m_lo).astype(_BF16)
    p_hi = jnp.exp2(s_hi - m_hi).astype(_BF16)
    rhs = jnp.concatenate([
        jnp.concatenate([jnp.where(low, v, zero), ones_lo], axis=1),
        jnp.concatenate([jnp.where(low, zero, v), ones_hi], axis=1)],
        axis=0)
    res = _dot(jnp.concatenate([p_lo, p_hi], axis=1), rhs)
    return res[:, 0:LANES], res[:, LANES:2 * LANES], m_lo, m_hi


PIPELINE_LOOKAHEAD = 3


def _software_pipeline(tiles):
    pending = []
    for n, (scores_fn, _) in enumerate(tiles):
        pending.append(scores_fn())
        if n >= PIPELINE_LOOKAHEAD:
            done = n - PIPELINE_LOOKAHEAD
            tiles[done][1](pending[done])
            pending[done] = None
    for done in range(max(len(tiles) - PIPELINE_LOOKAHEAD, 0), len(tiles)):
        tiles[done][1](pending[done])


def _window_cross_tiles(cur_ref, prev_ref, qc_ref, kvc_ref, bias0_ref, bias_ref,
                        oa_ref, oc_ref):
    rows_a = SWA_GROUP * BLOCK
    sink_row = lax.broadcasted_iota(jnp.int32, (BLOCK, A_KV), 0) == 0

    def window_tile(j):
        rows = slice(j * BLOCK, (j + 1) * BLOCK)
        prev_rows = slice((j - 1) * BLOCK, j * BLOCK)
        bias = bias0_ref if j == 0 else bias_ref

        def kv_window(c0, c1):
            prev = prev_ref[:, c0:c1] if j == 0 else cur_ref[prev_rows, c0:c1]
            prev = jnp.where(sink_row, jnp.zeros((), prev.dtype), prev)
            return jnp.concatenate([prev, cur_ref[rows, c0:c1]], axis=0)

        def scores():
            q = jnp.concatenate(
                [cur_ref[rows, 2 * A_KV + p * LANES:2 * A_KV + (p + 1) * LANES]
                 for p in range(SWA_GROUP)], axis=0)
            return _pair_scores(q, kv_window(0, A_KV),
                                bias[0:SWA_GROUP].reshape(rows_a, 2 * BLOCK),
                                bias[SWA_GROUP:SWA_HEADS].reshape(rows_a, 2 * BLOCK))

        def finish(s):
            out, den, _, _ = _pair_softmax_pv(s, kv_window(A_KV, 2 * A_KV))
            o = out / den
            for p in range(SWA_GROUP):
                oa_ref[rows, p * LANES:(p + 1) * LANES] = (
                    o[p * BLOCK:(p + 1) * BLOCK]).astype(_BF16)

        return scores, finish

    def cross_tile(p, rows):
        cols = slice(p * LANES, (p + 1) * LANES)

        def scores():
            return _pair_scores(qc_ref[rows, cols], kvc_ref[:, cols])

        def finish(s):
            out, den, _, _ = _pair_softmax_pv(
                s, kvc_ref[:, C_W + p * LANES:C_W + (p + 1) * LANES])
            oc_ref[rows, cols] = (out / den).astype(_BF16)

        return scores, finish

    return window_tile, cross_tile


def _dil_kernel(*refs):
    cur_refs, prev_refs = refs[0:2 * DIL_GROUPS:2], refs[1:2 * DIL_GROUPS:2]
    bias0_ref, bias_ref, ob_ref, o_scr, d_scr, m_scr = refs[2 * DIL_GROUPS:]
    low = _low_lanes()
    k_cols, v_cols, q_cols = (slice(n * B_G, (n + 1) * B_G) for n in range(3))

    def tile(g, sb, c):
        rate = DIL_RATES[g]
        rows = slice(sb * BLOCK, (sb + 1) * BLOCK)
        bias = bias0_ref if sb == 0 else bias_ref
        if rate == 1:
            out_rows = pl.ds(sb * BLOCK, BLOCK)
            cur = lambda r, cols: cur_refs[g][r, cols]
            prev = lambda cols: prev_refs[g][:, cols]
        else:
            out_rows = pl.ds(sb * BLOCK * rate + c, BLOCK, stride=rate)
            cur = lambda r, cols: cur_refs[g][c, r, cols]
            prev = lambda cols: prev_refs[g][c, :, cols]

        def window(cols):
            before = prev(cols) if sb == 0 else cur(slice((sb - 1) * BLOCK, sb * BLOCK), cols)
            return jnp.concatenate([before, cur(rows, cols)], axis=0)

        def scores():
            return _pair_scores(cur(rows, q_cols), window(k_cols), bias[g, 0], bias[g, 1])

        def finish(s):
            out, den, m_lo, m_hi = _pair_softmax_pv(s, window(v_cols))
            o_scr[g, out_rows, :] = out
            d_scr[g, out_rows, :] = den
            m_scr[g, out_rows, :] = jnp.where(low, m_lo, m_hi)

        return scores, finish

    _software_pipeline([tile(g, sb, c)
                        for g, rate in enumerate(DIL_RATES)
                        for sb in range(DIL_TILE // (BLOCK * rate))
                        for c in range(rate)])

    top = functools.reduce(jnp.maximum, [m_scr[g] for g in range(DIL_GROUPS)])
    scale = [jnp.exp2(m_scr[g] - top) for g in range(DIL_GROUPS)]
    total = sum(d_scr[g] * scale[g] for g in range(DIL_GROUPS))
    for g in range(DIL_GROUPS):
        ob_ref[:, g * B_G:(g + 1) * B_G] = (o_scr[g] * (scale[g] / total)).astype(_BF16)


def _attn_dil(zb, bias_b, batch, seq):
    t = zb[0].shape[0]
    tiles = seq // DIL_TILE
    row = lambda b, i: (b * tiles + i, 0)
    bias_shape = (None,) + bias_b.shape[1:]
    operands, specs = [], []
    for z, rate in zip(zb, DIL_RATES):
        rows = DIL_TILE // rate
        prev_block = lambda b, i, n=rows // BLOCK: jnp.maximum((b * tiles + i) * n - 1, 0)
        operands += [z, z]
        if rate == 1:
            specs += [pl.BlockSpec((rows, B_COLS), row),
                      pl.BlockSpec((BLOCK, B_COLS),
                                   lambda b, i, f=prev_block: (f(b, i), 0))]
        else:
            specs += [pl.BlockSpec((rate, rows, B_COLS), lambda b, i: (0, b * tiles + i, 0)),
                      pl.BlockSpec((rate, BLOCK, B_COLS),
                                   lambda b, i, f=prev_block: (0, f(b, i), 0))]
    return pl.pallas_call(
        _dil_kernel,
        grid=(batch, tiles),
        in_specs=specs + [
            pl.BlockSpec(bias_shape, lambda b, i: (jnp.minimum(i, 1), 0, 0, 0, 0)),
            pl.BlockSpec(bias_shape, lambda b, i: (1, 0, 0, 0, 0)),
        ],
        out_specs=pl.BlockSpec((DIL_TILE, B_W), row),
        out_shape=jax.ShapeDtypeStruct((t, B_W), _BF16),
        scratch_shapes=[
            pltpu.VMEM((DIL_GROUPS, DIL_TILE, B_G), _F32),
            pltpu.VMEM((DIL_GROUPS, DIL_TILE, B_G), _F32),
            pltpu.VMEM((DIL_GROUPS, DIL_TILE, B_G), _F32),
        ],
        compiler_params=_params(2),
        name="attn_dilated",
    )(*operands, bias_b, bias_b)


def _post_kernel(x_ref, za_ref, za_prev_ref, qc_ref, kvc_ref, bias0_ref, bias_ref, ob_ref,
                 g_ref, wg_ref, bg_ref, wa_ref, wb_ref, wc_ref, wo_ref, o_ref,
                 oa_ref, oc_ref):
    d = x_ref.shape[-1]
    window_tile, cross_tile = _window_cross_tiles(
        za_ref, za_prev_ref, qc_ref, kvc_ref, bias0_ref, bias_ref, oa_ref, oc_ref)
    blocks_per_sub = SUB_ROWS // BLOCK

    def attend(s, st):
        _software_pipeline(
            [window_tile(j) for j in range(s * blocks_per_sub, (s + 1) * blocks_per_sub)]
            + [cross_tile(p, _sub_rows(s)) for p in range(MEM_HEADS // 2)])

    def norm_in(s, st):
        st["h"] = _rms(x_ref[_sub_rows(s), :], g_ref[2:3, :]).astype(_BF16)

    def branch(n, o_branch_ref, w_branch_ref):
        def run(s, st):
            gate = jax.nn.sigmoid(_dot(st["h"], wg_ref[:, n * d:(n + 1) * d])
                                  + bg_ref[n:n + 1, :])
            term = gate * _dot(o_branch_ref[_sub_rows(s), :], w_branch_ref[...])
            st["merged"] = term if n == 0 else st["merged"] + term
        return run

    def project(s, st):
        st["y"] = _dot(st["merged"].astype(_BF16), wo_ref[...])

    def norm_out(s, st):
        rows = _sub_rows(s)
        o_ref[rows, :] = x_ref[rows, :] + _rms(st["y"], g_ref[3:4, :])

    _staggered(ROW_TILE // SUB_ROWS,
               [attend, norm_in, branch(0, oa_ref, wa_ref), branch(1, ob_ref, wb_ref),
                branch(2, oc_ref, wc_ref), project, norm_out])


def _post(x2d, za, qc, kvc, bias_a, ob, gains, w_gate, b_gate, w_a, w_b, w_c, w_o,
          layer, batch, seq):
    t, d = x2d.shape
    tiles = seq // ROW_TILE
    mem_len = kvc.shape[0] // batch
    row = lambda b, i: (b * tiles + i, 0)
    bias_shape = (None,) + bias_a.shape[1:]
    return pl.pallas_call(
        _post_kernel,
        grid=(batch, tiles),
        in_specs=[
            pl.BlockSpec((ROW_TILE, d), row),
            pl.BlockSpec((ROW_TILE, A_COLS), row),
            pl.BlockSpec((BLOCK, 2 * A_KV), lambda b, i: (
                jnp.maximum((b * tiles + i) * (ROW_TILE // BLOCK) - 1, 0), 0)),
            pl.BlockSpec((ROW_TILE, C_W), row),
            pl.BlockSpec((mem_len, 2 * C_W), lambda b, i: (b, 0)),
            pl.BlockSpec(bias_shape, lambda b, i: (jnp.minimum(i, 1), 0, 0, 0)),
            pl.BlockSpec(bias_shape, lambda b, i: (1, 0, 0, 0)),
            pl.BlockSpec((ROW_TILE, B_W), row),
            _layer_resident(gains.shape, layer),
            _layer_resident(w_gate.shape, layer),
            _layer_resident(b_gate.shape, layer),
            _layer_resident(w_a.shape, layer),
            _layer_resident(w_b.shape, layer),
            _layer_resident(w_c.shape, layer),
            _layer_resident(w_o.shape, layer),
        ],
        out_specs=pl.BlockSpec((ROW_TILE, d), row),
        out_shape=jax.ShapeDtypeStruct((t, d), _F32),
        scratch_shapes=[pltpu.VMEM((ROW_TILE, A_Q), _BF16),
                        pltpu.VMEM((ROW_TILE, C_W), _BF16)],
        compiler_params=_params(2),
        name="attn_window_cross_out_proj",
    )(x2d, za, za, qc, kvc, bias_a, bias_a, ob, gains, w_gate, b_gate, w_a, w_b, w_c, w_o)


def _t5_bucket(dist):
    max_exact = N_BUCKETS // 2
    d = jnp.maximum(dist, 1).astype(_F32)
    large = max_exact + (jnp.log(d / max_exact) / math.log(MAX_DISTANCE / max_exact)
                         * (N_BUCKETS - max_exact)).astype(jnp.int32)
    large = jnp.minimum(large, N_BUCKETS - 1)
    return jnp.where(dist < max_exact, dist, large)


def _band_bias(table, head0, n_heads, rate, max_dist):
    row = jnp.arange(BLOCK)[:, None]
    col = jnp.arange(2 * BLOCK)[None, :]
    dist = row + BLOCK - col
    bucket = _t5_bucket(jnp.maximum(dist, 0) * rate)
    hit = bucket[None, None] == jnp.arange(N_BUCKETS)[None, :, None, None]
    vals = table.T[head0:head0 + n_heads, :, None, None]
    bias = jnp.sum(jnp.where(hit, vals, 0.0), axis=1) * LOG2_E
    valid = (dist >= 0) & (dist <= max_dist)
    later = jnp.where(valid[None], bias, MASKED)
    first = jnp.where((valid & (col >= BLOCK))[None], bias, MASKED)
    return jnp.stack([first, later])


def _permute_in_proj(w):
    hd = HEAD_DIM
    qa, ka, va = w[..., 0:A_Q], w[..., A_Q:A_Q + A_KV], w[..., A_Q + A_KV:A_Q + 2 * A_KV]
    off = A_Q + 2 * A_KV
    qb, kb, vb = (w[..., off + n * B_W:off + (n + 1) * B_W] for n in range(3))
    qc = w[..., off + 3 * B_W:]
    grp = lambda w3, g: w3[..., g * B_G:(g + 1) * B_G]
    cols = [ka, va] + [qa[..., h * hd:(h + 1) * hd] for h in A_PAIR_ORDER] + [qc]
    for g in range(DIL_GROUPS):
        cols += [grp(kb, g), grp(vb, g), grp(qb, g)]
    return jnp.concatenate(cols, axis=-1)


def kernel(x, mem, rel_bias, norm_gain, mem_norm_gain, w_ffn1_in, w_ffn1_out, w_in,
           sinks, w_mem_kv, w_gate, b_gate, w_br_a, w_br_b, w_br_c, w_o,
           w_ffn2_in, w_ffn2_out):
    batch, seq, d = x.shape
    depth = norm_gain.shape[0]
    assert seq % DIL_TILE == 0 and (batch * seq) % ROW_TILE == 0

    table = rel_bias.astype(_F32)
    bias_a = _band_bias(table, 0, SWA_HEADS, 1, SWA_WINDOW - 1)
    bias_b = jnp.stack([
        _band_bias(table, SWA_HEADS + g * DIL_HEADS_PER_GROUP, DIL_HEADS_PER_GROUP,
                   DIL_RATES[g], DIL_WINDOWS[g] // DIL_RATES[g])
        for g in range(DIL_GROUPS)], axis=1)

    bf = lambda w: w.astype(_BF16)
    w1_in, w1_out, w2_in, w2_out = bf(w_ffn1_in), bf(w_ffn1_out), bf(w_ffn2_in), bf(w_ffn2_out)
    w_in_p = _permute_in_proj(bf(w_in))
    w_kv, w_g, w_b, w_c, w_out = bf(w_mem_kv), bf(w_gate), bf(w_br_b), bf(w_br_c), bf(w_o)
    w_a = bf(jnp.concatenate(
        [w_br_a[:, h * HEAD_DIM:(h + 1) * HEAD_DIM] for h in A_PAIR_ORDER], axis=1))

    group0 = A_COLS + C_W
    q_cols = [(2 * A_KV, group0)]
    q_cols += [(group0 + g * B_COLS + 2 * B_G, group0 + (g + 1) * B_COLS)
               for g in range(DIL_GROUPS)]
    q_scale = jnp.ones((1, w_in.shape[-1]), _F32)
    for c0, c1 in q_cols:
        q_scale = q_scale.at[:, c0:c1].set(Q_LOG2_SCALE)

    x2d = x.reshape(batch * seq, d)
    mem2d = mem.reshape(-1, d)
    for l in range(depth):
        x2d = _ffn(x2d, norm_gain, w1_in, w1_out, l, 0)
        za, qc, *zb = _pre(x2d, norm_gain, w_in_p, q_scale, l)
        kvc = _mem_kv(mem2d, mem_norm_gain[:, None, :], w_kv, l)
        bias_a_l = jnp.where(jnp.arange(2 * BLOCK) == 0,
                             (sinks[l].astype(_F32) * LOG2_E)[None, :, None, None], bias_a)
        ob = _attn_dil(zb, bias_b, batch, seq)
        x2d = _post(x2d, za, qc, kvc, bias_a_l, ob, norm_gain, w_g, b_gate,
                    w_a, w_b, w_c, w_out, l, batch, seq)
        x2d = _ffn(x2d, norm_gain, w2_in, w2_out, l, 4)
    return x2d.reshape(batch, seq, d)
```

```python
import functools
import math

import jax
import jax.numpy as jnp
from jax import lax
from jax.experimental import pallas as pl
from jax.experimental.pallas import tpu as pltpu

HEAD_DIM = 64
SWA_HEADS = 6
SWA_KV_HEADS = 2
SWA_GROUP = SWA_HEADS // SWA_KV_HEADS
SWA_WINDOW = 128
DIL_WINDOWS = (128, 512, 2048)
DIL_RATES = (1, 4, 16)
DIL_GROUPS = 3
DIL_HEADS_PER_GROUP = 2
MEM_HEADS = 4
N_BUCKETS = 32
MAX_DISTANCE = 2048
N_BRANCH = 3
EPS = 1e-6
BLOCK = 128
LANES = 128

A_Q = SWA_HEADS * HEAD_DIM
A_KV = SWA_KV_HEADS * HEAD_DIM
B_G = DIL_HEADS_PER_GROUP * HEAD_DIM
B_W = DIL_GROUPS * B_G
C_W = MEM_HEADS * HEAD_DIM
A_COLS = A_Q + 2 * A_KV
B_COLS = 3 * B_G
A_PAIR_ORDER = tuple(h for p in range(SWA_GROUP) for h in (p, p + SWA_GROUP))

LOG2_E = math.log2(math.e)
Q_LOG2_SCALE = HEAD_DIM ** -0.5 * LOG2_E
MASKED = -1e30

ROW_TILE = 1024
SUB_ROWS = 256
FFN_CHUNK = 512
DIL_TILE = BLOCK * max(DIL_RATES)
VMEM_LIMIT = 56 * 1024 * 1024

_F32 = jnp.float32
_BF16 = jnp.bfloat16


def _dot(a, b):
    return jnp.dot(a, b, preferred_element_type=_F32)


def _dot_nt(a, b):
    return lax.dot_general(a, b, (((1,), (1,)), ((), ())),
                           preferred_element_type=_F32)


def _rms(x, gain):
    ms = jnp.mean(x * x, axis=-1, keepdims=True)
    return x * lax.rsqrt(ms + EPS) * gain


def _resident(shape):
    return pl.BlockSpec(shape, lambda *_: (0,) * len(shape),
                        pipeline_mode=pl.Buffered(1))


def _layer_resident(shape, layer):
    zeros = (0,) * (len(shape) - 1)
    return pl.BlockSpec((None,) + tuple(shape[1:]), lambda *_: (layer,) + zeros,
                        pipeline_mode=pl.Buffered(1))


def _params(n_axes):
    return pltpu.CompilerParams(
        dimension_semantics=("arbitrary",) * n_axes,
        vmem_limit_bytes=VMEM_LIMIT)


def _staggered(n_sub, stages):
    state = [{} for _ in range(n_sub)]
    for t in range(n_sub + len(stages) - 1):
        for s in range(n_sub):
            if 0 <= t - s < len(stages):
                stages[t - s](s, state[s])


def _sub_rows(s):
    return slice(s * SUB_ROWS, (s + 1) * SUB_ROWS)


def _ffn_kernel(x_ref, g_ref, w_in_ref, w_out_ref, o_ref, *, ffn_dim, g0):
    def norm_in(s, st):
        st["h"] = _rms(x_ref[_sub_rows(s), :], g_ref[g0:g0 + 1, :]).astype(_BF16)

    bounds = [(c0, min(c0 + FFN_CHUNK, ffn_dim)) for c0 in range(0, ffn_dim, FFN_CHUNK)]

    def step(c):
        def run(s, st):
            prev = st.pop("ab", None)
            if c < len(bounds):
                c0, c1 = bounds[c]
                st["ab"] = (_dot(st["h"], w_in_ref[:, c0:c1]),
                            _dot(st["h"], w_in_ref[:, ffn_dim + c0:ffn_dim + c1]))
            if prev is not None:
                c0, c1 = bounds[c - 1]
                a, b = prev
                act = (a * jax.nn.sigmoid(a) * b).astype(_BF16)
                part = _dot(act, w_out_ref[c0:c1, :])
                st["y"] = part if "y" not in st else st["y"] + part
        return run

    def norm_out(s, st):
        rows = _sub_rows(s)
        o_ref[rows, :] = x_ref[rows, :] + 0.5 * _rms(st["y"], g_ref[g0 + 1:g0 + 2, :])

    _staggered(ROW_TILE // SUB_ROWS,
               [norm_in] + [step(c) for c in range(len(bounds) + 1)] + [norm_out])


def _ffn(x2d, gains, w_in, w_out, layer, g0):
    t, d = x2d.shape
    ffn_dim = w_out.shape[1]
    return pl.pallas_call(
        functools.partial(_ffn_kernel, ffn_dim=ffn_dim, g0=g0),
        grid=(t // ROW_TILE,),
        in_specs=[
            pl.BlockSpec((ROW_TILE, d), lambda i: (i, 0)),
            _layer_resident(gains.shape, layer),
            _layer_resident(w_in.shape, layer),
            _layer_resident(w_out.shape, layer),
        ],
        out_specs=pl.BlockSpec((ROW_TILE, d), lambda i: (i, 0)),
        out_shape=jax.ShapeDtypeStruct((t, d), _F32),
        compiler_params=_params(1),
        name="ffn",
    )(x2d, gains, w_in, w_out)


def _pre_kernel(x_ref, g_ref, w_ref, qs_ref, a_ref, c_ref, b0_ref, b1_ref, b2_ref, z_scr):
    def norm_in(s, st):
        st["h"] = _rms(x_ref[_sub_rows(s), :], g_ref[2:3, :]).astype(_BF16)

    first = (a_ref, c_ref, b0_ref)
    first_cols = sum(ref.shape[-1] for ref in first)

    def project_rows(s, st):
        z = _dot(st["h"], w_ref[:, 0:first_cols]) * qs_ref[:, 0:first_cols]
        col = 0
        for ref in first:
            ref[_sub_rows(s), :] = z[:, col:col + ref.shape[-1]].astype(_BF16)
            col += ref.shape[-1]

    def project_residues(s, st):
        z = _dot(st["h"], w_ref[:, first_cols:]) * qs_ref[:, first_cols:]
        n_slabs = z.shape[-1] // LANES
        for n in range(n_slabs):
            z_scr[s, n] = z[:, n * LANES:(n + 1) * LANES]
        for g, ref in ((1, b1_ref), (2, b2_ref)):
            rate = DIL_RATES[g]
            per = SUB_ROWS // rate
            for c in range(rate):
                for n in range(B_COLS // LANES):
                    slab = (g - 1) * (B_COLS // LANES) + n
                    ref[c, s * per:(s + 1) * per, n * LANES:(n + 1) * LANES] = (
                        z_scr[s, slab, pl.ds(c, per, stride=rate), :].astype(_BF16))

    _staggered(ROW_TILE // SUB_ROWS, [norm_in, project_residues, project_rows])


def _pre(x2d, gains, w_perm, q_scale, layer):
    t, d = x2d.shape
    row = lambda i: (i, 0)
    return pl.pallas_call(
        _pre_kernel,
        grid=(t // ROW_TILE,),
        in_specs=[
            pl.BlockSpec((ROW_TILE, d), row),
            _layer_resident(gains.shape, layer),
            _layer_resident(w_perm.shape, layer),
            _resident(q_scale.shape),
        ],
        out_specs=[
            pl.BlockSpec((ROW_TILE, A_COLS), row),
            pl.BlockSpec((ROW_TILE, C_W), row),
            pl.BlockSpec((ROW_TILE, B_COLS), row),
        ] + [pl.BlockSpec((r, ROW_TILE // r, B_COLS), lambda i: (0, i, 0))
             for r in DIL_RATES[1:]],
        out_shape=[
            jax.ShapeDtypeStruct((t, A_COLS), _BF16),
            jax.ShapeDtypeStruct((t, C_W), _BF16),
            jax.ShapeDtypeStruct((t, B_COLS), _BF16),
        ] + [jax.ShapeDtypeStruct((r, t // r, B_COLS), _BF16) for r in DIL_RATES[1:]],
        scratch_shapes=[
            pltpu.VMEM((ROW_TILE // SUB_ROWS, (DIL_GROUPS - 1) * B_COLS // LANES,
                        SUB_ROWS, LANES), _F32),
        ],
        compiler_params=_params(1),
        name="mix_in_proj",
    )(x2d, gains, w_perm, q_scale)


def _mem_kernel(m_ref, g_ref, w_ref, o_ref):
    h = _rms(m_ref[...], g_ref[...]).astype(_BF16)
    o_ref[...] = _dot(h, w_ref[...]).astype(_BF16)


def _mem_kv(mem2d, gains, w, layer):
    rows = mem2d.shape[0]
    return pl.pallas_call(
        _mem_kernel,
        grid=(1,),
        in_specs=[_resident(mem2d.shape),
                  _layer_resident(gains.shape, layer),
                  _layer_resident(w.shape, layer)],
        out_specs=pl.BlockSpec((rows, 2 * C_W), lambda i: (0, 0)),
        out_shape=jax.ShapeDtypeStruct((rows, 2 * C_W), _BF16),
        compiler_params=_params(1),
        name="mem_kv",
    )(mem2d, gains, w)


def _low_lanes():
    return lax.broadcasted_iota(jnp.int32, (1, LANES), 1) < HEAD_DIM


def _pair_scores(q, k, bias_lo=None, bias_hi=None):
    low = _low_lanes()
    zero = jnp.zeros((), k.dtype)
    if q.shape[0] <= k.shape[0]:
        s_lo = _dot_nt(jnp.where(low, q, zero), k)
        s_hi = _dot_nt(jnp.where(low, zero, q), k)
    else:
        s_lo = _dot_nt(q, jnp.where(low, k, zero))
        s_hi = _dot_nt(q, jnp.where(low, zero, k))
    if bias_lo is not None:
        s_lo = s_lo + bias_lo
        s_hi = s_hi + bias_hi
    return s_lo, s_hi


def _pair_softmax_pv(scores, v):
    s_lo, s_hi = scores
    low = _low_lanes()
    zero = jnp.zeros((), v.dtype)
    lane = lax.broadcasted_iota(jnp.int32, v.shape, 1)
    ones_lo = jnp.where(lane < HEAD_DIM, 1.0, 0.0).astype(v.dtype)
    ones_hi = jnp.where(lane < HEAD_DIM, 0.0, 1.0).astype(v.dtype)
    m_lo = jnp.max(s_lo, axis=-1, keepdims=True)
    m_hi = jnp.max(s_hi, axis=-1, keepdims=True)
    p_lo = jnp.exp2(s_lo - m_lo).astype(_BF16)
    p_hi = jnp.exp2(s_hi - m_hi).astype(_BF16)
    rhs = jnp.concatenate([
        jnp.concatenate([jnp.where(low, v, zero), ones_lo], axis=1),
        jnp.concatenate([jnp.where(low, zero, v), ones_hi], axis=1)],
        axis=0)
    res = _dot(jnp.concatenate([p_lo, p_hi], axis=1), rhs)
    return res[:, 0:LANES], res[:, LANES:2 * LANES], m_lo, m_hi


PIPELINE_LOOKAHEAD = 3


def _software_pipeline(tiles):
    pending = []
    for n, (scores_fn, _) in enumerate(tiles):
        pending.append(scores_fn())
        if n >= PIPELINE_LOOKAHEAD:
            done = n - PIPELINE_LOOKAHEAD
            tiles[done][1](pending[done])
            pending[done] = None
    for done in range(max(len(tiles) - PIPELINE_LOOKAHEAD, 0), len(tiles)):
        tiles[done][1](pending[done])


def _window_cross_tiles(cur_ref, prev_ref, qc_ref, kvc_ref, bias0_ref, bias_ref,
                        oa_ref, oc_ref):
    rows_a = SWA_GROUP * BLOCK
    sink_row = lax.broadcasted_iota(jnp.int32, (BLOCK, A_KV), 0) == 0

    def window_tile(j):
        rows = slice(j * BLOCK, (j + 1) * BLOCK)
        prev_rows = slice((j - 1) * BLOCK, j * BLOCK)
        bias = bias0_ref if j == 0 else bias_ref

        def kv_window(c0, c1):
            prev = prev_ref[:, c0:c1] if j == 0 else cur_ref[prev_rows, c0:c1]
            prev = jnp.where(sink_row, jnp.zeros((), prev.dtype), prev)
            return jnp.concatenate([prev, cur_ref[rows, c0:c1]], axis=0)

        def scores():
            q = jnp.concatenate(
                [cur_ref[rows, 2 * A_KV + p * LANES:2 * A_KV + (p + 1) * LANES]
                 for p in range(SWA_GROUP)], axis=0)
            return _pair_scores(q, kv_window(0, A_KV),
                                bias[0:SWA_GROUP].reshape(rows_a, 2 * BLOCK),
                                bias[SWA_GROUP:SWA_HEADS].reshape(rows_a, 2 * BLOCK))

        def finish(s):
            out, den, _, _ = _pair_softmax_pv(s, kv_window(A_KV, 2 * A_KV))
            o = out / den
            for p in range(SWA_GROUP):
                oa_ref[rows, p * LANES:(p + 1) * LANES] = (
                    o[p * BLOCK:(p + 1) * BLOCK]).astype(_BF16)

        return scores, finish

    def cross_tile(p, rows):
        cols = slice(p * LANES, (p + 1) * LANES)

        def scores():
            return _pair_scores(qc_ref[rows, cols], kvc_ref[:, cols])

        def finish(s):
            out, den, _, _ = _pair_softmax_pv(
                s, kvc_ref[:, C_W + p * LANES:C_W + (p + 1) * LANES])
            oc_ref[rows, cols] = (out / den).astype(_BF16)

        return scores, finish

    return window_tile, cross_tile


def _dil_kernel(*refs):
    cur_refs, prev_refs = refs[0:2 * DIL_GROUPS:2], refs[1:2 * DIL_GROUPS:2]
    bias0_ref, bias_ref, ob_ref, o_scr, d_scr, m_scr = refs[2 * DIL_GROUPS:]
    low = _low_lanes()
    k_cols, v_cols, q_cols = (slice(n * B_G, (n + 1) * B_G) for n in range(3))

    def tile(g, sb, c):
        rate = DIL_RATES[g]
        rows = slice(sb * BLOCK, (sb + 1) * BLOCK)
        bias = bias0_ref if sb == 0 else bias_ref
        if rate == 1:
            out_rows = pl.ds(sb * BLOCK, BLOCK)
            cur = lambda r, cols: cur_refs[g][r, cols]
            prev = lambda cols: prev_refs[g][:, cols]
        else:
            out_rows = pl.ds(sb * BLOCK * rate + c, BLOCK, stride=rate)
            cur = lambda r, cols: cur_refs[g][c, r, cols]
            prev = lambda cols: prev_refs[g][c, :, cols]

        def window(cols):
            before = prev(cols) if sb == 0 else cur(slice((sb - 1) * BLOCK, sb * BLOCK), cols)
            return jnp.concatenate([before, cur(rows, cols)], axis=0)

        def scores():
            return _pair_scores(cur(rows, q_cols), window(k_cols), bias[g, 0], bias[g, 1])

        def finish(s):
            out, den, m_lo, m_hi = _pair_softmax_pv(s, window(v_cols))
            o_scr[g, out_rows, :] = out
            d_scr[g, out_rows, :] = den
            m_scr[g, out_rows, :] = jnp.where(low, m_lo, m_hi)

        return scores, finish

    per_group = [[tile(g, sb, c)
                  for sb in range(DIL_TILE // (BLOCK * rate)) for c in range(rate)]
                 for g, rate in enumerate(DIL_RATES)]
    _software_pipeline([t for trio in zip(*per_group) for t in trio])

    top = functools.reduce(jnp.maximum, [m_scr[g] for g in range(DIL_GROUPS)])
    scale = [jnp.exp2(m_scr[g] - top) for g in range(DIL_GROUPS)]
    total = sum(d_scr[g] * scale[g] for g in range(DIL_GROUPS))
    for g in range(DIL_GROUPS):
        ob_ref[:, g * B_G:(g + 1) * B_G] = (o_scr[g] * (scale[g] / total)).astype(_BF16)


def _attn_dil(zb, bias_b, batch, seq):
    t = zb[0].shape[0]
    tiles = seq // DIL_TILE
    row = lambda b, i: (b * tiles + i, 0)
    bias_shape = (None,) + bias_b.shape[1:]
    operands, specs = [], []
    for z, rate in zip(zb, DIL_RATES):
        rows = DIL_TILE // rate
        prev_block = lambda b, i, n=rows // BLOCK: jnp.maximum((b * tiles + i) * n - 1, 0)
        operands += [z, z]
        if rate == 1:
            specs += [pl.BlockSpec((rows, B_COLS), row),
                      pl.BlockSpec((BLOCK, B_COLS),
                                   lambda b, i, f=prev_block: (f(b, i), 0))]
        else:
            specs += [pl.BlockSpec((rate, rows, B_COLS), lambda b, i: (0, b * tiles + i, 0)),
                      pl.BlockSpec((rate, BLOCK, B_COLS),
                                   lambda b, i, f=prev_block: (0, f(b, i), 0))]
    return pl.pallas_call(
        _dil_kernel,
        grid=(batch, tiles),
        in_specs=specs + [
            pl.BlockSpec(bias_shape, lambda b, i: (jnp.minimum(i, 1), 0, 0, 0, 0)),
            pl.BlockSpec(bias_shape, lambda b, i: (1, 0, 0, 0, 0)),
        ],
        out_specs=pl.BlockSpec((DIL_TILE, B_W), row),
        out_shape=jax.ShapeDtypeStruct((t, B_W), _BF16),
        scratch_shapes=[
            pltpu.VMEM((DIL_GROUPS, DIL_TILE, B_G), _F32),
            pltpu.VMEM((DIL_GROUPS, DIL_TILE, B_G), _F32),
            pltpu.VMEM((DIL_GROUPS, DIL_TILE, B_G), _F32),
        ],
        compiler_params=_params(2),
        name="attn_dilated",
    )(*operands, bias_b, bias_b)


def _post_kernel(x_ref, za_ref, za_prev_ref, qc_ref, kvc_ref, bias0_ref, bias_ref, ob_ref,
                 g_ref, wg_ref, bg_ref, wa_ref, wb_ref, wc_ref, wo_ref, o_ref,
                 oa_ref, oc_ref):
    d = x_ref.shape[-1]
    window_tile, cross_tile = _window_cross_tiles(
        za_ref, za_prev_ref, qc_ref, kvc_ref, bias0_ref, bias_ref, oa_ref, oc_ref)
    blocks_per_sub = SUB_ROWS // BLOCK

    def attend(s, st):
        _software_pipeline(
            [window_tile(j) for j in range(s * blocks_per_sub, (s + 1) * blocks_per_sub)]
            + [cross_tile(p, _sub_rows(s)) for p in range(MEM_HEADS // 2)])

    def norm_in(s, st):
        st["h"] = _rms(x_ref[_sub_rows(s), :], g_ref[2:3, :]).astype(_BF16)

    def branch(n, o_branch_ref, w_branch_ref):
        def run(s, st):
            gate = jax.nn.sigmoid(_dot(st["h"], wg_ref[:, n * d:(n + 1) * d])
                                  + bg_ref[n:n + 1, :])
            term = gate * _dot(o_branch_ref[_sub_rows(s), :], w_branch_ref[...])
            st["merged"] = term if n == 0 else st["merged"] + term
        return run

    def project(s, st):
        st["y"] = _dot(st["merged"].astype(_BF16), wo_ref[...])

    def norm_out(s, st):
        rows = _sub_rows(s)
        o_ref[rows, :] = x_ref[rows, :] + _rms(st["y"], g_ref[3:4, :])

    _staggered(ROW_TILE // SUB_ROWS,
               [attend, norm_in, branch(0, oa_ref, wa_ref), branch(1, ob_ref, wb_ref),
                branch(2, oc_ref, wc_ref), project, norm_out])


def _post(x2d, za, qc, kvc, bias_a, ob, gains, w_gate, b_gate, w_a, w_b, w_c, w_o,
          layer, batch, seq):
    t, d = x2d.shape
    tiles = seq // ROW_TILE
    mem_len = kvc.shape[0] // batch
    row = lambda b, i: (b * tiles + i, 0)
    bias_shape = (None,) + bias_a.shape[1:]
    return pl.pallas_call(
        _post_kernel,
        grid=(batch, tiles),
        in_specs=[
            pl.BlockSpec((ROW_TILE, d), row),
            pl.BlockSpec((ROW_TILE, A_COLS), row),
            pl.BlockSpec((BLOCK, 2 * A_KV), lambda b, i: (
                jnp.maximum((b * tiles + i) * (ROW_TILE // BLOCK) - 1, 0), 0)),
            pl.BlockSpec((ROW_TILE, C_W), row),
            pl.BlockSpec((mem_len, 2 * C_W), lambda b, i: (b, 0)),
            pl.BlockSpec(bias_shape, lambda b, i: (jnp.minimum(i, 1), 0, 0, 0)),
            pl.BlockSpec(bias_shape, lambda b, i: (1, 0, 0, 0)),
            pl.BlockSpec((ROW_TILE, B_W), row),
            _layer_resident(gains.shape, layer),
            _layer_resident(w_gate.shape, layer),
            _layer_resident(b_gate.shape, layer),
            _layer_resident(w_a.shape, layer),
            _layer_resident(w_b.shape, layer),
            _layer_resident(w_c.shape, layer),
            _layer_resident(w_o.shape, layer),
        ],
        out_specs=pl.BlockSpec((ROW_TILE, d), row),
        out_shape=jax.ShapeDtypeStruct((t, d), _F32),
        scratch_shapes=[pltpu.VMEM((ROW_TILE, A_Q), _BF16),
                        pltpu.VMEM((ROW_TILE, C_W), _BF16)],
        compiler_params=_params(2),
        name="attn_window_cross_out_proj",
    )(x2d, za, za, qc, kvc, bias_a, bias_a, ob, gains, w_gate, b_gate, w_a, w_b, w_c, w_o)


def _t5_bucket(dist):
    max_exact = N_BUCKETS // 2
    d = jnp.maximum(dist, 1).astype(_F32)
    large = max_exact + (jnp.log(d / max_exact) / math.log(MAX_DISTANCE / max_exact)
                         * (N_BUCKETS - max_exact)).astype(jnp.int32)
    large = jnp.minimum(large, N_BUCKETS - 1)
    return jnp.where(dist < max_exact, dist, large)


def _band_bias(table, head0, n_heads, rate, max_dist):
    row = jnp.arange(BLOCK)[:, None]
    col = jnp.arange(2 * BLOCK)[None, :]
    dist = row + BLOCK - col
    bucket = _t5_bucket(jnp.maximum(dist, 0) * rate)
    hit = bucket[None, None] == jnp.arange(N_BUCKETS)[None, :, None, None]
    vals = table.T[head0:head0 + n_heads, :, None, None]
    bias = jnp.sum(jnp.where(hit, vals, 0.0), axis=1) * LOG2_E
    valid = (dist >= 0) & (dist <= max_dist)
    later = jnp.where(valid[None], bias, MASKED)
    first = jnp.where((valid & (col >= BLOCK))[None], bias, MASKED)
    return jnp.stack([first, later])


def _permute_in_proj(w):
    hd = HEAD_DIM
    qa, ka, va = w[..., 0:A_Q], w[..., A_Q:A_Q + A_KV], w[..., A_Q + A_KV:A_Q + 2 * A_KV]
    off = A_Q + 2 * A_KV
    qb, kb, vb = (w[..., off + n * B_W:off + (n + 1) * B_W] for n in range(3))
    qc = w[..., off + 3 * B_W:]
    grp = lambda w3, g: w3[..., g * B_G:(g + 1) * B_G]
    cols = [ka, va] + [qa[..., h * hd:(h + 1) * hd] for h in A_PAIR_ORDER] + [qc]
    for g in range(DIL_GROUPS):
        cols += [grp(kb, g), grp(vb, g), grp(qb, g)]
    return jnp.concatenate(cols, axis=-1)


def kernel(x, mem, rel_bias, norm_gain, mem_norm_gain, w_ffn1_in, w_ffn1_out, w_in,
           sinks, w_mem_kv, w_gate, b_gate, w_br_a, w_br_b, w_br_c, w_o,
           w_ffn2_in, w_ffn2_out):
    batch, seq, d = x.shape
    depth = norm_gain.shape[0]
    assert seq % DIL_TILE == 0 and (batch * seq) % ROW_TILE == 0

    table = rel_bias.astype(_F32)
    bias_a = _band_bias(table, 0, SWA_HEADS, 1, SWA_WINDOW - 1)
    bias_b = jnp.stack([
        _band_bias(table, SWA_HEADS + g * DIL_HEADS_PER_GROUP, DIL_HEADS_PER_GROUP,
                   DIL_RATES[g], DIL_WINDOWS[g] // DIL_RATES[g])
        for g in range(DIL_GROUPS)], axis=1)

    bf = lambda w: w.astype(_BF16)
    w1_in, w1_out, w2_in, w2_out = bf(w_ffn1_in), bf(w_ffn1_out), bf(w_ffn2_in), bf(w_ffn2_out)
    w_in_p = _permute_in_proj(bf(w_in))
    w_kv, w_g, w_b, w_c, w_out = bf(w_mem_kv), bf(w_gate), bf(w_br_b), bf(w_br_c), bf(w_o)
    w_a = bf(jnp.concatenate(
        [w_br_a[:, h * HEAD_DIM:(h + 1) * HEAD_DIM] for h in A_PAIR_ORDER], axis=1))

    group0 = A_COLS + C_W
    q_cols = [(2 * A_KV, group0)]
    q_cols += [(group0 + g * B_COLS + 2 * B_G, group0 + (g + 1) * B_COLS)
               for g in range(DIL_GROUPS)]
    q_scale = jnp.ones((1, w_in.shape[-1]), _F32)
    for c0, c1 in q_cols:
        q_scale = q_scale.at[:, c0:c1].set(Q_LOG2_SCALE)

    x2d = x.reshape(batch * seq, d)
    mem2d = mem.reshape(-1, d)
    for l in range(depth):
        x2d = _ffn(x2d, norm_gain, w1_in, w1_out, l, 0)
        za, qc, *zb = _pre(x2d, norm_gain, w_in_p, q_scale, l)
        kvc = _mem_kv(mem2d, mem_norm_gain[:, None, :], w_kv, l)
        bias_a_l = jnp.where(jnp.arange(2 * BLOCK) == 0,
                             (sinks[l].astype(_F32) * LOG2_E)[None, :, None, None], bias_a)
        ob = _attn_dil(zb, bias_b, batch, seq)
        x2d = _post(x2d, za, qc, kvc, bias_a_l, ob, norm_gain, w_g, b_gate,
                    w_a, w_b, w_c, w_out, l, batch, seq)
        x2d = _ffn(x2d, norm_gain, w2_in, w2_out, l, 4)
    return x2d.reshape(batch, seq, d)
```

```python
import functools
import math

import jax
import jax.numpy as jnp
from jax import lax
from jax.experimental import pallas as pl
from jax.experimental.pallas import tpu as pltpu

HEAD_DIM = 64
SWA_HEADS = 6
SWA_KV_HEADS = 2
SWA_GROUP = SWA_HEADS // SWA_KV_HEADS
SWA_WINDOW = 128
DIL_WINDOWS = (128, 512, 2048)
DIL_RATES = (1, 4, 16)
DIL_GROUPS = 3
DIL_HEADS_PER_GROUP = 2
MEM_HEADS = 4
N_BUCKETS = 32
MAX_DISTANCE = 2048
N_BRANCH = 3
EPS = 1e-6
BLOCK = 128
LANES = 128

A_Q = SWA_HEADS * HEAD_DIM
A_KV = SWA_KV_HEADS * HEAD_DIM
B_G = DIL_HEADS_PER_GROUP * HEAD_DIM
B_W = DIL_GROUPS * B_G
C_W = MEM_HEADS * HEAD_DIM
A_COLS = A_Q + 2 * A_KV
B_COLS = 3 * B_G
A_PAIR_ORDER = tuple(h for p in range(SWA_GROUP) for h in (p, p + SWA_GROUP))

LOG2_E = math.log2(math.e)
Q_LOG2_SCALE = HEAD_DIM ** -0.5 * LOG2_E
MASKED = -1e30

ROW_TILE = 1024
PROJ_TILE = 2048
SUB_ROWS = 256
FFN_CHUNK = 512
DIL_TILE = BLOCK * max(DIL_RATES)
VMEM_LIMIT = 56 * 1024 * 1024

_F32 = jnp.float32
_BF16 = jnp.bfloat16


def _dot(a, b):
    return jnp.dot(a, b, preferred_element_type=_F32)


def _dot_nt(a, b):
    return lax.dot_general(a, b, (((1,), (1,)), ((), ())),
                           preferred_element_type=_F32)


def _rms(x, gain):
    ms = jnp.mean(x * x, axis=-1, keepdims=True)
    return x * lax.rsqrt(ms + EPS) * gain


def _resident(shape):
    return pl.BlockSpec(shape, lambda *_: (0,) * len(shape),
                        pipeline_mode=pl.Buffered(1))


def _layer_resident(shape, layer):
    zeros = (0,) * (len(shape) - 1)
    return pl.BlockSpec((None,) + tuple(shape[1:]), lambda *_: (layer,) + zeros,
                        pipeline_mode=pl.Buffered(1))


def _params(n_axes):
    return pltpu.CompilerParams(
        dimension_semantics=("arbitrary",) * n_axes,
        vmem_limit_bytes=VMEM_LIMIT)


def _staggered(n_sub, stages):
    state = [{} for _ in range(n_sub)]
    for t in range(n_sub + len(stages) - 1):
        for s in range(n_sub):
            if 0 <= t - s < len(stages):
                stages[t - s](s, state[s])


def _sub_rows(s):
    return slice(s * SUB_ROWS, (s + 1) * SUB_ROWS)


def _ffn_kernel(x_ref, g_ref, w_in_ref, w_out_ref, o_ref, *, ffn_dim, g0):
    def norm_in(s, st):
        st["h"] = _rms(x_ref[_sub_rows(s), :], g_ref[g0:g0 + 1, :]).astype(_BF16)

    bounds = [(c0, min(c0 + FFN_CHUNK, ffn_dim)) for c0 in range(0, ffn_dim, FFN_CHUNK)]

    def step(c):
        def run(s, st):
            prev = st.pop("ab", None)
            if c < len(bounds):
                c0, c1 = bounds[c]
                st["ab"] = (_dot(st["h"], w_in_ref[:, c0:c1]),
                            _dot(st["h"], w_in_ref[:, ffn_dim + c0:ffn_dim + c1]))
            if prev is not None:
                c0, c1 = bounds[c - 1]
                a, b = prev
                act = (a * jax.nn.sigmoid(a) * b).astype(_BF16)
                part = _dot(act, w_out_ref[c0:c1, :])
                st["y"] = part if "y" not in st else st["y"] + part
        return run

    def norm_out(s, st):
        rows = _sub_rows(s)
        o_ref[rows, :] = x_ref[rows, :] + 0.5 * _rms(st["y"], g_ref[g0 + 1:g0 + 2, :])

    _staggered(ROW_TILE // SUB_ROWS,
               [norm_in] + [step(c) for c in range(len(bounds) + 1)] + [norm_out])


def _ffn(x2d, gains, w_in, w_out, layer, g0):
    t, d = x2d.shape
    ffn_dim = w_out.shape[1]
    return pl.pallas_call(
        functools.partial(_ffn_kernel, ffn_dim=ffn_dim, g0=g0),
        grid=(t // ROW_TILE,),
        in_specs=[
            pl.BlockSpec((ROW_TILE, d), lambda i: (i, 0)),
            _layer_resident(gains.shape, layer),
            _layer_resident(w_in.shape, layer),
            _layer_resident(w_out.shape, layer),
        ],
        out_specs=pl.BlockSpec((ROW_TILE, d), lambda i: (i, 0)),
        out_shape=jax.ShapeDtypeStruct((t, d), _F32),
        compiler_params=_params(1),
        name="ffn",
    )(x2d, gains, w_in, w_out)


def _pre_kernel(x_ref, g_ref, w_ref, qs_ref, a_ref, c_ref, b0_ref, b1_ref, b2_ref, z_scr):
    def norm_in(s, st):
        st["h"] = _rms(x_ref[_sub_rows(s), :], g_ref[2:3, :]).astype(_BF16)

    first = (a_ref, c_ref, b0_ref)
    first_cols = sum(ref.shape[-1] for ref in first)

    def project_rows(s, st):
        z = _dot(st["h"], w_ref[:, 0:first_cols]) * qs_ref[:, 0:first_cols]
        col = 0
        for ref in first:
            ref[_sub_rows(s), :] = z[:, col:col + ref.shape[-1]].astype(_BF16)
            col += ref.shape[-1]

    def project_residues(s, st):
        z = _dot(st["h"], w_ref[:, first_cols:]) * qs_ref[:, first_cols:]
        n_slabs = z.shape[-1] // LANES
        for n in range(n_slabs):
            z_scr[s, n] = z[:, n * LANES:(n + 1) * LANES]
        for g, ref in ((1, b1_ref), (2, b2_ref)):
            rate = DIL_RATES[g]
            per = SUB_ROWS // rate
            for c in range(rate):
                for n in range(B_COLS // LANES):
                    slab = (g - 1) * (B_COLS // LANES) + n
                    ref[c, s * per:(s + 1) * per, n * LANES:(n + 1) * LANES] = (
                        z_scr[s, slab, pl.ds(c, per, stride=rate), :].astype(_BF16))

    _staggered(PROJ_TILE // SUB_ROWS, [norm_in, project_residues, project_rows])


def _pre(x2d, gains, w_perm, q_scale, layer):
    t, d = x2d.shape
    row = lambda i: (i, 0)
    return pl.pallas_call(
        _pre_kernel,
        grid=(t // PROJ_TILE,),
        in_specs=[
            pl.BlockSpec((PROJ_TILE, d), row),
            _layer_resident(gains.shape, layer),
            _layer_resident(w_perm.shape, layer),
            _resident(q_scale.shape),
        ],
        out_specs=[
            pl.BlockSpec((PROJ_TILE, A_COLS), row),
            pl.BlockSpec((PROJ_TILE, C_W), row),
            pl.BlockSpec((PROJ_TILE, B_COLS), row),
        ] + [pl.BlockSpec((r, PROJ_TILE // r, B_COLS), lambda i: (0, i, 0))
             for r in DIL_RATES[1:]],
        out_shape=[
            jax.ShapeDtypeStruct((t, A_COLS), _BF16),
            jax.ShapeDtypeStruct((t, C_W), _BF16),
            jax.ShapeDtypeStruct((t, B_COLS), _BF16),
        ] + [jax.ShapeDtypeStruct((r, t // r, B_COLS), _BF16) for r in DIL_RATES[1:]],
        scratch_shapes=[
            pltpu.VMEM((PROJ_TILE // SUB_ROWS, (DIL_GROUPS - 1) * B_COLS // LANES,
                        SUB_ROWS, LANES), _F32),
        ],
        compiler_params=_params(1),
        name="mix_in_proj",
    )(x2d, gains, w_perm, q_scale)


def _mem_kernel(m_ref, g_ref, w_ref, o_ref):
    h = _rms(m_ref[...], g_ref[...]).astype(_BF16)
    o_ref[...] = _dot(h, w_ref[...]).astype(_BF16)


def _mem_kv(mem2d, gains, w, layer):
    rows = mem2d.shape[0]
    return pl.pallas_call(
        _mem_kernel,
        grid=(1,),
        in_specs=[_resident(mem2d.shape),
                  _layer_resident(gains.shape, layer),
                  _layer_resident(w.shape, layer)],
        out_specs=pl.BlockSpec((rows, 2 * C_W), lambda i: (0, 0)),
        out_shape=jax.ShapeDtypeStruct((rows, 2 * C_W), _BF16),
        compiler_params=_params(1),
        name="mem_kv",
    )(mem2d, gains, w)


def _low_lanes():
    return lax.broadcasted_iota(jnp.int32, (1, LANES), 1) < HEAD_DIM


def _pair_scores(q, k, bias_lo=None, bias_hi=None):
    low = _low_lanes()
    zero = jnp.zeros((), k.dtype)
    if q.shape[0] <= k.shape[0]:
        s_lo = _dot_nt(jnp.where(low, q, zero), k)
        s_hi = _dot_nt(jnp.where(low, zero, q), k)
    else:
        s_lo = _dot_nt(q, jnp.where(low, k, zero))
        s_hi = _dot_nt(q, jnp.where(low, zero, k))
    if bias_lo is not None:
        s_lo = s_lo + bias_lo
        s_hi = s_hi + bias_hi
    return s_lo, s_hi


def _pair_softmax_pv(scores, v):
    s_lo, s_hi = scores
    low = _low_lanes()
    zero = jnp.zeros((), v.dtype)
    lane = lax.broadcasted_iota(jnp.int32, v.shape, 1)
    ones_lo = jnp.where(lane < HEAD_DIM, 1.0, 0.0).astype(v.dtype)
    ones_hi = jnp.where(lane < HEAD_DIM, 0.0, 1.0).astype(v.dtype)
    m_lo = jnp.max(s_lo, axis=-1, keepdims=True)
    m_hi = jnp.max(s_hi, axis=-1, keepdims=True)
    p_lo = jnp.exp2(s_lo - m_lo).astype(_BF16)
    p_hi = jnp.exp2(s_hi - m_hi).astype(_BF16)
    rhs = jnp.concatenate([
        jnp.concatenate([jnp.where(low, v, zero), ones_lo], axis=1),
        jnp.concatenate([jnp.where(low, zero, v), ones_hi], axis=1)],
        axis=0)
    res = _dot(jnp.concatenate([p_lo, p_hi], axis=1), rhs)
    return res[:, 0:LANES], res[:, LANES:2 * LANES], m_lo, m_hi


PIPELINE_LOOKAHEAD = 3


def _software_pipeline(tiles):
    pending = []
    for n, (scores_fn, _) in enumerate(tiles):
        pending.append(scores_fn())
        if n >= PIPELINE_LOOKAHEAD:
            done = n - PIPELINE_LOOKAHEAD
            tiles[done][1](pending[done])
            pending[done] = None
    for done in range(max(len(tiles) - PIPELINE_LOOKAHEAD, 0), len(tiles)):
        tiles[done][1](pending[done])


def _window_cross_tiles(cur_ref, prev_ref, qc_ref, kvc_ref, bias0_ref, bias_ref,
                        oa_ref, oc_ref):
    rows_a = SWA_GROUP * BLOCK
    sink_row = lax.broadcasted_iota(jnp.int32, (BLOCK, A_KV), 0) == 0

    def window_tile(j):
        rows = slice(j * BLOCK, (j + 1) * BLOCK)
        prev_rows = slice((j - 1) * BLOCK, j * BLOCK)
        bias = bias0_ref if j == 0 else bias_ref

        def kv_window(c0, c1):
            prev = prev_ref[:, c0:c1] if j == 0 else cur_ref[prev_rows, c0:c1]
            prev = jnp.where(sink_row, jnp.zeros((), prev.dtype), prev)
            return jnp.concatenate([prev, cur_ref[rows, c0:c1]], axis=0)

        def scores():
            q = jnp.concatenate(
                [cur_ref[rows, 2 * A_KV + p * LANES:2 * A_KV + (p + 1) * LANES]
                 for p in range(SWA_GROUP)], axis=0)
            return _pair_scores(q, kv_window(0, A_KV),
                                bias[0:SWA_GROUP].reshape(rows_a, 2 * BLOCK),
                                bias[SWA_GROUP:SWA_HEADS].reshape(rows_a, 2 * BLOCK))

        def finish(s):
            out, den, _, _ = _pair_softmax_pv(s, kv_window(A_KV, 2 * A_KV))
            o = out / den
            for p in range(SWA_GROUP):
                oa_ref[rows, p * LANES:(p + 1) * LANES] = (
                    o[p * BLOCK:(p + 1) * BLOCK]).astype(_BF16)

        return scores, finish

    def cross_tile(p, rows):
        cols = slice(p * LANES, (p + 1) * LANES)

        def scores():
            return _pair_scores(qc_ref[rows, cols], kvc_ref[:, cols])

        def finish(s):
            out, den, _, _ = _pair_softmax_pv(
                s, kvc_ref[:, C_W + p * LANES:C_W + (p + 1) * LANES])
            oc_ref[rows, cols] = (out / den).astype(_BF16)

        return scores, finish

    return window_tile, cross_tile


def _dil_kernel(*refs):
    cur_refs, prev_refs = refs[0:2 * DIL_GROUPS:2], refs[1:2 * DIL_GROUPS:2]
    bias0_ref, bias_ref, ob_ref, o_scr, d_scr, m_scr = refs[2 * DIL_GROUPS:]
    low = _low_lanes()
    k_cols, v_cols, q_cols = (slice(n * B_G, (n + 1) * B_G) for n in range(3))

    def tile(g, sb, c):
        rate = DIL_RATES[g]
        rows = slice(sb * BLOCK, (sb + 1) * BLOCK)
        bias = bias0_ref if sb == 0 else bias_ref
        if rate == 1:
            out_rows = pl.ds(sb * BLOCK, BLOCK)
            cur = lambda r, cols: cur_refs[g][r, cols]
            prev = lambda cols: prev_refs[g][:, cols]
        else:
            out_rows = pl.ds(sb * BLOCK * rate + c, BLOCK, stride=rate)
            cur = lambda r, cols: cur_refs[g][c, r, cols]
            prev = lambda cols: prev_refs[g][c, :, cols]

        def window(cols):
            before = prev(cols) if sb == 0 else cur(slice((sb - 1) * BLOCK, sb * BLOCK), cols)
            return jnp.concatenate([before, cur(rows, cols)], axis=0)

        def scores():
            return _pair_scores(cur(rows, q_cols), window(k_cols), bias[g, 0], bias[g, 1])

        def finish(s):
            out, den, m_lo, m_hi = _pair_softmax_pv(s, window(v_cols))
            o_scr[g, out_rows, :] = out
            d_scr[g, out_rows, :] = den
            m_scr[g, out_rows, :] = jnp.where(low, m_lo, m_hi)

        return scores, finish

    per_group = [[tile(g, sb, c)
                  for sb in range(DIL_TILE // (BLOCK * rate)) for c in range(rate)]
                 for g, rate in enumerate(DIL_RATES)]
    _software_pipeline([t for trio in zip(*per_group) for t in trio])

    top = functools.reduce(jnp.maximum, [m_scr[g] for g in range(DIL_GROUPS)])
    scale = [jnp.exp2(m_scr[g] - top) for g in range(DIL_GROUPS)]
    total = sum(d_scr[g] * scale[g] for g in range(DIL_GROUPS))
    for g in range(DIL_GROUPS):
        ob_ref[:, g * B_G:(g + 1) * B_G] = (o_scr[g] * (scale[g] / total)).astype(_BF16)


def _attn_dil(zb, bias_b, batch, seq):
    t = zb[0].shape[0]
    tiles = seq // DIL_TILE
    row = lambda b, i: (b * tiles + i, 0)
    bias_shape = (None,) + bias_b.shape[1:]
    operands, specs = [], []
    for z, rate in zip(zb, DIL_RATES):
        rows = DIL_TILE // rate
        prev_block = lambda b, i, n=rows // BLOCK: jnp.maximum((b * tiles + i) * n - 1, 0)
        operands += [z, z]
        if rate == 1:
            specs += [pl.BlockSpec((rows, B_COLS), row),
                      pl.BlockSpec((BLOCK, B_COLS),
                                   lambda b, i, f=prev_block: (f(b, i), 0))]
        else:
            specs += [pl.BlockSpec((rate, rows, B_COLS), lambda b, i: (0, b * tiles + i, 0)),
                      pl.BlockSpec((rate, BLOCK, B_COLS),
                                   lambda b, i, f=prev_block: (0, f(b, i), 0))]
    return pl.pallas_call(
        _dil_kernel,
        grid=(batch, tiles),
        in_specs=specs + [
            pl.BlockSpec(bias_shape, lambda b, i: (jnp.minimum(i, 1), 0, 0, 0, 0)),
            pl.BlockSpec(bias_shape, lambda b, i: (1, 0, 0, 0, 0)),
        ],
        out_specs=pl.BlockSpec((DIL_TILE, B_W), row),
        out_shape=jax.ShapeDtypeStruct((t, B_W), _BF16),
        scratch_shapes=[
            pltpu.VMEM((DIL_GROUPS, DIL_TILE, B_G), _F32),
            pltpu.VMEM((DIL_GROUPS, DIL_TILE, B_G), _F32),
            pltpu.VMEM((DIL_GROUPS, DIL_TILE, B_G), _F32),
        ],
        compiler_params=_params(2),
        name="attn_dilated",
    )(*operands, bias_b, bias_b)


def _post_kernel(x_ref, za_ref, za_prev_ref, qc_ref, kvc_ref, bias0_ref, bias_ref, ob_ref,
                 g_ref, wg_ref, bg_ref, wa_ref, wb_ref, wc_ref, wo_ref, o_ref,
                 oa_ref, oc_ref):
    d = x_ref.shape[-1]
    window_tile, cross_tile = _window_cross_tiles(
        za_ref, za_prev_ref, qc_ref, kvc_ref, bias0_ref, bias_ref, oa_ref, oc_ref)
    blocks_per_sub = SUB_ROWS // BLOCK

    def attend(s, st):
        _software_pipeline(
            [window_tile(j) for j in range(s * blocks_per_sub, (s + 1) * blocks_per_sub)]
            + [cross_tile(p, _sub_rows(s)) for p in range(MEM_HEADS // 2)])

    def norm_in(s, st):
        st["h"] = _rms(x_ref[_sub_rows(s), :], g_ref[2:3, :]).astype(_BF16)

    def branch(n, o_branch_ref, w_branch_ref):
        def run(s, st):
            gate = jax.nn.sigmoid(_dot(st["h"], wg_ref[:, n * d:(n + 1) * d])
                                  + bg_ref[n:n + 1, :])
            term = gate * _dot(o_branch_ref[_sub_rows(s), :], w_branch_ref[...])
            st["merged"] = term if n == 0 else st["merged"] + term
        return run

    def project(s, st):
        st["y"] = _dot(st["merged"].astype(_BF16), wo_ref[...])

    def norm_out(s, st):
        rows = _sub_rows(s)
        o_ref[rows, :] = x_ref[rows, :] + _rms(st["y"], g_ref[3:4, :])

    _staggered(ROW_TILE // SUB_ROWS,
               [attend, norm_in, branch(0, oa_ref, wa_ref), branch(1, ob_ref, wb_ref),
                branch(2, oc_ref, wc_ref), project, norm_out])


def _post(x2d, za, qc, kvc, bias_a, ob, gains, w_gate, b_gate, w_a, w_b, w_c, w_o,
          layer, batch, seq):
    t, d = x2d.shape
    tiles = seq // ROW_TILE
    mem_len = kvc.shape[0] // batch
    row = lambda b, i: (b * tiles + i, 0)
    bias_shape = (None,) + bias_a.shape[1:]
    return pl.pallas_call(
        _post_kernel,
        grid=(batch, tiles),
        in_specs=[
            pl.BlockSpec((ROW_TILE, d), row),
            pl.BlockSpec((ROW_TILE, A_COLS), row),
            pl.BlockSpec((BLOCK, 2 * A_KV), lambda b, i: (
                jnp.maximum((b * tiles + i) * (ROW_TILE // BLOCK) - 1, 0), 0)),
            pl.BlockSpec((ROW_TILE, C_W), row),
            pl.BlockSpec((mem_len, 2 * C_W), lambda b, i: (b, 0)),
            pl.BlockSpec(bias_shape, lambda b, i: (jnp.minimum(i, 1), 0, 0, 0)),
            pl.BlockSpec(bias_shape, lambda b, i: (1, 0, 0, 0)),
            pl.BlockSpec((ROW_TILE, B_W), row),
            _layer_resident(gains.shape, layer),
            _layer_resident(w_gate.shape, layer),
            _layer_resident(b_gate.shape, layer),
            _layer_resident(w_a.shape, layer),
            _layer_resident(w_b.shape, layer),
            _layer_resident(w_c.shape, layer),
            _layer_resident(w_o.shape, layer),
        ],
        out_specs=pl.BlockSpec((ROW_TILE, d), row),
        out_shape=jax.ShapeDtypeStruct((t, d), _F32),
        scratch_shapes=[pltpu.VMEM((ROW_TILE, A_Q), _BF16),
                        pltpu.VMEM((ROW_TILE, C_W), _BF16)],
        compiler_params=_params(2),
        name="attn_window_cross_out_proj",
    )(x2d, za, za, qc, kvc, bias_a, bias_a, ob, gains, w_gate, b_gate, w_a, w_b, w_c, w_o)


def _t5_bucket(dist):
    max_exact = N_BUCKETS // 2
    d = jnp.maximum(dist, 1).astype(_F32)
    large = max_exact + (jnp.log(d / max_exact) / math.log(MAX_DISTANCE / max_exact)
                         * (N_BUCKETS - max_exact)).astype(jnp.int32)
    large = jnp.minimum(large, N_BUCKETS - 1)
    return jnp.where(dist < max_exact, dist, large)


def _band_bias(table, head0, n_heads, rate, max_dist):
    row = jnp.arange(BLOCK)[:, None]
    col = jnp.arange(2 * BLOCK)[None, :]
    dist = row + BLOCK - col
    bucket = _t5_bucket(jnp.maximum(dist, 0) * rate)
    hit = bucket[None, None] == jnp.arange(N_BUCKETS)[None, :, None, None]
    vals = table.T[head0:head0 + n_heads, :, None, None]
    bias = jnp.sum(jnp.where(hit, vals, 0.0), axis=1) * LOG2_E
    valid = (dist >= 0) & (dist <= max_dist)
    later = jnp.where(valid[None], bias, MASKED)
    first = jnp.where((valid & (col >= BLOCK))[None], bias, MASKED)
    return jnp.stack([first, later])


def _permute_in_proj(w):
    hd = HEAD_DIM
    qa, ka, va = w[..., 0:A_Q], w[..., A_Q:A_Q + A_KV], w[..., A_Q + A_KV:A_Q + 2 * A_KV]
    off = A_Q + 2 * A_KV
    qb, kb, vb = (w[..., off + n * B_W:off + (n + 1) * B_W] for n in range(3))
    qc = w[..., off + 3 * B_W:]
    grp = lambda w3, g: w3[..., g * B_G:(g + 1) * B_G]
    cols = [ka, va] + [qa[..., h * hd:(h + 1) * hd] for h in A_PAIR_ORDER] + [qc]
    for g in range(DIL_GROUPS):
        cols += [grp(kb, g), grp(vb, g), grp(qb, g)]
    return jnp.concatenate(cols, axis=-1)


def kernel(x, mem, rel_bias, norm_gain, mem_norm_gain, w_ffn1_in, w_ffn1_out, w_in,
           sinks, w_mem_kv, w_gate, b_gate, w_br_a, w_br_b, w_br_c, w_o,
           w_ffn2_in, w_ffn2_out):
    batch, seq, d = x.shape
    depth = norm_gain.shape[0]
    assert seq % DIL_TILE == 0 and seq % ROW_TILE == 0 and (batch * seq) % PROJ_TILE == 0

    table = rel_bias.astype(_F32)
    bias_a = _band_bias(table, 0, SWA_HEADS, 1, SWA_WINDOW - 1)
    bias_b = jnp.stack([
        _band_bias(table, SWA_HEADS + g * DIL_HEADS_PER_GROUP, DIL_HEADS_PER_GROUP,
                   DIL_RATES[g], DIL_WINDOWS[g] // DIL_RATES[g])
        for g in range(DIL_GROUPS)], axis=1)

    bf = lambda w: w.astype(_BF16)
    w1_in, w1_out, w2_in, w2_out = bf(w_ffn1_in), bf(w_ffn1_out), bf(w_ffn2_in), bf(w_ffn2_out)
    w_in_p = _permute_in_proj(bf(w_in))
    w_kv, w_g, w_b, w_c, w_out = bf(w_mem_kv), bf(w_gate), bf(w_br_b), bf(w_br_c), bf(w_o)
    w_a = bf(jnp.concatenate(
        [w_br_a[:, h * HEAD_DIM:(h + 1) * HEAD_DIM] for h in A_PAIR_ORDER], axis=1))

    group0 = A_COLS + C_W
    q_cols = [(2 * A_KV, group0)]
    q_cols += [(group0 + g * B_COLS + 2 * B_G, group0 + (g + 1) * B_COLS)
               for g in range(DIL_GROUPS)]
    q_scale = jnp.ones((1, w_in.shape[-1]), _F32)
    for c0, c1 in q_cols:
        q_scale = q_scale.at[:, c0:c1].set(Q_LOG2_SCALE)

    x2d = x.reshape(batch * seq, d)
    mem2d = mem.reshape(-1, d)
    for l in range(depth):
        x2d = _ffn(x2d, norm_gain, w1_in, w1_out, l, 0)
        za, qc, *zb = _pre(x2d, norm_gain, w_in_p, q_scale, l)
        kvc = _mem_kv(mem2d, mem_norm_gain[:, None, :], w_kv, l)
        bias_a_l = jnp.where(jnp.arange(2 * BLOCK) == 0,
                             (sinks[l].astype(_F32) * LOG2_E)[None, :, None, None], bias_a)
        ob = _attn_dil(zb, bias_b, batch, seq)
        x2d = _post(x2d, za, qc, kvc, bias_a_l, ob, norm_gain, w_g, b_gate,
                    w_a, w_b, w_c, w_out, l, batch, seq)
        x2d = _ffn(x2d, norm_gain, w2_in, w2_out, l, 4)
    return x2d.reshape(batch, seq, d)
```

```python
import functools
import math

import jax
import jax.numpy as jnp
from jax import lax
from jax.experimental import pallas as pl
from jax.experimental.pallas import tpu as pltpu

HEAD_DIM = 64
SWA_HEADS = 6
SWA_KV_HEADS = 2
SWA_GROUP = SWA_HEADS // SWA_KV_HEADS
SWA_WINDOW = 128
DIL_WINDOWS = (128, 512, 2048)
DIL_RATES = (1, 4, 16)
DIL_GROUPS = 3
DIL_HEADS_PER_GROUP = 2
MEM_HEADS = 4
N_BUCKETS = 32
MAX_DISTANCE = 2048
N_BRANCH = 3
EPS = 1e-6
BLOCK = 128
LANES = 128

A_Q = SWA_HEADS * HEAD_DIM
A_KV = SWA_KV_HEADS * HEAD_DIM
B_G = DIL_HEADS_PER_GROUP * HEAD_DIM
B_W = DIL_GROUPS * B_G
C_W = MEM_HEADS * HEAD_DIM
A_COLS = A_Q + 2 * A_KV
B_COLS = 3 * B_G
A_PAIR_ORDER = tuple(h for p in range(SWA_GROUP) for h in (p, p + SWA_GROUP))

LOG2_E = math.log2(math.e)
Q_LOG2_SCALE = HEAD_DIM ** -0.5 * LOG2_E
MASKED = -1e30

ROW_TILE = 1024
PROJ_TILE = 2048
SUB_ROWS = 256
FFN_CHUNK = 512
DIL_TILE = BLOCK * max(DIL_RATES)
VMEM_LIMIT = 56 * 1024 * 1024

_F32 = jnp.float32
_BF16 = jnp.bfloat16


def _dot(a, b):
    return jnp.dot(a, b, preferred_element_type=_F32)


def _dot_nt(a, b):
    return lax.dot_general(a, b, (((1,), (1,)), ((), ())),
                           preferred_element_type=_F32)


def _rms(x, gain):
    ms = jnp.mean(x * x, axis=-1, keepdims=True)
    return x * lax.rsqrt(ms + EPS) * gain


def _resident(shape):
    return pl.BlockSpec(shape, lambda *_: (0,) * len(shape),
                        pipeline_mode=pl.Buffered(1))


def _layer_resident(shape, layer):
    zeros = (0,) * (len(shape) - 1)
    return pl.BlockSpec((None,) + tuple(shape[1:]), lambda *_: (layer,) + zeros,
                        pipeline_mode=pl.Buffered(1))


def _params(n_axes):
    return pltpu.CompilerParams(
        dimension_semantics=("arbitrary",) * n_axes,
        vmem_limit_bytes=VMEM_LIMIT)


def _staggered(n_sub, stages):
    state = [{} for _ in range(n_sub)]
    for t in range(n_sub + len(stages) - 1):
        for s in range(n_sub):
            if 0 <= t - s < len(stages):
                stages[t - s](s, state[s])


def _sub_rows(s):
    return slice(s * SUB_ROWS, (s + 1) * SUB_ROWS)


def _ffn_kernel(x_ref, g_ref, w_in_ref, w_out_ref, o_ref, *, ffn_dim, g0):
    def norm_in(s, st):
        st["h"] = _rms(x_ref[_sub_rows(s), :], g_ref[g0:g0 + 1, :]).astype(_BF16)

    bounds = [(c0, min(c0 + FFN_CHUNK, ffn_dim)) for c0 in range(0, ffn_dim, FFN_CHUNK)]

    def step(c):
        def run(s, st):
            prev = st.pop("ab", None)
            if c < len(bounds):
                c0, c1 = bounds[c]
                st["ab"] = (_dot(st["h"], w_in_ref[:, c0:c1]),
                            _dot(st["h"], w_in_ref[:, ffn_dim + c0:ffn_dim + c1]))
            if prev is not None:
                c0, c1 = bounds[c - 1]
                a, b = prev
                act = (a * jax.nn.sigmoid(a) * b).astype(_BF16)
                part = _dot(act, w_out_ref[c0:c1, :])
                st["y"] = part if "y" not in st else st["y"] + part
        return run

    def norm_out(s, st):
        rows = _sub_rows(s)
        o_ref[rows, :] = x_ref[rows, :] + 0.5 * _rms(st["y"], g_ref[g0 + 1:g0 + 2, :])

    _staggered(ROW_TILE // SUB_ROWS,
               [norm_in] + [step(c) for c in range(len(bounds) + 1)] + [norm_out])


def _ffn(x2d, gains, w_in, w_out, layer, g0):
    t, d = x2d.shape
    ffn_dim = w_out.shape[1]
    return pl.pallas_call(
        functools.partial(_ffn_kernel, ffn_dim=ffn_dim, g0=g0),
        grid=(t // ROW_TILE,),
        in_specs=[
            pl.BlockSpec((ROW_TILE, d), lambda i: (i, 0)),
            _layer_resident(gains.shape, layer),
            _layer_resident(w_in.shape, layer),
            _layer_resident(w_out.shape, layer),
        ],
        out_specs=pl.BlockSpec((ROW_TILE, d), lambda i: (i, 0)),
        out_shape=jax.ShapeDtypeStruct((t, d), _F32),
        compiler_params=_params(1),
        name="ffn",
    )(x2d, gains, w_in, w_out)


def _pre_kernel(x_ref, g_ref, w_ref, qs_ref, a_ref, c_ref, b0_ref, b1_ref, b2_ref, z_scr):
    def norm_in(s, st):
        st["h"] = _rms(x_ref[_sub_rows(s), :], g_ref[2:3, :]).astype(_BF16)

    first = (a_ref, c_ref, b0_ref)
    first_cols = sum(ref.shape[-1] for ref in first)

    def project_rows(s, st):
        z = _dot(st["h"], w_ref[:, 0:first_cols]) * qs_ref[:, 0:first_cols]
        col = 0
        for ref in first:
            ref[_sub_rows(s), :] = z[:, col:col + ref.shape[-1]].astype(_BF16)
            col += ref.shape[-1]

    def project_residues(s, st):
        z = _dot(st["h"], w_ref[:, first_cols:]) * qs_ref[:, first_cols:]
        n_slabs = z.shape[-1] // LANES
        for n in range(n_slabs):
            z_scr[s, n] = z[:, n * LANES:(n + 1) * LANES]

    def relayout_residues(s, st):
        for g, ref in ((1, b1_ref), (2, b2_ref)):
            rate = DIL_RATES[g]
            per = SUB_ROWS // rate
            for c in range(rate):
                for n in range(B_COLS // LANES):
                    slab = (g - 1) * (B_COLS // LANES) + n
                    ref[c, s * per:(s + 1) * per, n * LANES:(n + 1) * LANES] = (
                        z_scr[s, slab, pl.ds(c, per, stride=rate), :].astype(_BF16))

    _staggered(PROJ_TILE // SUB_ROWS,
               [norm_in, project_residues, relayout_residues, project_rows])


def _pre(x2d, gains, w_perm, q_scale, layer):
    t, d = x2d.shape
    row = lambda i: (i, 0)
    return pl.pallas_call(
        _pre_kernel,
        grid=(t // PROJ_TILE,),
        in_specs=[
            pl.BlockSpec((PROJ_TILE, d), row),
            _layer_resident(gains.shape, layer),
            _layer_resident(w_perm.shape, layer),
            _resident(q_scale.shape),
        ],
        out_specs=[
            pl.BlockSpec((PROJ_TILE, A_COLS), row),
            pl.BlockSpec((PROJ_TILE, C_W), row),
            pl.BlockSpec((PROJ_TILE, B_COLS), row),
        ] + [pl.BlockSpec((r, PROJ_TILE // r, B_COLS), lambda i: (0, i, 0))
             for r in DIL_RATES[1:]],
        out_shape=[
            jax.ShapeDtypeStruct((t, A_COLS), _BF16),
            jax.ShapeDtypeStruct((t, C_W), _BF16),
            jax.ShapeDtypeStruct((t, B_COLS), _BF16),
        ] + [jax.ShapeDtypeStruct((r, t // r, B_COLS), _BF16) for r in DIL_RATES[1:]],
        scratch_shapes=[
            pltpu.VMEM((PROJ_TILE // SUB_ROWS, (DIL_GROUPS - 1) * B_COLS // LANES,
                        SUB_ROWS, LANES), _F32),
        ],
        compiler_params=_params(1),
        name="mix_in_proj",
    )(x2d, gains, w_perm, q_scale)


def _mem_kernel(m_ref, g_ref, w_ref, o_ref):
    h = _rms(m_ref[...], g_ref[...]).astype(_BF16)
    o_ref[...] = _dot(h, w_ref[...]).astype(_BF16)


def _mem_kv(mem2d, gains, w, layer):
    rows = mem2d.shape[0]
    return pl.pallas_call(
        _mem_kernel,
        grid=(1,),
        in_specs=[_resident(mem2d.shape),
                  _layer_resident(gains.shape, layer),
                  _layer_resident(w.shape, layer)],
        out_specs=pl.BlockSpec((rows, 2 * C_W), lambda i: (0, 0)),
        out_shape=jax.ShapeDtypeStruct((rows, 2 * C_W), _BF16),
        compiler_params=_params(1),
        name="mem_kv",
    )(mem2d, gains, w)


def _low_lanes():
    return lax.broadcasted_iota(jnp.int32, (1, LANES), 1) < HEAD_DIM


def _pair_scores(q, k, bias_lo=None, bias_hi=None):
    low = _low_lanes()
    zero = jnp.zeros((), k.dtype)
    if q.shape[0] <= k.shape[0]:
        s_lo = _dot_nt(jnp.where(low, q, zero), k)
        s_hi = _dot_nt(jnp.where(low, zero, q), k)
    else:
        s_lo = _dot_nt(q, jnp.where(low, k, zero))
        s_hi = _dot_nt(q, jnp.where(low, zero, k))
    if bias_lo is not None:
        s_lo = s_lo + bias_lo
        s_hi = s_hi + bias_hi
    return s_lo, s_hi


def _pair_softmax_pv(scores, v):
    s_lo, s_hi = scores
    low = _low_lanes()
    zero = jnp.zeros((), v.dtype)
    lane = lax.broadcasted_iota(jnp.int32, v.shape, 1)
    ones_lo = jnp.where(lane < HEAD_DIM, 1.0, 0.0).astype(v.dtype)
    ones_hi = jnp.where(lane < HEAD_DIM, 0.0, 1.0).astype(v.dtype)
    m_lo = jnp.max(s_lo, axis=-1, keepdims=True)
    m_hi = jnp.max(s_hi, axis=-1, keepdims=True)
    p_lo = jnp.exp2(s_lo - m_lo).astype(_BF16)
    p_hi = jnp.exp2(s_hi - m_hi).astype(_BF16)
    rhs = jnp.concatenate([
        jnp.concatenate([jnp.where(low, v, zero), ones_lo], axis=1),
        jnp.concatenate([jnp.where(low, zero, v), ones_hi], axis=1)],
        axis=0)
    res = _dot(jnp.concatenate([p_lo, p_hi], axis=1), rhs)
    return res[:, 0:LANES], res[:, LANES:2 * LANES], m_lo, m_hi


WINDOW_LOOKAHEAD = 3
DILATED_LOOKAHEAD = 2


def _software_pipeline(tiles, lookahead):
    pending = []
    for n, (scores_fn, _) in enumerate(tiles):
        pending.append(scores_fn())
        if n >= lookahead:
            done = n - lookahead
            tiles[done][1](pending[done])
            pending[done] = None
    for done in range(max(len(tiles) - lookahead, 0), len(tiles)):
        tiles[done][1](pending[done])


def _window_cross_tiles(cur_ref, prev_ref, qc_ref, kvc_ref, bias0_ref, bias_ref,
                        oa_ref, oc_ref):
    rows_a = SWA_GROUP * BLOCK
    sink_row = lax.broadcasted_iota(jnp.int32, (BLOCK, A_KV), 0) == 0

    def window_tile(j):
        rows = slice(j * BLOCK, (j + 1) * BLOCK)
        prev_rows = slice((j - 1) * BLOCK, j * BLOCK)
        bias = bias0_ref if j == 0 else bias_ref

        def kv_window(c0, c1):
            prev = prev_ref[:, c0:c1] if j == 0 else cur_ref[prev_rows, c0:c1]
            prev = jnp.where(sink_row, jnp.zeros((), prev.dtype), prev)
            return jnp.concatenate([prev, cur_ref[rows, c0:c1]], axis=0)

        def scores():
            q = jnp.concatenate(
                [cur_ref[rows, 2 * A_KV + p * LANES:2 * A_KV + (p + 1) * LANES]
                 for p in range(SWA_GROUP)], axis=0)
            return _pair_scores(q, kv_window(0, A_KV),
                                bias[0:SWA_GROUP].reshape(rows_a, 2 * BLOCK),
                                bias[SWA_GROUP:SWA_HEADS].reshape(rows_a, 2 * BLOCK))

        def finish(s):
            out, den, _, _ = _pair_softmax_pv(s, kv_window(A_KV, 2 * A_KV))
            o = out / den
            for p in range(SWA_GROUP):
                oa_ref[rows, p * LANES:(p + 1) * LANES] = (
                    o[p * BLOCK:(p + 1) * BLOCK]).astype(_BF16)

        return scores, finish

    def cross_tile(p, rows):
        cols = slice(p * LANES, (p + 1) * LANES)

        def scores():
            return _pair_scores(qc_ref[rows, cols], kvc_ref[:, cols])

        def finish(s):
            out, den, _, _ = _pair_softmax_pv(
                s, kvc_ref[:, C_W + p * LANES:C_W + (p + 1) * LANES])
            oc_ref[rows, cols] = (out / den).astype(_BF16)

        return scores, finish

    return window_tile, cross_tile


def _dil_kernel(*refs):
    cur_refs, prev_refs = refs[0:2 * DIL_GROUPS:2], refs[1:2 * DIL_GROUPS:2]
    bias0_ref, bias_ref, ob_ref, o_scr, d_scr, m_scr = refs[2 * DIL_GROUPS:]
    low = _low_lanes()
    k_cols, v_cols, q_cols = (slice(n * B_G, (n + 1) * B_G) for n in range(3))

    def tile(g, sb, c):
        rate = DIL_RATES[g]
        rows = slice(sb * BLOCK, (sb + 1) * BLOCK)
        bias = bias0_ref if sb == 0 else bias_ref
        if rate == 1:
            out_rows = pl.ds(sb * BLOCK, BLOCK)
            cur = lambda r, cols: cur_refs[g][r, cols]
            prev = lambda cols: prev_refs[g][:, cols]
        else:
            out_rows = pl.ds(sb * BLOCK * rate + c, BLOCK, stride=rate)
            cur = lambda r, cols: cur_refs[g][c, r, cols]
            prev = lambda cols: prev_refs[g][c, :, cols]

        def window(cols):
            before = prev(cols) if sb == 0 else cur(slice((sb - 1) * BLOCK, sb * BLOCK), cols)
            return jnp.concatenate([before, cur(rows, cols)], axis=0)

        def scores():
            return _pair_scores(cur(rows, q_cols), window(k_cols), bias[g, 0], bias[g, 1])

        def finish(s):
            out, den, m_lo, m_hi = _pair_softmax_pv(s, window(v_cols))
            o_scr[g, out_rows, :] = out
            d_scr[g, out_rows, :] = den
            m_scr[g, out_rows, :] = jnp.where(low, m_lo, m_hi)

        return scores, finish

    per_group = [[tile(g, sb, c)
                  for sb in range(DIL_TILE // (BLOCK * rate)) for c in range(rate)]
                 for g, rate in enumerate(DIL_RATES)]
    _software_pipeline([t for trio in zip(*per_group) for t in trio], DILATED_LOOKAHEAD)

    top = functools.reduce(jnp.maximum, [m_scr[g] for g in range(DIL_GROUPS)])
    scale = [jnp.exp2(m_scr[g] - top) for g in range(DIL_GROUPS)]
    total = sum(d_scr[g] * scale[g] for g in range(DIL_GROUPS))
    for g in range(DIL_GROUPS):
        ob_ref[:, g * B_G:(g + 1) * B_G] = (o_scr[g] * (scale[g] / total)).astype(_BF16)


def _attn_dil(zb, bias_b, batch, seq):
    t = zb[0].shape[0]
    tiles = seq // DIL_TILE
    row = lambda b, i: (b * tiles + i, 0)
    bias_shape = (None,) + bias_b.shape[1:]
    operands, specs = [], []
    for z, rate in zip(zb, DIL_RATES):
        rows = DIL_TILE // rate
        prev_block = lambda b, i, n=rows // BLOCK: jnp.maximum((b * tiles + i) * n - 1, 0)
        operands += [z, z]
        if rate == 1:
            specs += [pl.BlockSpec((rows, B_COLS), row),
                      pl.BlockSpec((BLOCK, B_COLS),
                                   lambda b, i, f=prev_block: (f(b, i), 0))]
        else:
            specs += [pl.BlockSpec((rate, rows, B_COLS), lambda b, i: (0, b * tiles + i, 0)),
                      pl.BlockSpec((rate, BLOCK, B_COLS),
                                   lambda b, i, f=prev_block: (0, f(b, i), 0))]
    return pl.pallas_call(
        _dil_kernel,
        grid=(batch, tiles),
        in_specs=specs + [
            pl.BlockSpec(bias_shape, lambda b, i: (jnp.minimum(i, 1), 0, 0, 0, 0)),
            pl.BlockSpec(bias_shape, lambda b, i: (1, 0, 0, 0, 0)),
        ],
        out_specs=pl.BlockSpec((DIL_TILE, B_W), row),
        out_shape=jax.ShapeDtypeStruct((t, B_W), _BF16),
        scratch_shapes=[
            pltpu.VMEM((DIL_GROUPS, DIL_TILE, B_G), _F32),
            pltpu.VMEM((DIL_GROUPS, DIL_TILE, B_G), _F32),
            pltpu.VMEM((DIL_GROUPS, DIL_TILE, B_G), _F32),
        ],
        compiler_params=_params(2),
        name="attn_dilated",
    )(*operands, bias_b, bias_b)


def _post_kernel(x_ref, za_ref, za_prev_ref, qc_ref, kvc_ref, bias0_ref, bias_ref, ob_ref,
                 g_ref, wg_ref, bg_ref, wa_ref, wb_ref, wc_ref, wo_ref, o_ref,
                 oa_ref, oc_ref):
    d = x_ref.shape[-1]
    window_tile, cross_tile = _window_cross_tiles(
        za_ref, za_prev_ref, qc_ref, kvc_ref, bias0_ref, bias_ref, oa_ref, oc_ref)
    blocks_per_sub = SUB_ROWS // BLOCK

    def attend(s, st):
        _software_pipeline(
            [window_tile(j) for j in range(s * blocks_per_sub, (s + 1) * blocks_per_sub)]
            + [cross_tile(p, _sub_rows(s)) for p in range(MEM_HEADS // 2)],
            WINDOW_LOOKAHEAD)

    def norm_in(s, st):
        st["h"] = _rms(x_ref[_sub_rows(s), :], g_ref[2:3, :]).astype(_BF16)

    def branch(n, o_branch_ref, w_branch_ref):
        def run(s, st):
            gate = jax.nn.sigmoid(_dot(st["h"], wg_ref[:, n * d:(n + 1) * d])
                                  + bg_ref[n:n + 1, :])
            term = gate * _dot(o_branch_ref[_sub_rows(s), :], w_branch_ref[...])
            st["merged"] = term if n == 0 else st["merged"] + term
        return run

    def project(s, st):
        st["y"] = _dot(st["merged"].astype(_BF16), wo_ref[...])

    def norm_out(s, st):
        rows = _sub_rows(s)
        o_ref[rows, :] = x_ref[rows, :] + _rms(st["y"], g_ref[3:4, :])

    _staggered(ROW_TILE // SUB_ROWS,
               [attend, norm_in, branch(0, oa_ref, wa_ref), branch(1, ob_ref, wb_ref),
                branch(2, oc_ref, wc_ref), project, norm_out])


def _post(x2d, za, qc, kvc, bias_a, ob, gains, w_gate, b_gate, w_a, w_b, w_c, w_o,
          layer, batch, seq):
    t, d = x2d.shape
    tiles = seq // ROW_TILE
    mem_len = kvc.shape[0] // batch
    row = lambda b, i: (b * tiles + i, 0)
    bias_shape = (None,) + bias_a.shape[1:]
    return pl.pallas_call(
        _post_kernel,
        grid=(batch, tiles),
        in_specs=[
            pl.BlockSpec((ROW_TILE, d), row),
            pl.BlockSpec((ROW_TILE, A_COLS), row),
            pl.BlockSpec((BLOCK, 2 * A_KV), lambda b, i: (
                jnp.maximum((b * tiles + i) * (ROW_TILE // BLOCK) - 1, 0), 0)),
            pl.BlockSpec((ROW_TILE, C_W), row),
            pl.BlockSpec((mem_len, 2 * C_W), lambda b, i: (b, 0)),
            pl.BlockSpec(bias_shape, lambda b, i: (jnp.minimum(i, 1), 0, 0, 0)),
            pl.BlockSpec(bias_shape, lambda b, i: (1, 0, 0, 0)),
            pl.BlockSpec((ROW_TILE, B_W), row),
            _layer_resident(gains.shape, layer),
            _layer_resident(w_gate.shape, layer),
            _layer_resident(b_gate.shape, layer),
            _layer_resident(w_a.shape, layer),
            _layer_resident(w_b.shape, layer),
            _layer_resident(w_c.shape, layer),
            _layer_resident(w_o.shape, layer),
        ],
        out_specs=pl.BlockSpec((ROW_TILE, d), row),
        out_shape=jax.ShapeDtypeStruct((t, d), _F32),
        scratch_shapes=[pltpu.VMEM((ROW_TILE, A_Q), _BF16),
                        pltpu.VMEM((ROW_TILE, C_W), _BF16)],
        compiler_params=_params(2),
        name="attn_window_cross_out_proj",
    )(x2d, za, za, qc, kvc, bias_a, bias_a, ob, gains, w_gate, b_gate, w_a, w_b, w_c, w_o)


def _t5_bucket(dist):
    max_exact = N_BUCKETS // 2
    d = jnp.maximum(dist, 1).astype(_F32)
    large = max_exact + (jnp.log(d / max_exact) / math.log(MAX_DISTANCE / max_exact)
                         * (N_BUCKETS - max_exact)).astype(jnp.int32)
    large = jnp.minimum(large, N_BUCKETS - 1)
    return jnp.where(dist < max_exact, dist, large)


def _band_bias(table, head0, n_heads, rate, max_dist):
    row = jnp.arange(BLOCK)[:, None]
    col = jnp.arange(2 * BLOCK)[None, :]
    dist = row + BLOCK - col
    bucket = _t5_bucket(jnp.maximum(dist, 0) * rate)
    hit = bucket[None, None] == jnp.arange(N_BUCKETS)[None, :, None, None]
    vals = table.T[head0:head0 + n_heads, :, None, None]
    bias = jnp.sum(jnp.where(hit, vals, 0.0), axis=1) * LOG2_E
    valid = (dist >= 0) & (dist <= max_dist)
    later = jnp.where(valid[None], bias, MASKED)
    first = jnp.where((valid & (col >= BLOCK))[None], bias, MASKED)
    return jnp.stack([first, later])


def _permute_in_proj(w):
    hd = HEAD_DIM
    qa, ka, va = w[..., 0:A_Q], w[..., A_Q:A_Q + A_KV], w[..., A_Q + A_KV:A_Q + 2 * A_KV]
    off = A_Q + 2 * A_KV
    qb, kb, vb = (w[..., off + n * B_W:off + (n + 1) * B_W] for n in range(3))
    qc = w[..., off + 3 * B_W:]
    grp = lambda w3, g: w3[..., g * B_G:(g + 1) * B_G]
    cols = [ka, va] + [qa[..., h * hd:(h + 1) * hd] for h in A_PAIR_ORDER] + [qc]
    for g in range(DIL_GROUPS):
        cols += [grp(kb, g), grp(vb, g), grp(qb, g)]
    return jnp.concatenate(cols, axis=-1)


def kernel(x, mem, rel_bias, norm_gain, mem_norm_gain, w_ffn1_in, w_ffn1_out, w_in,
           sinks, w_mem_kv, w_gate, b_gate, w_br_a, w_br_b, w_br_c, w_o,
           w_ffn2_in, w_ffn2_out):
    batch, seq, d = x.shape
    depth = norm_gain.shape[0]
    assert seq % DIL_TILE == 0 and seq % ROW_TILE == 0 and (batch * seq) % PROJ_TILE == 0

    table = rel_bias.astype(_F32)
    bias_a = _band_bias(table, 0, SWA_HEADS, 1, SWA_WINDOW - 1)
    bias_b = jnp.stack([
        _band_bias(table, SWA_HEADS + g * DIL_HEADS_PER_GROUP, DIL_HEADS_PER_GROUP,
                   DIL_RATES[g], DIL_WINDOWS[g] // DIL_RATES[g])
        for g in range(DIL_GROUPS)], axis=1)

    bf = lambda w: w.astype(_BF16)
    w1_in, w1_out, w2_in, w2_out = bf(w_ffn1_in), bf(w_ffn1_out), bf(w_ffn2_in), bf(w_ffn2_out)
    w_in_p = _permute_in_proj(bf(w_in))
    w_kv, w_g, w_b, w_c, w_out = bf(w_mem_kv), bf(w_gate), bf(w_br_b), bf(w_br_c), bf(w_o)
    w_a = bf(jnp.concatenate(
        [w_br_a[:, h * HEAD_DIM:(h + 1) * HEAD_DIM] for h in A_PAIR_ORDER], axis=1))

    group0 = A_COLS + C_W
    q_cols = [(2 * A_KV, group0)]
    q_cols += [(group0 + g * B_COLS + 2 * B_G, group0 + (g + 1) * B_COLS)
               for g in range(DIL_GROUPS)]
    q_scale = jnp.ones((1, w_in.shape[-1]), _F32)
    for c0, c1 in q_cols:
        q_scale = q_scale.at[:, c0:c1].set(Q_LOG2_SCALE)

    x2d = x.reshape(batch * seq, d)
    mem2d = mem.reshape(-1, d)
    for l in range(depth):
        x2d = _ffn(x2d, norm_gain, w1_in, w1_out, l, 0)
        za, qc, *zb = _pre(x2d, norm_gain, w_in_p, q_scale, l)
        kvc = _mem_kv(mem2d, mem_norm_gain[:, None, :], w_kv, l)
        bias_a_l = jnp.where(jnp.arange(2 * BLOCK) == 0,
                             (sinks[l].astype(_F32) * LOG2_E)[None, :, None, None], bias_a)
        ob = _attn_dil(zb, bias_b, batch, seq)
        x2d = _post(x2d, za, qc, kvc, bias_a_l, ob, norm_gain, w_g, b_gate,
                    w_a, w_b, w_c, w_out, l, batch, seq)
        x2d = _ffn(x2d, norm_gain, w2_in, w2_out, l, 4)
    return x2d.reshape(batch, seq, d)
```

```python
import functools
import math

import jax
import jax.numpy as jnp
from jax import lax
from jax.experimental import pallas as pl
from jax.experimental.pallas import tpu as pltpu

HEAD_DIM = 64
SWA_HEADS = 6
SWA_KV_HEADS = 2
SWA_GROUP = SWA_HEADS // SWA_KV_HEADS
SWA_WINDOW = 128
DIL_WINDOWS = (128, 512, 2048)
DIL_RATES = (1, 4, 16)
DIL_GROUPS = 3
DIL_HEADS_PER_GROUP = 2
MEM_HEADS = 4
N_BUCKETS = 32
MAX_DISTANCE = 2048
N_BRANCH = 3
EPS = 1e-6
BLOCK = 128
LANES = 128

A_Q = SWA_HEADS * HEAD_DIM
A_KV = SWA_KV_HEADS * HEAD_DIM
B_G = DIL_HEADS_PER_GROUP * HEAD_DIM
B_W = DIL_GROUPS * B_G
C_W = MEM_HEADS * HEAD_DIM
A_COLS = A_Q + 2 * A_KV
B_COLS = 3 * B_G
A_PAIR_ORDER = tuple(h for p in range(SWA_GROUP) for h in (p, p + SWA_GROUP))

LOG2_E = math.log2(math.e)
Q_LOG2_SCALE = HEAD_DIM ** -0.5 * LOG2_E
MASKED = -1e30

ROW_TILE = 1024
PROJ_TILE = 2048
SUB_ROWS = 256
FFN_CHUNK = 512
DIL_TILE = BLOCK * max(DIL_RATES)
VMEM_LIMIT = 56 * 1024 * 1024

_F32 = jnp.float32
_BF16 = jnp.bfloat16


def _dot(a, b):
    return jnp.dot(a, b, preferred_element_type=_F32)


def _dot_nt(a, b):
    return lax.dot_general(a, b, (((1,), (1,)), ((), ())),
                           preferred_element_type=_F32)


def _rms(x, gain):
    ms = jnp.mean(x * x, axis=-1, keepdims=True)
    return x * lax.rsqrt(ms + EPS) * gain


def _resident(shape):
    return pl.BlockSpec(shape, lambda *_: (0,) * len(shape),
                        pipeline_mode=pl.Buffered(1))


def _layer_resident(shape, layer):
    zeros = (0,) * (len(shape) - 1)
    return pl.BlockSpec((None,) + tuple(shape[1:]), lambda *_: (layer,) + zeros,
                        pipeline_mode=pl.Buffered(1))


def _params(n_axes):
    return pltpu.CompilerParams(
        dimension_semantics=("arbitrary",) * n_axes,
        vmem_limit_bytes=VMEM_LIMIT)


def _staggered(n_sub, stages):
    state = [{} for _ in range(n_sub)]
    for t in range(n_sub + len(stages) - 1):
        for s in range(n_sub):
            if 0 <= t - s < len(stages):
                stages[t - s](s, state[s])


def _sub_rows(s):
    return slice(s * SUB_ROWS, (s + 1) * SUB_ROWS)


def _ffn_kernel(x_ref, g_ref, w_in_ref, w_out_ref, o_ref, *, ffn_dim, g0):
    def norm_in(s, st):
        st["h"] = _rms(x_ref[_sub_rows(s), :], g_ref[g0:g0 + 1, :]).astype(_BF16)

    bounds = [(c0, min(c0 + FFN_CHUNK, ffn_dim)) for c0 in range(0, ffn_dim, FFN_CHUNK)]

    def step(c):
        def run(s, st):
            prev = st.pop("ab", None)
            if c < len(bounds):
                c0, c1 = bounds[c]
                st["ab"] = (_dot(st["h"], w_in_ref[:, c0:c1]),
                            _dot(st["h"], w_in_ref[:, ffn_dim + c0:ffn_dim + c1]))
            if prev is not None:
                c0, c1 = bounds[c - 1]
                a, b = prev
                act = (a * jax.nn.sigmoid(a) * b).astype(_BF16)
                part = _dot(act, w_out_ref[c0:c1, :])
                st["y"] = part if "y" not in st else st["y"] + part
        return run

    def norm_out(s, st):
        rows = _sub_rows(s)
        o_ref[rows, :] = x_ref[rows, :] + 0.5 * _rms(st["y"], g_ref[g0 + 1:g0 + 2, :])

    _staggered(ROW_TILE // SUB_ROWS,
               [norm_in] + [step(c) for c in range(len(bounds) + 1)] + [norm_out])


def _ffn(x2d, gains, w_in, w_out, layer, g0):
    t, d = x2d.shape
    ffn_dim = w_out.shape[1]
    return pl.pallas_call(
        functools.partial(_ffn_kernel, ffn_dim=ffn_dim, g0=g0),
        grid=(t // ROW_TILE,),
        in_specs=[
            pl.BlockSpec((ROW_TILE, d), lambda i: (i, 0)),
            _layer_resident(gains.shape, layer),
            _layer_resident(w_in.shape, layer),
            _layer_resident(w_out.shape, layer),
        ],
        out_specs=pl.BlockSpec((ROW_TILE, d), lambda i: (i, 0)),
        out_shape=jax.ShapeDtypeStruct((t, d), _F32),
        compiler_params=_params(1),
        name="ffn",
    )(x2d, gains, w_in, w_out)


def _pre_kernel(x_ref, g_ref, w_ref, qs_ref, a_ref, c_ref, b0_ref, b1_ref, b2_ref, z_scr):
    def norm_in(s, st):
        st["h"] = _rms(x_ref[_sub_rows(s), :], g_ref[2:3, :]).astype(_BF16)

    first = (a_ref, c_ref, b0_ref)
    first_cols = sum(ref.shape[-1] for ref in first)

    def project_rows(s, st):
        z = _dot(st["h"], w_ref[:, 0:first_cols]) * qs_ref[:, 0:first_cols]
        col = 0
        for ref in first:
            ref[_sub_rows(s), :] = z[:, col:col + ref.shape[-1]].astype(_BF16)
            col += ref.shape[-1]

    def project_residues(s, st):
        z = _dot(st["h"], w_ref[:, first_cols:]) * qs_ref[:, first_cols:]
        n_slabs = z.shape[-1] // LANES
        for n in range(n_slabs):
            z_scr[s, n] = z[:, n * LANES:(n + 1) * LANES]
        for g, ref in ((1, b1_ref), (2, b2_ref)):
            rate = DIL_RATES[g]
            per = SUB_ROWS // rate
            for c in range(rate):
                for n in range(B_COLS // LANES):
                    slab = (g - 1) * (B_COLS // LANES) + n
                    ref[c, s * per:(s + 1) * per, n * LANES:(n + 1) * LANES] = (
                        z_scr[s, slab, pl.ds(c, per, stride=rate), :].astype(_BF16))

    _staggered(PROJ_TILE // SUB_ROWS, [norm_in, project_residues, project_rows])


def _pre(x2d, gains, w_perm, q_scale, layer):
    t, d = x2d.shape
    row = lambda i: (i, 0)
    return pl.pallas_call(
        _pre_kernel,
        grid=(t // PROJ_TILE,),
        in_specs=[
            pl.BlockSpec((PROJ_TILE, d), row),
            _layer_resident(gains.shape, layer),
            _layer_resident(w_perm.shape, layer),
            _resident(q_scale.shape),
        ],
        out_specs=[
            pl.BlockSpec((PROJ_TILE, A_COLS), row),
            pl.BlockSpec((PROJ_TILE, C_W), row),
            pl.BlockSpec((PROJ_TILE, B_COLS), row),
        ] + [pl.BlockSpec((r, PROJ_TILE // r, B_COLS), lambda i: (0, i, 0))
             for r in DIL_RATES[1:]],
        out_shape=[
            jax.ShapeDtypeStruct((t, A_COLS), _BF16),
            jax.ShapeDtypeStruct((t, C_W), _BF16),
            jax.ShapeDtypeStruct((t, B_COLS), _BF16),
        ] + [jax.ShapeDtypeStruct((r, t // r, B_COLS), _BF16) for r in DIL_RATES[1:]],
        scratch_shapes=[
            pltpu.VMEM((PROJ_TILE // SUB_ROWS, (DIL_GROUPS - 1) * B_COLS // LANES,
                        SUB_ROWS, LANES), _F32),
        ],
        compiler_params=_params(1),
        name="mix_in_proj",
    )(x2d, gains, w_perm, q_scale)


def _mem_kernel(m_ref, g_ref, w_ref, o_ref):
    h = _rms(m_ref[...], g_ref[...]).astype(_BF16)
    o_ref[...] = _dot(h, w_ref[...]).astype(_BF16)


def _mem_kv(mem2d, gains, w, layer):
    rows = mem2d.shape[0]
    return pl.pallas_call(
        _mem_kernel,
        grid=(1,),
        in_specs=[_resident(mem2d.shape),
                  _layer_resident(gains.shape, layer),
                  _layer_resident(w.shape, layer)],
        out_specs=pl.BlockSpec((rows, 2 * C_W), lambda i: (0, 0)),
        out_shape=jax.ShapeDtypeStruct((rows, 2 * C_W), _BF16),
        compiler_params=_params(1),
        name="mem_kv",
    )(mem2d, gains, w)


def _low_lanes():
    return lax.broadcasted_iota(jnp.int32, (1, LANES), 1) < HEAD_DIM


def _pair_scores(q, k, bias_lo=None, bias_hi=None):
    low = _low_lanes()
    zero = jnp.zeros((), k.dtype)
    if q.shape[0] <= k.shape[0]:
        s_lo = _dot_nt(jnp.where(low, q, zero), k)
        s_hi = _dot_nt(jnp.where(low, zero, q), k)
    else:
        s_lo = _dot_nt(q, jnp.where(low, k, zero))
        s_hi = _dot_nt(q, jnp.where(low, zero, k))
    if bias_lo is not None:
        s_lo = s_lo + bias_lo
        s_hi = s_hi + bias_hi
    return s_lo, s_hi


def _pair_softmax_pv(scores, v):
    s_lo, s_hi = scores
    low = _low_lanes()
    zero = jnp.zeros((), v.dtype)
    lane = lax.broadcasted_iota(jnp.int32, v.shape, 1)
    ones_lo = jnp.where(lane < HEAD_DIM, 1.0, 0.0).astype(v.dtype)
    ones_hi = jnp.where(lane < HEAD_DIM, 0.0, 1.0).astype(v.dtype)
    m_lo = jnp.max(s_lo, axis=-1, keepdims=True)
    m_hi = jnp.max(s_hi, axis=-1, keepdims=True)
    p_lo = jnp.exp2(s_lo - m_lo).astype(_BF16)
    p_hi = jnp.exp2(s_hi - m_hi).astype(_BF16)
    rhs = jnp.concatenate([
        jnp.concatenate([jnp.where(low, v, zero), ones_lo], axis=1),
        jnp.concatenate([jnp.where(low, zero, v), ones_hi], axis=1)],
        axis=0)
    res = _dot(jnp.concatenate([p_lo, p_hi], axis=1), rhs)
    return res[:, 0:LANES], res[:, LANES:2 * LANES], m_lo, m_hi


WINDOW_LOOKAHEAD = 3
DILATED_LOOKAHEAD = 2


def _software_pipeline(tiles, lookahead):
    pending = []
    for n, (scores_fn, _) in enumerate(tiles):
        pending.append(scores_fn())
        if n >= lookahead:
            done = n - lookahead
            tiles[done][1](pending[done])
            pending[done] = None
    for done in range(max(len(tiles) - lookahead, 0), len(tiles)):
        tiles[done][1](pending[done])


def _window_cross_tiles(cur_ref, prev_ref, qc_ref, kvc_ref, bias0_ref, bias_ref,
                        oa_ref, oc_ref):
    rows_a = SWA_GROUP * BLOCK
    sink_row = lax.broadcasted_iota(jnp.int32, (BLOCK, A_KV), 0) == 0

    def window_tile(j):
        rows = slice(j * BLOCK, (j + 1) * BLOCK)
        prev_rows = slice((j - 1) * BLOCK, j * BLOCK)
        bias = bias0_ref if j == 0 else bias_ref

        def kv_window(c0, c1):
            prev = prev_ref[:, c0:c1] if j == 0 else cur_ref[prev_rows, c0:c1]
            prev = jnp.where(sink_row, jnp.zeros((), prev.dtype), prev)
            return jnp.concatenate([prev, cur_ref[rows, c0:c1]], axis=0)

        def scores():
            q = jnp.concatenate(
                [cur_ref[rows, 2 * A_KV + p * LANES:2 * A_KV + (p + 1) * LANES]
                 for p in range(SWA_GROUP)], axis=0)
            return _pair_scores(q, kv_window(0, A_KV),
                                bias[0:SWA_GROUP].reshape(rows_a, 2 * BLOCK),
                                bias[SWA_GROUP:SWA_HEADS].reshape(rows_a, 2 * BLOCK))

        def finish(s):
            out, den, _, _ = _pair_softmax_pv(s, kv_window(A_KV, 2 * A_KV))
            o = out / den
            for p in range(SWA_GROUP):
                oa_ref[rows, p * LANES:(p + 1) * LANES] = (
                    o[p * BLOCK:(p + 1) * BLOCK]).astype(_BF16)

        return scores, finish

    def cross_tile(p, rows):
        cols = slice(p * LANES, (p + 1) * LANES)

        def scores():
            return _pair_scores(qc_ref[rows, cols], kvc_ref[:, cols])

        def finish(s):
            out, den, _, _ = _pair_softmax_pv(
                s, kvc_ref[:, C_W + p * LANES:C_W + (p + 1) * LANES])
            oc_ref[rows, cols] = (out / den).astype(_BF16)

        return scores, finish

    return window_tile, cross_tile


def _dil_kernel(*refs):
    cur_refs, prev_refs = refs[0:2 * DIL_GROUPS:2], refs[1:2 * DIL_GROUPS:2]
    bias0_ref, bias_ref, ob_ref, o_scr, d_scr, m_scr = refs[2 * DIL_GROUPS:]
    low = _low_lanes()
    k_cols, v_cols, q_cols = (slice(n * B_G, (n + 1) * B_G) for n in range(3))

    def tile(g, sb, c):
        rate = DIL_RATES[g]
        rows = slice(sb * BLOCK, (sb + 1) * BLOCK)
        bias = bias0_ref if sb == 0 else bias_ref
        if rate == 1:
            out_rows = pl.ds(sb * BLOCK, BLOCK)
            cur = lambda r, cols: cur_refs[g][r, cols]
            prev = lambda cols: prev_refs[g][:, cols]
        else:
            out_rows = pl.ds(sb * BLOCK * rate + c, BLOCK, stride=rate)
            cur = lambda r, cols: cur_refs[g][c, r, cols]
            prev = lambda cols: prev_refs[g][c, :, cols]

        def window(cols):
            before = prev(cols) if sb == 0 else cur(slice((sb - 1) * BLOCK, sb * BLOCK), cols)
            return jnp.concatenate([before, cur(rows, cols)], axis=0)

        def scores():
            return _pair_scores(cur(rows, q_cols), window(k_cols), bias[g, 0], bias[g, 1])

        def finish(s):
            out, den, m_lo, m_hi = _pair_softmax_pv(s, window(v_cols))
            o_scr[g, out_rows, :] = out
            d_scr[g, out_rows, :] = den
            m_scr[g, out_rows, :] = jnp.where(low, m_lo, m_hi)

        return scores, finish

    per_group = [[tile(g, sb, c)
                  for sb in range(DIL_TILE // (BLOCK * rate)) for c in range(rate)]
                 for g, rate in enumerate(DIL_RATES)]
    _software_pipeline([t for trio in zip(*per_group) for t in trio], DILATED_LOOKAHEAD)

    top = functools.reduce(jnp.maximum, [m_scr[g] for g in range(DIL_GROUPS)])
    scale = [jnp.exp2(m_scr[g] - top) for g in range(DIL_GROUPS)]
    total = sum(d_scr[g] * scale[g] for g in range(DIL_GROUPS))
    for g in range(DIL_GROUPS):
        ob_ref[:, g * B_G:(g + 1) * B_G] = (o_scr[g] * (scale[g] / total)).astype(_BF16)


def _attn_dil(zb, bias_b, batch, seq):
    t = zb[0].shape[0]
    tiles = seq // DIL_TILE
    row = lambda b, i: (b * tiles + i, 0)
    bias_shape = (None,) + bias_b.shape[1:]
    operands, specs = [], []
    for z, rate in zip(zb, DIL_RATES):
        rows = DIL_TILE // rate
        prev_block = lambda b, i, n=rows // BLOCK: jnp.maximum((b * tiles + i) * n - 1, 0)
        operands += [z, z]
        if rate == 1:
            specs += [pl.BlockSpec((rows, B_COLS), row),
                      pl.BlockSpec((BLOCK, B_COLS),
                                   lambda b, i, f=prev_block: (f(b, i), 0))]
        else:
            specs += [pl.BlockSpec((rate, rows, B_COLS), lambda b, i: (0, b * tiles + i, 0)),
                      pl.BlockSpec((rate, BLOCK, B_COLS),
                                   lambda b, i, f=prev_block: (0, f(b, i), 0))]
    return pl.pallas_call(
        _dil_kernel,
        grid=(batch, tiles),
        in_specs=specs + [
            pl.BlockSpec(bias_shape, lambda b, i: (jnp.minimum(i, 1), 0, 0, 0, 0)),
            pl.BlockSpec(bias_shape, lambda b, i: (1, 0, 0, 0, 0)),
        ],
        out_specs=pl.BlockSpec((DIL_TILE, B_W), row),
        out_shape=jax.ShapeDtypeStruct((t, B_W), _BF16),
        scratch_shapes=[
            pltpu.VMEM((DIL_GROUPS, DIL_TILE, B_G), _F32),
            pltpu.VMEM((DIL_GROUPS, DIL_TILE, B_G), _F32),
            pltpu.VMEM((DIL_GROUPS, DIL_TILE, B_G), _F32),
        ],
        compiler_params=_params(2),
        name="attn_dilated",
    )(*operands, bias_b, bias_b)


def _post_kernel(x_ref, za_ref, za_prev_ref, qc_ref, kvc_ref, bias0_ref, bias_ref, ob_ref,
                 g_ref, wg_ref, bg_ref, wa_ref, wb_ref, wc_ref, wo_ref, o_ref,
                 oa_ref, oc_ref):
    d = x_ref.shape[-1]
    window_tile, cross_tile = _window_cross_tiles(
        za_ref, za_prev_ref, qc_ref, kvc_ref, bias0_ref, bias_ref, oa_ref, oc_ref)
    blocks_per_sub = SUB_ROWS // BLOCK

    def attend(s, st):
        _software_pipeline(
            [window_tile(j) for j in range(s * blocks_per_sub, (s + 1) * blocks_per_sub)]
            + [cross_tile(p, _sub_rows(s)) for p in range(MEM_HEADS // 2)],
            WINDOW_LOOKAHEAD)

    def norm_in(s, st):
        st["h"] = _rms(x_ref[_sub_rows(s), :], g_ref[2:3, :]).astype(_BF16)

    def branch(n, o_branch_ref, w_branch_ref):
        def run(s, st):
            gate = jax.nn.sigmoid(_dot(st["h"], wg_ref[:, n * d:(n + 1) * d])
                                  + bg_ref[n:n + 1, :])
            term = gate * _dot(o_branch_ref[_sub_rows(s), :], w_branch_ref[...])
            st["merged"] = term if n == 0 else st["merged"] + term
        return run

    def project(s, st):
        st["y"] = _dot(st["merged"].astype(_BF16), wo_ref[...])

    def norm_out(s, st):
        rows = _sub_rows(s)
        o_ref[rows, :] = x_ref[rows, :] + _rms(st["y"], g_ref[3:4, :])

    _staggered(ROW_TILE // SUB_ROWS,
               [attend, norm_in, branch(0, oa_ref, wa_ref), branch(1, ob_ref, wb_ref),
                branch(2, oc_ref, wc_ref), project, norm_out])


def _post(x2d, za, qc, kvc, bias_a, ob, gains, w_gate, b_gate, w_a, w_b, w_c, w_o,
          layer, batch, seq):
    t, d = x2d.shape
    tiles = seq // ROW_TILE
    mem_len = kvc.shape[0] // batch
    row = lambda b, i: (b * tiles + i, 0)
    bias_shape = (None,) + bias_a.shape[1:]
    return pl.pallas_call(
        _post_kernel,
        grid=(batch, tiles),
        in_specs=[
            pl.BlockSpec((ROW_TILE, d), row),
            pl.BlockSpec((ROW_TILE, A_COLS), row),
            pl.BlockSpec((BLOCK, 2 * A_KV), lambda b, i: (
                jnp.maximum((b * tiles + i) * (ROW_TILE // BLOCK) - 1, 0), 0)),
            pl.BlockSpec((ROW_TILE, C_W), row),
            pl.BlockSpec((mem_len, 2 * C_W), lambda b, i: (b, 0)),
            pl.BlockSpec(bias_shape, lambda b, i: (jnp.minimum(i, 1), 0, 0, 0)),
            pl.BlockSpec(bias_shape, lambda b, i: (1, 0, 0, 0)),
            pl.BlockSpec((ROW_TILE, B_W), row),
            _layer_resident(gains.shape, layer),
            _layer_resident(w_gate.shape, layer),
            _layer_resident(b_gate.shape, layer),
            _layer_resident(w_a.shape, layer),
            _layer_resident(w_b.shape, layer),
            _layer_resident(w_c.shape, layer),
            _layer_resident(w_o.shape, layer),
        ],
        out_specs=pl.BlockSpec((ROW_TILE, d), row),
        out_shape=jax.ShapeDtypeStruct((t, d), _F32),
        scratch_shapes=[pltpu.VMEM((ROW_TILE, A_Q), _BF16),
                        pltpu.VMEM((ROW_TILE, C_W), _BF16)],
        compiler_params=_params(2),
        name="attn_window_cross_out_proj",
    )(x2d, za, za, qc, kvc, bias_a, bias_a, ob, gains, w_gate, b_gate, w_a, w_b, w_c, w_o)


def _t5_bucket(dist):
    max_exact = N_BUCKETS // 2
    d = jnp.maximum(dist, 1).astype(_F32)
    large = max_exact + (jnp.log(d / max_exact) / math.log(MAX_DISTANCE / max_exact)
                         * (N_BUCKETS - max_exact)).astype(jnp.int32)
    large = jnp.minimum(large, N_BUCKETS - 1)
    return jnp.where(dist < max_exact, dist, large)


def _band_bias(table, head0, n_heads, rate, max_dist):
    row = jnp.arange(BLOCK)[:, None]
    col = jnp.arange(2 * BLOCK)[None, :]
    dist = row + BLOCK - col
    bucket = _t5_bucket(jnp.maximum(dist, 0) * rate)
    hit = bucket[None, None] == jnp.arange(N_BUCKETS)[None, :, None, None]
    vals = table.T[head0:head0 + n_heads, :, None, None]
    bias = jnp.sum(jnp.where(hit, vals, 0.0), axis=1) * LOG2_E
    valid = (dist >= 0) & (dist <= max_dist)
    later = jnp.where(valid[None], bias, MASKED)
    first = jnp.where((valid & (col >= BLOCK))[None], bias, MASKED)
    return jnp.stack([first, later])


def _permute_in_proj(w):
    hd = HEAD_DIM
    qa, ka, va = w[..., 0:A_Q], w[..., A_Q:A_Q + A_KV], w[..., A_Q + A_KV:A_Q + 2 * A_KV]
    off = A_Q + 2 * A_KV
    qb, kb, vb = (w[..., off + n * B_W:off + (n + 1) * B_W] for n in range(3))
    qc = w[..., off + 3 * B_W:]
    grp = lambda w3, g: w3[..., g * B_G:(g + 1) * B_G]
    cols = [ka, va] + [qa[..., h * hd:(h + 1) * hd] for h in A_PAIR_ORDER] + [qc]
    for g in range(DIL_GROUPS):
        cols += [grp(kb, g), grp(vb, g), grp(qb, g)]
    return jnp.concatenate(cols, axis=-1)


def kernel(x, mem, rel_bias, norm_gain, mem_norm_gain, w_ffn1_in, w_ffn1_out, w_in,
           sinks, w_mem_kv, w_gate, b_gate, w_br_a, w_br_b, w_br_c, w_o,
           w_ffn2_in, w_ffn2_out):
    batch, seq, d = x.shape
    depth = norm_gain.shape[0]
    assert seq % DIL_TILE == 0 and seq % ROW_TILE == 0 and (batch * seq) % PROJ_TILE == 0

    table = rel_bias.astype(_F32)
    bias_a = _band_bias(table, 0, SWA_HEADS, 1, SWA_WINDOW - 1)
    bias_b = jnp.stack([
        _band_bias(table, SWA_HEADS + g * DIL_HEADS_PER_GROUP, DIL_HEADS_PER_GROUP,
                   DIL_RATES[g], DIL_WINDOWS[g] // DIL_RATES[g])
        for g in range(DIL_GROUPS)], axis=1)

    bf = lambda w: w.astype(_BF16)
    w1_in, w1_out, w2_in, w2_out = bf(w_ffn1_in), bf(w_ffn1_out), bf(w_ffn2_in), bf(w_ffn2_out)
    w_in_p = _permute_in_proj(bf(w_in))
    w_kv, w_g, w_b, w_c, w_out = bf(w_mem_kv), bf(w_gate), bf(w_br_b), bf(w_br_c), bf(w_o)
    w_a = bf(jnp.concatenate(
        [w_br_a[:, h * HEAD_DIM:(h + 1) * HEAD_DIM] for h in A_PAIR_ORDER], axis=1))

    group0 = A_COLS + C_W
    q_cols = [(2 * A_KV, group0)]
    q_cols += [(group0 + g * B_COLS + 2 * B_G, group0 + (g + 1) * B_COLS)
               for g in range(DIL_GROUPS)]
    q_scale = jnp.ones((1, w_in.shape[-1]), _F32)
    for c0, c1 in q_cols:
        q_scale = q_scale.at[:, c0:c1].set(Q_LOG2_SCALE)

    x2d = x.reshape(batch * seq, d)
    mem2d = mem.reshape(-1, d)
    for l in range(depth):
        x2d = _ffn(x2d, norm_gain, w1_in, w1_out, l, 0)
        za, qc, *zb = _pre(x2d, norm_gain, w_in_p, q_scale, l)
        kvc = _mem_kv(mem2d, mem_norm_gain[:, None, :], w_kv, l)
        bias_a_l = jnp.where(jnp.arange(2 * BLOCK) == 0,
                             (sinks[l].astype(_F32) * LOG2_E)[None, :, None, None], bias_a)
        ob = _attn_dil(zb, bias_b, batch, seq)
        x2d = _post(x2d, za, qc, kvc, bias_a_l, ob, norm_gain, w_g, b_gate,
                    w_a, w_b, w_c, w_out, l, batch, seq)
        x2d = _ffn(x2d, norm_gain, w2_in, w2_out, l, 4)
    return x2d.reshape(batch, seq, d)
```

```python
import functools
import math

import jax
import jax.numpy as jnp
from jax import lax
from jax.experimental import pallas as pl
from jax.experimental.pallas import tpu as pltpu

HEAD_DIM = 64
SWA_HEADS = 6
SWA_KV_HEADS = 2
SWA_GROUP = SWA_HEADS // SWA_KV_HEADS
SWA_WINDOW = 128
DIL_WINDOWS = (128, 512, 2048)
DIL_RATES = (1, 4, 16)
DIL_GROUPS = 3
DIL_HEADS_PER_GROUP = 2
MEM_HEADS = 4
N_BUCKETS = 32
MAX_DISTANCE = 2048
N_BRANCH = 3
EPS = 1e-6
BLOCK = 128
LANES = 128

A_Q = SWA_HEADS * HEAD_DIM
A_KV = SWA_KV_HEADS * HEAD_DIM
B_G = DIL_HEADS_PER_GROUP * HEAD_DIM
B_W = DIL_GROUPS * B_G
C_W = MEM_HEADS * HEAD_DIM
A_COLS = A_Q + 2 * A_KV
B_COLS = 3 * B_G
A_PAIR_ORDER = tuple(h for p in range(SWA_GROUP) for h in (p, p + SWA_GROUP))

LOG2_E = math.log2(math.e)
Q_LOG2_SCALE = HEAD_DIM ** -0.5 * LOG2_E
MASKED = -1e30

ROW_TILE = 1024
PROJ_TILE = 2048
SUB_ROWS = 256
FFN_CHUNK = 512
DIL_TILE = BLOCK * max(DIL_RATES)
VMEM_LIMIT = 56 * 1024 * 1024

_F32 = jnp.float32
_BF16 = jnp.bfloat16


def _dot(a, b):
    return jnp.dot(a, b, preferred_element_type=_F32)


def _dot_nt(a, b):
    return lax.dot_general(a, b, (((1,), (1,)), ((), ())),
                           preferred_element_type=_F32)


def _rms(x, gain):
    ms = jnp.mean(x * x, axis=-1, keepdims=True)
    return x * lax.rsqrt(ms + EPS) * gain


def _resident(shape):
    return pl.BlockSpec(shape, lambda *_: (0,) * len(shape),
                        pipeline_mode=pl.Buffered(1))


def _layer_resident(shape, layer):
    zeros = (0,) * (len(shape) - 1)
    return pl.BlockSpec((None,) + tuple(shape[1:]), lambda *_: (layer,) + zeros,
                        pipeline_mode=pl.Buffered(1))


def _params(n_axes):
    return pltpu.CompilerParams(
        dimension_semantics=("arbitrary",) * n_axes,
        vmem_limit_bytes=VMEM_LIMIT)


def _staggered(n_sub, stages):
    state = [{} for _ in range(n_sub)]
    for t in range(n_sub + len(stages) - 1):
        for s in range(n_sub):
            if 0 <= t - s < len(stages):
                stages[t - s](s, state[s])


def _sub_rows(s):
    return slice(s * SUB_ROWS, (s + 1) * SUB_ROWS)


def _ffn_kernel(x_ref, g_ref, w_in_ref, w_out_ref, o_ref, *, ffn_dim, g0):
    def norm_in(s, st):
        st["h"] = _rms(x_ref[_sub_rows(s), :], g_ref[g0:g0 + 1, :]).astype(_BF16)

    bounds = [(c0, min(c0 + FFN_CHUNK, ffn_dim)) for c0 in range(0, ffn_dim, FFN_CHUNK)]

    def step(c):
        def run(s, st):
            prev = st.pop("ab", None)
            if c < len(bounds):
                c0, c1 = bounds[c]
                st["ab"] = (_dot(st["h"], w_in_ref[:, c0:c1]),
                            _dot(st["h"], w_in_ref[:, ffn_dim + c0:ffn_dim + c1]))
            if prev is not None:
                c0, c1 = bounds[c - 1]
                a, b = prev
                act = (a * jax.nn.sigmoid(a) * b).astype(_BF16)
                part = _dot(act, w_out_ref[c0:c1, :])
                st["y"] = part if "y" not in st else st["y"] + part
        return run

    def norm_out(s, st):
        rows = _sub_rows(s)
        o_ref[rows, :] = x_ref[rows, :] + 0.5 * _rms(st["y"], g_ref[g0 + 1:g0 + 2, :])

    _staggered(ROW_TILE // SUB_ROWS,
               [norm_in] + [step(c) for c in range(len(bounds) + 1)] + [norm_out])


def _ffn(x2d, gains, w_in, w_out, layer, g0):
    t, d = x2d.shape
    ffn_dim = w_out.shape[1]
    return pl.pallas_call(
        functools.partial(_ffn_kernel, ffn_dim=ffn_dim, g0=g0),
        grid=(t // ROW_TILE,),
        in_specs=[
            pl.BlockSpec((ROW_TILE, d), lambda i: (i, 0)),
            _layer_resident(gains.shape, layer),
            _layer_resident(w_in.shape, layer),
            _layer_resident(w_out.shape, layer),
        ],
        out_specs=pl.BlockSpec((ROW_TILE, d), lambda i: (i, 0)),
        out_shape=jax.ShapeDtypeStruct((t, d), _F32),
        compiler_params=_params(1),
        name="ffn",
    )(x2d, gains, w_in, w_out)


def _pre_kernel(x_ref, g_ref, w_ref, qs_ref, a_ref, c_ref, b0_ref, b1_ref, b2_ref,
                z_scr, t_scr):
    def norm_in(s, st):
        st["h"] = _rms(x_ref[_sub_rows(s), :], g_ref[2:3, :]).astype(_BF16)

    first = (a_ref, c_ref, b0_ref)
    first_cols = sum(ref.shape[-1] for ref in first)

    def project_rows(s, st):
        z = _dot(st["h"], w_ref[:, 0:first_cols]) * qs_ref[:, 0:first_cols]
        col = 0
        for ref in first:
            ref[_sub_rows(s), :] = z[:, col:col + ref.shape[-1]].astype(_BF16)
            col += ref.shape[-1]

    def project_residues(s, st):
        z = _dot(st["h"], w_ref[:, first_cols:]) * qs_ref[:, first_cols:]
        n_slabs = z.shape[-1] // LANES
        for n in range(n_slabs):
            z_scr[s, n] = z[:, n * LANES:(n + 1) * LANES]
        nb = B_COLS // LANES
        r4 = DIL_RATES[1]
        assert DIL_RATES[2] == r4 * r4
        per4, per16 = SUB_ROWS // r4, SUB_ROWS // (r4 * r4)
        for c in range(r4):
            for n in range(nb):
                b1_ref[c, s * per4:(s + 1) * per4, n * LANES:(n + 1) * LANES] = (
                    z_scr[s, n, pl.ds(c, per4, stride=r4), :].astype(_BF16))
                t_scr[s, n, c * per4:(c + 1) * per4, :] = (
                    z_scr[s, nb + n, pl.ds(c, per4, stride=r4), :])
        for c4 in range(r4):
            for k in range(r4):
                for n in range(nb):
                    b2_ref[c4 + r4 * k, s * per16:(s + 1) * per16, n * LANES:(n + 1) * LANES] = (
                        t_scr[s, n, pl.ds(c4 * per4 + k, per16, stride=r4), :].astype(_BF16))

    _staggered(PROJ_TILE // SUB_ROWS, [norm_in, project_residues, project_rows])


def _pre(x2d, gains, w_perm, q_scale, layer):
    t, d = x2d.shape
    row = lambda i: (i, 0)
    return pl.pallas_call(
        _pre_kernel,
        grid=(t // PROJ_TILE,),
        in_specs=[
            pl.BlockSpec((PROJ_TILE, d), row),
            _layer_resident(gains.shape, layer),
            _layer_resident(w_perm.shape, layer),
            _resident(q_scale.shape),
        ],
        out_specs=[
            pl.BlockSpec((PROJ_TILE, A_COLS), row),
            pl.BlockSpec((PROJ_TILE, C_W), row),
            pl.BlockSpec((PROJ_TILE, B_COLS), row),
        ] + [pl.BlockSpec((r, PROJ_TILE // r, B_COLS), lambda i: (0, i, 0))
             for r in DIL_RATES[1:]],
        out_shape=[
            jax.ShapeDtypeStruct((t, A_COLS), _BF16),
            jax.ShapeDtypeStruct((t, C_W), _BF16),
            jax.ShapeDtypeStruct((t, B_COLS), _BF16),
        ] + [jax.ShapeDtypeStruct((r, t // r, B_COLS), _BF16) for r in DIL_RATES[1:]],
        scratch_shapes=[
            pltpu.VMEM((PROJ_TILE // SUB_ROWS, (DIL_GROUPS - 1) * B_COLS // LANES,
                        SUB_ROWS, LANES), _F32),
            pltpu.VMEM((PROJ_TILE // SUB_ROWS, B_COLS // LANES, SUB_ROWS, LANES), _F32),
        ],
        compiler_params=_params(1),
        name="mix_in_proj",
    )(x2d, gains, w_perm, q_scale)


def _mem_kernel(m_ref, g_ref, w_ref, o_ref):
    h = _rms(m_ref[...], g_ref[...]).astype(_BF16)
    o_ref[...] = _dot(h, w_ref[...]).astype(_BF16)


def _mem_kv(mem2d, gains, w, layer):
    rows = mem2d.shape[0]
    return pl.pallas_call(
        _mem_kernel,
        grid=(1,),
        in_specs=[_resident(mem2d.shape),
                  _layer_resident(gains.shape, layer),
                  _layer_resident(w.shape, layer)],
        out_specs=pl.BlockSpec((rows, 2 * C_W), lambda i: (0, 0)),
        out_shape=jax.ShapeDtypeStruct((rows, 2 * C_W), _BF16),
        compiler_params=_params(1),
        name="mem_kv",
    )(mem2d, gains, w)


def _low_lanes():
    return lax.broadcasted_iota(jnp.int32, (1, LANES), 1) < HEAD_DIM


def _pair_scores(q, k, bias_lo=None, bias_hi=None):
    low = _low_lanes()
    zero = jnp.zeros((), k.dtype)
    if q.shape[0] <= k.shape[0]:
        s_lo = _dot_nt(jnp.where(low, q, zero), k)
        s_hi = _dot_nt(jnp.where(low, zero, q), k)
    else:
        s_lo = _dot_nt(q, jnp.where(low, k, zero))
        s_hi = _dot_nt(q, jnp.where(low, zero, k))
    if bias_lo is not None:
        s_lo = s_lo + bias_lo
        s_hi = s_hi + bias_hi
    return s_lo, s_hi


def _pair_softmax_pv(scores, v):
    s_lo, s_hi = scores
    low = _low_lanes()
    zero = jnp.zeros((), v.dtype)
    lane = lax.broadcasted_iota(jnp.int32, v.shape, 1)
    ones_lo = jnp.where(lane < HEAD_DIM, 1.0, 0.0).astype(v.dtype)
    ones_hi = jnp.where(lane < HEAD_DIM, 0.0, 1.0).astype(v.dtype)
    m_lo = jnp.max(s_lo, axis=-1, keepdims=True)
    m_hi = jnp.max(s_hi, axis=-1, keepdims=True)
    p_lo = jnp.exp2(s_lo - m_lo).astype(_BF16)
    p_hi = jnp.exp2(s_hi - m_hi).astype(_BF16)
    rhs = jnp.concatenate([
        jnp.concatenate([jnp.where(low, v, zero), ones_lo], axis=1),
        jnp.concatenate([jnp.where(low, zero, v), ones_hi], axis=1)],
        axis=0)
    res = _dot(jnp.concatenate([p_lo, p_hi], axis=1), rhs)
    return res[:, 0:LANES], res[:, LANES:2 * LANES], m_lo, m_hi


WINDOW_LOOKAHEAD = 2
DILATED_LOOKAHEAD = 2


def _software_pipeline(tiles, lookahead):
    pending = []
    for n, (scores_fn, _) in enumerate(tiles):
        pending.append(scores_fn())
        if n >= lookahead:
            done = n - lookahead
            tiles[done][1](pending[done])
            pending[done] = None
    for done in range(max(len(tiles) - lookahead, 0), len(tiles)):
        tiles[done][1](pending[done])


def _window_cross_tiles(cur_ref, prev_ref, qc_ref, kvc_ref, bias0_ref, bias_ref,
                        oa_ref, oc_ref):
    rows_a = SWA_GROUP * BLOCK
    sink_row = lax.broadcasted_iota(jnp.int32, (BLOCK, A_KV), 0) == 0

    def window_tile(j):
        rows = slice(j * BLOCK, (j + 1) * BLOCK)
        prev_rows = slice((j - 1) * BLOCK, j * BLOCK)
        bias = bias0_ref if j == 0 else bias_ref

        def kv_window(c0, c1):
            prev = prev_ref[:, c0:c1] if j == 0 else cur_ref[prev_rows, c0:c1]
            prev = jnp.where(sink_row, jnp.zeros((), prev.dtype), prev)
            return jnp.concatenate([prev, cur_ref[rows, c0:c1]], axis=0)

        def scores():
            q = jnp.concatenate(
                [cur_ref[rows, 2 * A_KV + p * LANES:2 * A_KV + (p + 1) * LANES]
                 for p in range(SWA_GROUP)], axis=0)
            return _pair_scores(q, kv_window(0, A_KV),
                                bias[0:SWA_GROUP].reshape(rows_a, 2 * BLOCK),
                                bias[SWA_GROUP:SWA_HEADS].reshape(rows_a, 2 * BLOCK))

        def finish(s):
            out, den, _, _ = _pair_softmax_pv(s, kv_window(A_KV, 2 * A_KV))
            o = out / den
            for p in range(SWA_GROUP):
                oa_ref[rows, p * LANES:(p + 1) * LANES] = (
                    o[p * BLOCK:(p + 1) * BLOCK]).astype(_BF16)

        return scores, finish

    def cross_tile(p, rows):
        cols = slice(p * LANES, (p + 1) * LANES)

        def scores():
            return _pair_scores(qc_ref[rows, cols], kvc_ref[:, cols])

        def finish(s):
            out, den, _, _ = _pair_softmax_pv(
                s, kvc_ref[:, C_W + p * LANES:C_W + (p + 1) * LANES])
            oc_ref[rows, cols] = (out / den).astype(_BF16)

        return scores, finish

    return window_tile, cross_tile


def _dil_kernel(*refs):
    cur_refs, prev_refs = refs[0:2 * DIL_GROUPS:2], refs[1:2 * DIL_GROUPS:2]
    bias0_ref, bias_ref, ob_ref, o_scr, d_scr, m_scr = refs[2 * DIL_GROUPS:]
    low = _low_lanes()
    k_cols, v_cols, q_cols = (slice(n * B_G, (n + 1) * B_G) for n in range(3))

    def tile(g, sb, c):
        rate = DIL_RATES[g]
        rows = slice(sb * BLOCK, (sb + 1) * BLOCK)
        bias = bias0_ref if sb == 0 else bias_ref
        if rate == 1:
            out_rows = pl.ds(sb * BLOCK, BLOCK)
            cur = lambda r, cols: cur_refs[g][r, cols]
            prev = lambda cols: prev_refs[g][:, cols]
        else:
            out_rows = pl.ds(sb * BLOCK * rate + c, BLOCK, stride=rate)
            cur = lambda r, cols: cur_refs[g][c, r, cols]
            prev = lambda cols: prev_refs[g][c, :, cols]

        def window(cols):
            before = prev(cols) if sb == 0 else cur(slice((sb - 1) * BLOCK, sb * BLOCK), cols)
            return jnp.concatenate([before, cur(rows, cols)], axis=0)

        def scores():
            return _pair_scores(cur(rows, q_cols), window(k_cols), bias[g, 0], bias[g, 1])

        def finish(s):
            out, den, m_lo, m_hi = _pair_softmax_pv(s, window(v_cols))
            o_scr[g, out_rows, :] = out
            d_scr[g, out_rows, :] = den
            m_scr[g, out_rows, :] = jnp.where(low, m_lo, m_hi)

        return scores, finish

    per_group = [[tile(g, sb, c)
                  for sb in range(DIL_TILE // (BLOCK * rate)) for c in range(rate)]
                 for g, rate in enumerate(DIL_RATES)]
    _software_pipeline([t for trio in zip(*per_group) for t in trio], DILATED_LOOKAHEAD)

    top = functools.reduce(jnp.maximum, [m_scr[g] for g in range(DIL_GROUPS)])
    scale = [jnp.exp2(m_scr[g] - top) for g in range(DIL_GROUPS)]
    total = sum(d_scr[g] * scale[g] for g in range(DIL_GROUPS))
    for g in range(DIL_GROUPS):
        ob_ref[:, g * B_G:(g + 1) * B_G] = (o_scr[g] * (scale[g] / total)).astype(_BF16)


def _attn_dil(zb, bias_b, batch, seq):
    t = zb[0].shape[0]
    tiles = seq // DIL_TILE
    row = lambda b, i: (b * tiles + i, 0)
    bias_shape = (None,) + bias_b.shape[1:]
    operands, specs = [], []
    for z, rate in zip(zb, DIL_RATES):
        rows = DIL_TILE // rate
        prev_block = lambda b, i, n=rows // BLOCK: jnp.maximum((b * tiles + i) * n - 1, 0)
        operands += [z, z]
        if rate == 1:
            specs += [pl.BlockSpec((rows, B_COLS), row),
                      pl.BlockSpec((BLOCK, B_COLS),
                                   lambda b, i, f=prev_block: (f(b, i), 0))]
        else:
            specs += [pl.BlockSpec((rate, rows, B_COLS), lambda b, i: (0, b * tiles + i, 0)),
                      pl.BlockSpec((rate, BLOCK, B_COLS),
                                   lambda b, i, f=prev_block: (0, f(b, i), 0))]
    return pl.pallas_call(
        _dil_kernel,
        grid=(batch, tiles),
        in_specs=specs + [
            pl.BlockSpec(bias_shape, lambda b, i: (jnp.minimum(i, 1), 0, 0, 0, 0)),
            pl.BlockSpec(bias_shape, lambda b, i: (1, 0, 0, 0, 0)),
        ],
        out_specs=pl.BlockSpec((DIL_TILE, B_W), row),
        out_shape=jax.ShapeDtypeStruct((t, B_W), _BF16),
        scratch_shapes=[
            pltpu.VMEM((DIL_GROUPS, DIL_TILE, B_G), _F32),
            pltpu.VMEM((DIL_GROUPS, DIL_TILE, B_G), _F32),
            pltpu.VMEM((DIL_GROUPS, DIL_TILE, B_G), _F32),
        ],
        compiler_params=_params(2),
        name="attn_dilated",
    )(*operands, bias_b, bias_b)


def _post_kernel(x_ref, za_ref, za_prev_ref, qc_ref, kvc_ref, bias0_ref, bias_ref, ob_ref,
                 g_ref, wg_ref, bg_ref, wa_ref, wb_ref, wc_ref, wo_ref, o_ref,
                 oa_ref, oc_ref):
    d = x_ref.shape[-1]
    window_tile, cross_tile = _window_cross_tiles(
        za_ref, za_prev_ref, qc_ref, kvc_ref, bias0_ref, bias_ref, oa_ref, oc_ref)
    blocks_per_sub = SUB_ROWS // BLOCK

    def attend(s, st):
        _software_pipeline(
            [window_tile(j) for j in range(s * blocks_per_sub, (s + 1) * blocks_per_sub)]
            + [cross_tile(p, _sub_rows(s)) for p in range(MEM_HEADS // 2)],
            WINDOW_LOOKAHEAD)

    def norm_in(s, st):
        st["h"] = _rms(x_ref[_sub_rows(s), :], g_ref[2:3, :]).astype(_BF16)

    def branch(n, o_branch_ref, w_branch_ref):
        def run(s, st):
            gate = jax.nn.sigmoid(_dot(st["h"], wg_ref[:, n * d:(n + 1) * d])
                                  + bg_ref[n:n + 1, :])
            term = gate * _dot(o_branch_ref[_sub_rows(s), :], w_branch_ref[...])
            st["merged"] = term if n == 0 else st["merged"] + term
        return run

    def project(s, st):
        st["y"] = _dot(st["merged"].astype(_BF16), wo_ref[...])

    def norm_out(s, st):
        rows = _sub_rows(s)
        o_ref[rows, :] = x_ref[rows, :] + _rms(st["y"], g_ref[3:4, :])

    _staggered(ROW_TILE // SUB_ROWS,
               [attend, norm_in, branch(0, oa_ref, wa_ref), branch(1, ob_ref, wb_ref),
                branch(2, oc_ref, wc_ref), project, norm_out])


def _post(x2d, za, qc, kvc, bias_a, ob, gains, w_gate, b_gate, w_a, w_b, w_c, w_o,
          layer, batch, seq):
    t, d = x2d.shape
    tiles = seq // ROW_TILE
    mem_len = kvc.shape[0] // batch
    row = lambda b, i: (b * tiles + i, 0)
    bias_shape = (None,) + bias_a.shape[1:]
    return pl.pallas_call(
        _post_kernel,
        grid=(batch, tiles),
        in_specs=[
            pl.BlockSpec((ROW_TILE, d), row),
            pl.BlockSpec((ROW_TILE, A_COLS), row),
            pl.BlockSpec((BLOCK, 2 * A_KV), lambda b, i: (
                jnp.maximum((b * tiles + i) * (ROW_TILE // BLOCK) - 1, 0), 0)),
            pl.BlockSpec((ROW_TILE, C_W), row),
            pl.BlockSpec((mem_len, 2 * C_W), lambda b, i: (b, 0)),
            pl.BlockSpec(bias_shape, lambda b, i: (jnp.minimum(i, 1), 0, 0, 0)),
            pl.BlockSpec(bias_shape, lambda b, i: (1, 0, 0, 0)),
            pl.BlockSpec((ROW_TILE, B_W), row),
            _layer_resident(gains.shape, layer),
            _layer_resident(w_gate.shape, layer),
            _layer_resident(b_gate.shape, layer),
            _layer_resident(w_a.shape, layer),
            _layer_resident(w_b.shape, layer),
            _layer_resident(w_c.shape, layer),
            _layer_resident(w_o.shape, layer),
        ],
        out_specs=pl.BlockSpec((ROW_TILE, d), row),
        out_shape=jax.ShapeDtypeStruct((t, d), _F32),
        scratch_shapes=[pltpu.VMEM((ROW_TILE, A_Q), _BF16),
                        pltpu.VMEM((ROW_TILE, C_W), _BF16)],
        compiler_params=_params(2),
        name="attn_window_cross_out_proj",
    )(x2d, za, za, qc, kvc, bias_a, bias_a, ob, gains, w_gate, b_gate, w_a, w_b, w_c, w_o)


def _t5_bucket(dist):
    max_exact = N_BUCKETS // 2
    d = jnp.maximum(dist, 1).astype(_F32)
    large = max_exact + (jnp.log(d / max_exact) / math.log(MAX_DISTANCE / max_exact)
                         * (N_BUCKETS - max_exact)).astype(jnp.int32)
    large = jnp.minimum(large, N_BUCKETS - 1)
    return jnp.where(dist < max_exact, dist, large)


def _band_bias(table, head0, n_heads, rate, max_dist):
    row = jnp.arange(BLOCK)[:, None]
    col = jnp.arange(2 * BLOCK)[None, :]
    dist = row + BLOCK - col
    bucket = _t5_bucket(jnp.maximum(dist, 0) * rate)
    hit = bucket[None, None] == jnp.arange(N_BUCKETS)[None, :, None, None]
    vals = table.T[head0:head0 + n_heads, :, None, None]
    bias = jnp.sum(jnp.where(hit, vals, 0.0), axis=1) * LOG2_E
    valid = (dist >= 0) & (dist <= max_dist)
    later = jnp.where(valid[None], bias, MASKED)
    first = jnp.where((valid & (col >= BLOCK))[None], bias, MASKED)
    return jnp.stack([first, later])


def _permute_in_proj(w):
    hd = HEAD_DIM
    qa, ka, va = w[..., 0:A_Q], w[..., A_Q:A_Q + A_KV], w[..., A_Q + A_KV:A_Q + 2 * A_KV]
    off = A_Q + 2 * A_KV
    qb, kb, vb = (w[..., off + n * B_W:off + (n + 1) * B_W] for n in range(3))
    qc = w[..., off + 3 * B_W:]
    grp = lambda w3, g: w3[..., g * B_G:(g + 1) * B_G]
    cols = [ka, va] + [qa[..., h * hd:(h + 1) * hd] for h in A_PAIR_ORDER] + [qc]
    for g in range(DIL_GROUPS):
        cols += [grp(kb, g), grp(vb, g), grp(qb, g)]
    return jnp.concatenate(cols, axis=-1)


def kernel(x, mem, rel_bias, norm_gain, mem_norm_gain, w_ffn1_in, w_ffn1_out, w_in,
           sinks, w_mem_kv, w_gate, b_gate, w_br_a, w_br_b, w_br_c, w_o,
           w_ffn2_in, w_ffn2_out):
    batch, seq, d = x.shape
    depth = norm_gain.shape[0]
    assert seq % DIL_TILE == 0 and seq % ROW_TILE == 0 and (batch * seq) % PROJ_TILE == 0

    table = rel_bias.astype(_F32)
    bias_a = _band_bias(table, 0, SWA_HEADS, 1, SWA_WINDOW - 1)
    bias_b = jnp.stack([
        _band_bias(table, SWA_HEADS + g * DIL_HEADS_PER_GROUP, DIL_HEADS_PER_GROUP,
                   DIL_RATES[g], DIL_WINDOWS[g] // DIL_RATES[g])
        for g in range(DIL_GROUPS)], axis=1)

    bf = lambda w: w.astype(_BF16)
    w1_in, w1_out, w2_in, w2_out = bf(w_ffn1_in), bf(w_ffn1_out), bf(w_ffn2_in), bf(w_ffn2_out)
    w_in_p = _permute_in_proj(bf(w_in))
    w_kv, w_g, w_b, w_c, w_out = bf(w_mem_kv), bf(w_gate), bf(w_br_b), bf(w_br_c), bf(w_o)
    w_a = bf(jnp.concatenate(
        [w_br_a[:, h * HEAD_DIM:(h + 1) * HEAD_DIM] for h in A_PAIR_ORDER], axis=1))

    group0 = A_COLS + C_W
    q_cols = [(2 * A_KV, group0)]
    q_cols += [(group0 + g * B_COLS + 2 * B_G, group0 + (g + 1) * B_COLS)
               for g in range(DIL_GROUPS)]
    q_scale = jnp.ones((1, w_in.shape[-1]), _F32)
    for c0, c1 in q_cols:
        q_scale = q_scale.at[:, c0:c1].set(Q_LOG2_SCALE)

    x2d = x.reshape(batch * seq, d)
    mem2d = mem.reshape(-1, d)
    for l in range(depth):
        x2d = _ffn(x2d, norm_gain, w1_in, w1_out, l, 0)
        za, qc, *zb = _pre(x2d, norm_gain, w_in_p, q_scale, l)
        kvc = _mem_kv(mem2d, mem_norm_gain[:, None, :], w_kv, l)
        bias_a_l = jnp.where(jnp.arange(2 * BLOCK) == 0,
                             (sinks[l].astype(_F32) * LOG2_E)[None, :, None, None], bias_a)
        ob = _attn_dil(zb, bias_b, batch, seq)
        x2d = _post(x2d, za, qc, kvc, bias_a_l, ob, norm_gain, w_g, b_gate,
                    w_a, w_b, w_c, w_out, l, batch, seq)
        x2d = _ffn(x2d, norm_gain, w2_in, w2_out, l, 4)
    return x2d.reshape(batch, seq, d)
```

```python
import functools
import math

import jax
import jax.numpy as jnp
from jax import lax
from jax.experimental import pallas as pl
from jax.experimental.pallas import tpu as pltpu

HEAD_DIM = 64
SWA_HEADS = 6
SWA_KV_HEADS = 2
SWA_GROUP = SWA_HEADS // SWA_KV_HEADS
SWA_WINDOW = 128
DIL_WINDOWS = (128, 512, 2048)
DIL_RATES = (1, 4, 16)
DIL_GROUPS = 3
DIL_HEADS_PER_GROUP = 2
MEM_HEADS = 4
N_BUCKETS = 32
MAX_DISTANCE = 2048
N_BRANCH = 3
EPS = 1e-6
BLOCK = 128
LANES = 128

A_Q = SWA_HEADS * HEAD_DIM
A_KV = SWA_KV_HEADS * HEAD_DIM
B_G = DIL_HEADS_PER_GROUP * HEAD_DIM
B_W = DIL_GROUPS * B_G
C_W = MEM_HEADS * HEAD_DIM
A_COLS = A_Q + 2 * A_KV
B_COLS = 3 * B_G
A_PAIR_ORDER = tuple(h for p in range(SWA_GROUP) for h in (p, p + SWA_GROUP))

LOG2_E = math.log2(math.e)
Q_LOG2_SCALE = HEAD_DIM ** -0.5 * LOG2_E
MASKED = -1e30

ROW_TILE = 1024
PROJ_TILE = 2048
SUB_ROWS = 256
FFN_CHUNK = 512
DIL_TILE = BLOCK * max(DIL_RATES)
VMEM_LIMIT = 56 * 1024 * 1024

_F32 = jnp.float32
_BF16 = jnp.bfloat16


def _dot(a, b):
    return jnp.dot(a, b, preferred_element_type=_F32)


def _dot_nt(a, b):
    return lax.dot_general(a, b, (((1,), (1,)), ((), ())),
                           preferred_element_type=_F32)


def _rms(x, gain):
    ms = jnp.mean(x * x, axis=-1, keepdims=True)
    return x * lax.rsqrt(ms + EPS) * gain


def _resident(shape):
    return pl.BlockSpec(shape, lambda *_: (0,) * len(shape),
                        pipeline_mode=pl.Buffered(1))


def _layer_resident(shape, layer):
    zeros = (0,) * (len(shape) - 1)
    return pl.BlockSpec((None,) + tuple(shape[1:]), lambda *_: (layer,) + zeros,
                        pipeline_mode=pl.Buffered(1))


def _params(n_axes):
    return pltpu.CompilerParams(
        dimension_semantics=("arbitrary",) * n_axes,
        vmem_limit_bytes=VMEM_LIMIT)


def _staggered(n_sub, stages):
    state = [{} for _ in range(n_sub)]
    for t in range(n_sub + len(stages) - 1):
        for s in range(n_sub):
            if 0 <= t - s < len(stages):
                stages[t - s](s, state[s])


def _sub_rows(s):
    return slice(s * SUB_ROWS, (s + 1) * SUB_ROWS)


def _ffn_kernel(x_ref, g_ref, w_in_ref, w_out_ref, o_ref, *, ffn_dim, g0):
    def norm_in(s, st):
        st["h"] = _rms(x_ref[_sub_rows(s), :], g_ref[g0:g0 + 1, :]).astype(_BF16)

    bounds = [(c0, min(c0 + FFN_CHUNK, ffn_dim)) for c0 in range(0, ffn_dim, FFN_CHUNK)]

    def step(c):
        def run(s, st):
            prev = st.pop("ab", None)
            if c < len(bounds):
                c0, c1 = bounds[c]
                st["ab"] = (_dot(st["h"], w_in_ref[:, c0:c1]),
                            _dot(st["h"], w_in_ref[:, ffn_dim + c0:ffn_dim + c1]))
            if prev is not None:
                c0, c1 = bounds[c - 1]
                a, b = prev
                act = (a * jax.nn.sigmoid(a) * b).astype(_BF16)
                part = _dot(act, w_out_ref[c0:c1, :])
                st["y"] = part if "y" not in st else st["y"] + part
        return run

    def norm_out(s, st):
        rows = _sub_rows(s)
        o_ref[rows, :] = x_ref[rows, :] + 0.5 * _rms(st["y"], g_ref[g0 + 1:g0 + 2, :])

    _staggered(ROW_TILE // SUB_ROWS,
               [norm_in] + [step(c) for c in range(len(bounds) + 1)] + [norm_out])


def _ffn(x2d, gains, w_in, w_out, layer, g0):
    t, d = x2d.shape
    ffn_dim = w_out.shape[1]
    return pl.pallas_call(
        functools.partial(_ffn_kernel, ffn_dim=ffn_dim, g0=g0),
        grid=(t // ROW_TILE,),
        in_specs=[
            pl.BlockSpec((ROW_TILE, d), lambda i: (i, 0)),
            _layer_resident(gains.shape, layer),
            _layer_resident(w_in.shape, layer),
            _layer_resident(w_out.shape, layer),
        ],
        out_specs=pl.BlockSpec((ROW_TILE, d), lambda i: (i, 0)),
        out_shape=jax.ShapeDtypeStruct((t, d), _F32),
        compiler_params=_params(1),
        name="ffn",
    )(x2d, gains, w_in, w_out)


def _pre_kernel(x_ref, g_ref, w_ref, qs_ref, a_ref, c_ref, b0_ref, b1_ref, b2_ref, z_scr):
    def norm_in(s, st):
        st["h"] = _rms(x_ref[_sub_rows(s), :], g_ref[2:3, :]).astype(_BF16)

    first = (a_ref, c_ref, b0_ref)
    first_cols = sum(ref.shape[-1] for ref in first)

    def project_rows(s, st):
        z = _dot(st["h"], w_ref[:, 0:first_cols]) * qs_ref[:, 0:first_cols]
        col = 0
        for ref in first:
            ref[_sub_rows(s), :] = z[:, col:col + ref.shape[-1]].astype(_BF16)
            col += ref.shape[-1]

    def project_residues(s, st):
        z = _dot(st["h"], w_ref[:, first_cols:]) * qs_ref[:, first_cols:]
        n_slabs = z.shape[-1] // LANES
        for n in range(n_slabs):
            z_scr[s, n] = z[:, n * LANES:(n + 1) * LANES]
        for g, ref in ((1, b1_ref), (2, b2_ref)):
            rate = DIL_RATES[g]
            per = SUB_ROWS // rate
            for c in range(rate):
                for n in range(B_COLS // LANES):
                    slab = (g - 1) * (B_COLS // LANES) + n
                    ref[c, s * per:(s + 1) * per, n * LANES:(n + 1) * LANES] = (
                        z_scr[s, slab, pl.ds(c, per, stride=rate), :].astype(_BF16))

    _staggered(PROJ_TILE // SUB_ROWS, [norm_in, project_residues, project_rows])


def _pre(x2d, gains, w_perm, q_scale, layer):
    t, d = x2d.shape
    row = lambda i: (i, 0)
    return pl.pallas_call(
        _pre_kernel,
        grid=(t // PROJ_TILE,),
        in_specs=[
            pl.BlockSpec((PROJ_TILE, d), row),
            _layer_resident(gains.shape, layer),
            _layer_resident(w_perm.shape, layer),
            _resident(q_scale.shape),
        ],
        out_specs=[
            pl.BlockSpec((PROJ_TILE, A_COLS), row),
            pl.BlockSpec((PROJ_TILE, C_W), row),
            pl.BlockSpec((PROJ_TILE, B_COLS), row),
        ] + [pl.BlockSpec((r, PROJ_TILE // r, B_COLS), lambda i: (0, i, 0))
             for r in DIL_RATES[1:]],
        out_shape=[
            jax.ShapeDtypeStruct((t, A_COLS), _BF16),
            jax.ShapeDtypeStruct((t, C_W), _BF16),
            jax.ShapeDtypeStruct((t, B_COLS), _BF16),
        ] + [jax.ShapeDtypeStruct((r, t // r, B_COLS), _BF16) for r in DIL_RATES[1:]],
        scratch_shapes=[
            pltpu.VMEM((PROJ_TILE // SUB_ROWS, (DIL_GROUPS - 1) * B_COLS // LANES,
                        SUB_ROWS, LANES), _F32),
        ],
        compiler_params=_params(1),
        name="mix_in_proj",
    )(x2d, gains, w_perm, q_scale)


def _mem_kernel(m_ref, g_ref, w_ref, o_ref):
    h = _rms(m_ref[...], g_ref[...]).astype(_BF16)
    o_ref[...] = _dot(h, w_ref[...]).astype(_BF16)


def _mem_kv(mem2d, gains, w, layer):
    rows = mem2d.shape[0]
    return pl.pallas_call(
        _mem_kernel,
        grid=(1,),
        in_specs=[_resident(mem2d.shape),
                  _layer_resident(gains.shape, layer),
                  _layer_resident(w.shape, layer)],
        out_specs=pl.BlockSpec((rows, 2 * C_W), lambda i: (0, 0)),
        out_shape=jax.ShapeDtypeStruct((rows, 2 * C_W), _BF16),
        compiler_params=_params(1),
        name="mem_kv",
    )(mem2d, gains, w)


def _low_lanes():
    return lax.broadcasted_iota(jnp.int32, (1, LANES), 1) < HEAD_DIM


def _pair_scores(q, k, bias_lo=None, bias_hi=None):
    low = _low_lanes()
    zero = jnp.zeros((), k.dtype)
    if q.shape[0] <= k.shape[0]:
        s_lo = _dot_nt(jnp.where(low, q, zero), k)
        s_hi = _dot_nt(jnp.where(low, zero, q), k)
    else:
        s_lo = _dot_nt(q, jnp.where(low, k, zero))
        s_hi = _dot_nt(q, jnp.where(low, zero, k))
    if bias_lo is not None:
        s_lo = s_lo + bias_lo
        s_hi = s_hi + bias_hi
    return s_lo, s_hi


def _pair_softmax_pv(scores, v):
    s_lo, s_hi = scores
    low = _low_lanes()
    zero = jnp.zeros((), v.dtype)
    lane = lax.broadcasted_iota(jnp.int32, v.shape, 1)
    ones_lo = jnp.where(lane < HEAD_DIM, 1.0, 0.0).astype(v.dtype)
    ones_hi = jnp.where(lane < HEAD_DIM, 0.0, 1.0).astype(v.dtype)
    m_lo = jnp.max(s_lo, axis=-1, keepdims=True)
    m_hi = jnp.max(s_hi, axis=-1, keepdims=True)
    p_lo = jnp.exp2(s_lo - m_lo).astype(_BF16)
    p_hi = jnp.exp2(s_hi - m_hi).astype(_BF16)
    rhs = jnp.concatenate([
        jnp.concatenate([jnp.where(low, v, zero), ones_lo], axis=1),
        jnp.concatenate([jnp.where(low, zero, v), ones_hi], axis=1)],
        axis=0)
    res = _dot(jnp.concatenate([p_lo, p_hi], axis=1), rhs)
    return res[:, 0:LANES], res[:, LANES:2 * LANES], m_lo, m_hi


WINDOW_LOOKAHEAD = 2
DILATED_LOOKAHEAD = 2


def _software_pipeline(tiles, lookahead):
    pending = []
    for n, (scores_fn, _) in enumerate(tiles):
        pending.append(scores_fn())
        if n >= lookahead:
            done = n - lookahead
            tiles[done][1](pending[done])
            pending[done] = None
    for done in range(max(len(tiles) - lookahead, 0), len(tiles)):
        tiles[done][1](pending[done])


def _window_cross_tiles(cur_ref, prev_ref, qc_ref, kvc_ref, bias0_ref, bias_ref,
                        oa_ref, oc_ref):
    rows_a = SWA_GROUP * BLOCK
    sink_row = lax.broadcasted_iota(jnp.int32, (BLOCK, A_KV), 0) == 0

    def window_tile(j):
        rows = slice(j * BLOCK, (j + 1) * BLOCK)
        prev_rows = slice((j - 1) * BLOCK, j * BLOCK)
        bias = bias0_ref if j == 0 else bias_ref

        def kv_window(c0, c1):
            prev = prev_ref[:, c0:c1] if j == 0 else cur_ref[prev_rows, c0:c1]
            prev = jnp.where(sink_row, jnp.zeros((), prev.dtype), prev)
            return jnp.concatenate([prev, cur_ref[rows, c0:c1]], axis=0)

        def scores():
            q = jnp.concatenate(
                [cur_ref[rows, 2 * A_KV + p * LANES:2 * A_KV + (p + 1) * LANES]
                 for p in range(SWA_GROUP)], axis=0)
            return _pair_scores(q, kv_window(0, A_KV),
                                bias[0:SWA_GROUP].reshape(rows_a, 2 * BLOCK),
                                bias[SWA_GROUP:SWA_HEADS].reshape(rows_a, 2 * BLOCK))

        def finish(s):
            out, den, _, _ = _pair_softmax_pv(s, kv_window(A_KV, 2 * A_KV))
            o = out / den
            for p in range(SWA_GROUP):
                oa_ref[rows, p * LANES:(p + 1) * LANES] = (
                    o[p * BLOCK:(p + 1) * BLOCK]).astype(_BF16)

        return scores, finish

    def cross_tile(p, rows):
        cols = slice(p * LANES, (p + 1) * LANES)

        def scores():
            return _pair_scores(qc_ref[rows, cols], kvc_ref[:, cols])

        def finish(s):
            out, den, _, _ = _pair_softmax_pv(
                s, kvc_ref[:, C_W + p * LANES:C_W + (p + 1) * LANES])
            oc_ref[rows, cols] = (out / den).astype(_BF16)

        return scores, finish

    return window_tile, cross_tile


def _dil_kernel(*refs):
    cur_refs, prev_refs = refs[0:2 * DIL_GROUPS:2], refs[1:2 * DIL_GROUPS:2]
    bias0_ref, bias_ref, ob_ref, o_scr, d_scr, m_scr = refs[2 * DIL_GROUPS:]
    low = _low_lanes()
    k_cols, v_cols, q_cols = (slice(n * B_G, (n + 1) * B_G) for n in range(3))

    def tile(g, sb, c):
        rate = DIL_RATES[g]
        rows = slice(sb * BLOCK, (sb + 1) * BLOCK)
        bias = bias0_ref if sb == 0 else bias_ref
        if rate == 1:
            out_rows = pl.ds(sb * BLOCK, BLOCK)
            cur = lambda r, cols: cur_refs[g][r, cols]
            prev = lambda cols: prev_refs[g][:, cols]
        else:
            out_rows = pl.ds(sb * BLOCK * rate + c, BLOCK, stride=rate)
            cur = lambda r, cols: cur_refs[g][c, r, cols]
            prev = lambda cols: prev_refs[g][c, :, cols]

        def window(cols):
            before = prev(cols) if sb == 0 else cur(slice((sb - 1) * BLOCK, sb * BLOCK), cols)
            return jnp.concatenate([before, cur(rows, cols)], axis=0)

        def scores():
            return _pair_scores(cur(rows, q_cols), window(k_cols), bias[g, 0], bias[g, 1])

        def finish(s):
            out, den, m_lo, m_hi = _pair_softmax_pv(s, window(v_cols))
            o_scr[g, out_rows, :] = out
            d_scr[g, out_rows, :] = den
            m_scr[g, out_rows, :] = jnp.where(low, m_lo, m_hi)

        return scores, finish

    per_group = [[tile(g, sb, c)
                  for sb in range(DIL_TILE // (BLOCK * rate)) for c in range(rate)]
                 for g, rate in enumerate(DIL_RATES)]
    _software_pipeline([t for trio in zip(*per_group) for t in trio], DILATED_LOOKAHEAD)

    top = functools.reduce(jnp.maximum, [m_scr[g] for g in range(DIL_GROUPS)])
    scale = [jnp.exp2(m_scr[g] - top) for g in range(DIL_GROUPS)]
    total = sum(d_scr[g] * scale[g] for g in range(DIL_GROUPS))
    for g in range(DIL_GROUPS):
        ob_ref[:, g * B_G:(g + 1) * B_G] = (o_scr[g] * (scale[g] / total)).astype(_BF16)


def _attn_dil(zb, bias_b, batch, seq):
    t = zb[0].shape[0]
    tiles = seq // DIL_TILE
    row = lambda b, i: (b * tiles + i, 0)
    bias_shape = (None,) + bias_b.shape[1:]
    operands, specs = [], []
    for z, rate in zip(zb, DIL_RATES):
        rows = DIL_TILE // rate
        prev_block = lambda b, i, n=rows // BLOCK: jnp.maximum((b * tiles + i) * n - 1, 0)
        operands += [z, z]
        if rate == 1:
            specs += [pl.BlockSpec((rows, B_COLS), row),
                      pl.BlockSpec((BLOCK, B_COLS),
                                   lambda b, i, f=prev_block: (f(b, i), 0))]
        else:
            specs += [pl.BlockSpec((rate, rows, B_COLS), lambda b, i: (0, b * tiles + i, 0)),
                      pl.BlockSpec((rate, BLOCK, B_COLS),
                                   lambda b, i, f=prev_block: (0, f(b, i), 0))]
    return pl.pallas_call(
        _dil_kernel,
        grid=(batch, tiles),
        in_specs=specs + [
            pl.BlockSpec(bias_shape, lambda b, i: (jnp.minimum(i, 1), 0, 0, 0, 0)),
            pl.BlockSpec(bias_shape, lambda b, i: (1, 0, 0, 0, 0)),
        ],
        out_specs=pl.BlockSpec((DIL_TILE, B_W), row),
        out_shape=jax.ShapeDtypeStruct((t, B_W), _BF16),
        scratch_shapes=[
            pltpu.VMEM((DIL_GROUPS, DIL_TILE, B_G), _F32),
            pltpu.VMEM((DIL_GROUPS, DIL_TILE, B_G), _F32),
            pltpu.VMEM((DIL_GROUPS, DIL_TILE, B_G), _F32),
        ],
        compiler_params=_params(2),
        name="attn_dilated",
    )(*operands, bias_b, bias_b)


def _post_kernel(x_ref, za_ref, za_prev_ref, qc_ref, kvc_ref, bias0_ref, bias_ref, ob_ref,
                 g_ref, wg_ref, bg_ref, wa_ref, wb_ref, wc_ref, wo_ref, o_ref,
                 oa_ref, oc_ref):
    d = x_ref.shape[-1]
    window_tile, cross_tile = _window_cross_tiles(
        za_ref, za_prev_ref, qc_ref, kvc_ref, bias0_ref, bias_ref, oa_ref, oc_ref)
    blocks_per_sub = SUB_ROWS // BLOCK

    def attend(s, st):
        _software_pipeline(
            [window_tile(j) for j in range(s * blocks_per_sub, (s + 1) * blocks_per_sub)]
            + [cross_tile(p, _sub_rows(s)) for p in range(MEM_HEADS // 2)],
            WINDOW_LOOKAHEAD)

    def norm_in(s, st):
        st["h"] = _rms(x_ref[_sub_rows(s), :], g_ref[2:3, :]).astype(_BF16)

    def branch(n, o_branch_ref, w_branch_ref):
        def run(s, st):
            gate = jax.nn.sigmoid(_dot(st["h"], wg_ref[:, n * d:(n + 1) * d])
                                  + bg_ref[n:n + 1, :])
            term = gate * _dot(o_branch_ref[_sub_rows(s), :], w_branch_ref[...])
            st["merged"] = term if n == 0 else st["merged"] + term
        return run

    def project(s, st):
        st["y"] = _dot(st["merged"].astype(_BF16), wo_ref[...])

    def norm_out(s, st):
        rows = _sub_rows(s)
        o_ref[rows, :] = x_ref[rows, :] + _rms(st["y"], g_ref[3:4, :])

    _staggered(ROW_TILE // SUB_ROWS,
               [attend, norm_in, branch(0, oa_ref, wa_ref), branch(1, ob_ref, wb_ref),
                branch(2, oc_ref, wc_ref), project, norm_out])


def _post(x2d, za, qc, kvc, bias_a, ob, gains, w_gate, b_gate, w_a, w_b, w_c, w_o,
          layer, batch, seq):
    t, d = x2d.shape
    tiles = seq // ROW_TILE
    mem_len = kvc.shape[0] // batch
    row = lambda b, i: (b * tiles + i, 0)
    bias_shape = (None,) + bias_a.shape[1:]
    return pl.pallas_call(
        _post_kernel,
        grid=(batch, tiles),
        in_specs=[
            pl.BlockSpec((ROW_TILE, d), row),
            pl.BlockSpec((ROW_TILE, A_COLS), row),
            pl.BlockSpec((BLOCK, 2 * A_KV), lambda b, i: (
                jnp.maximum((b * tiles + i) * (ROW_TILE // BLOCK) - 1, 0), 0)),
            pl.BlockSpec((ROW_TILE, C_W), row),
            pl.BlockSpec((mem_len, 2 * C_W), lambda b, i: (b, 0)),
            pl.BlockSpec(bias_shape, lambda b, i: (jnp.minimum(i, 1), 0, 0, 0)),
            pl.BlockSpec(bias_shape, lambda b, i: (1, 0, 0, 0)),
            pl.BlockSpec((ROW_TILE, B_W), row),
            _layer_resident(gains.shape, layer),
            _layer_resident(w_gate.shape, layer),
            _layer_resident(b_gate.shape, layer),
            _layer_resident(w_a.shape, layer),
            _layer_resident(w_b.shape, layer),
            _layer_resident(w_c.shape, layer),
            _layer_resident(w_o.shape, layer),
        ],
        out_specs=pl.BlockSpec((ROW_TILE, d), row),
        out_shape=jax.ShapeDtypeStruct((t, d), _F32),
        scratch_shapes=[pltpu.VMEM((ROW_TILE, A_Q), _BF16),
                        pltpu.VMEM((ROW_TILE, C_W), _BF16)],
        compiler_params=_params(2),
        name="attn_window_cross_out_proj",
    )(x2d, za, za, qc, kvc, bias_a, bias_a, ob, gains, w_gate, b_gate, w_a, w_b, w_c, w_o)


def _t5_bucket(dist):
    max_exact = N_BUCKETS // 2
    d = jnp.maximum(dist, 1).astype(_F32)
    large = max_exact + (jnp.log(d / max_exact) / math.log(MAX_DISTANCE / max_exact)
                         * (N_BUCKETS - max_exact)).astype(jnp.int32)
    large = jnp.minimum(large, N_BUCKETS - 1)
    return jnp.where(dist < max_exact, dist, large)


def _band_bias(table, head0, n_heads, rate, max_dist):
    row = jnp.arange(BLOCK)[:, None]
    col = jnp.arange(2 * BLOCK)[None, :]
    dist = row + BLOCK - col
    bucket = _t5_bucket(jnp.maximum(dist, 0) * rate)
    hit = bucket[None, None] == jnp.arange(N_BUCKETS)[None, :, None, None]
    vals = table.T[head0:head0 + n_heads, :, None, None]
    bias = jnp.sum(jnp.where(hit, vals, 0.0), axis=1) * LOG2_E
    valid = (dist >= 0) & (dist <= max_dist)
    later = jnp.where(valid[None], bias, MASKED)
    first = jnp.where((valid & (col >= BLOCK))[None], bias, MASKED)
    return jnp.stack([first, later])


def _permute_in_proj(w):
    hd = HEAD_DIM
    qa, ka, va = w[..., 0:A_Q], w[..., A_Q:A_Q + A_KV], w[..., A_Q + A_KV:A_Q + 2 * A_KV]
    off = A_Q + 2 * A_KV
    qb, kb, vb = (w[..., off + n * B_W:off + (n + 1) * B_W] for n in range(3))
    qc = w[..., off + 3 * B_W:]
    grp = lambda w3, g: w3[..., g * B_G:(g + 1) * B_G]
    cols = [ka, va] + [qa[..., h * hd:(h + 1) * hd] for h in A_PAIR_ORDER] + [qc]
    for g in range(DIL_GROUPS):
        cols += [grp(kb, g), grp(vb, g), grp(qb, g)]
    return jnp.concatenate(cols, axis=-1)


def kernel(x, mem, rel_bias, norm_gain, mem_norm_gain, w_ffn1_in, w_ffn1_out, w_in,
           sinks, w_mem_kv, w_gate, b_gate, w_br_a, w_br_b, w_br_c, w_o,
           w_ffn2_in, w_ffn2_out):
    batch, seq, d = x.shape
    depth = norm_gain.shape[0]
    assert seq % DIL_TILE == 0 and seq % ROW_TILE == 0 and (batch * seq) % PROJ_TILE == 0

    table = rel_bias.astype(_F32)
    bias_a = _band_bias(table, 0, SWA_HEADS, 1, SWA_WINDOW - 1)
    bias_b = jnp.stack([
        _band_bias(table, SWA_HEADS + g * DIL_HEADS_PER_GROUP, DIL_HEADS_PER_GROUP,
                   DIL_RATES[g], DIL_WINDOWS[g] // DIL_RATES[g])
        for g in range(DIL_GROUPS)], axis=1)

    bf = lambda w: w.astype(_BF16)
    w1_in, w1_out, w2_in, w2_out = bf(w_ffn1_in), bf(w_ffn1_out), bf(w_ffn2_in), bf(w_ffn2_out)
    w_in_p = _permute_in_proj(bf(w_in))
    w_kv, w_g, w_b, w_c, w_out = bf(w_mem_kv), bf(w_gate), bf(w_br_b), bf(w_br_c), bf(w_o)
    w_a = bf(jnp.concatenate(
        [w_br_a[:, h * HEAD_DIM:(h + 1) * HEAD_DIM] for h in A_PAIR_ORDER], axis=1))

    group0 = A_COLS + C_W
    q_cols = [(2 * A_KV, group0)]
    q_cols += [(group0 + g * B_COLS + 2 * B_G, group0 + (g + 1) * B_COLS)
               for g in range(DIL_GROUPS)]
    q_scale = jnp.ones((1, w_in.shape[-1]), _F32)
    for c0, c1 in q_cols:
        q_scale = q_scale.at[:, c0:c1].set(Q_LOG2_SCALE)

    x2d = x.reshape(batch * seq, d)
    mem2d = mem.reshape(-1, d)
    for l in range(depth):
        x2d = _ffn(x2d, norm_gain, w1_in, w1_out, l, 0)
        za, qc, *zb = _pre(x2d, norm_gain, w_in_p, q_scale, l)
        kvc = _mem_kv(mem2d, mem_norm_gain[:, None, :], w_kv, l)
        bias_a_l = jnp.where(jnp.arange(2 * BLOCK) == 0,
                             (sinks[l].astype(_F32) * LOG2_E)[None, :, None, None], bias_a)
        ob = _attn_dil(zb, bias_b, batch, seq)
        x2d = _post(x2d, za, qc, kvc, bias_a_l, ob, norm_gain, w_g, b_gate,
                    w_a, w_b, w_c, w_out, l, batch, seq)
        x2d = _ffn(x2d, norm_gain, w2_in, w2_out, l, 4)
    return x2d.reshape(batch, seq, d)
```

```python
import functools
import math

import jax
import jax.numpy as jnp
from jax import lax
from jax.experimental import pallas as pl
from jax.experimental.pallas import tpu as pltpu

HEAD_DIM = 64
SWA_HEADS = 6
SWA_KV_HEADS = 2
SWA_GROUP = SWA_HEADS // SWA_KV_HEADS
SWA_WINDOW = 128
DIL_WINDOWS = (128, 512, 2048)
DIL_RATES = (1, 4, 16)
DIL_GROUPS = 3
DIL_HEADS_PER_GROUP = 2
MEM_HEADS = 4
N_BUCKETS = 32
MAX_DISTANCE = 2048
N_BRANCH = 3
EPS = 1e-6
BLOCK = 128
LANES = 128

A_Q = SWA_HEADS * HEAD_DIM
A_KV = SWA_KV_HEADS * HEAD_DIM
B_G = DIL_HEADS_PER_GROUP * HEAD_DIM
B_W = DIL_GROUPS * B_G
C_W = MEM_HEADS * HEAD_DIM
A_COLS = A_Q + 2 * A_KV
B_COLS = 3 * B_G
A_PAIR_ORDER = tuple(h for p in range(SWA_GROUP) for h in (p, p + SWA_GROUP))

LOG2_E = math.log2(math.e)
Q_LOG2_SCALE = HEAD_DIM ** -0.5 * LOG2_E
MASKED = -1e30

ROW_TILE = 1024
PROJ_TILE = 2048
SUB_ROWS = 256
FFN_CHUNK = 512
DIL_TILE = BLOCK * max(DIL_RATES)
VMEM_LIMIT = 56 * 1024 * 1024

_F32 = jnp.float32
_BF16 = jnp.bfloat16


def _dot(a, b):
    return jnp.dot(a, b, preferred_element_type=_F32)


def _dot_nt(a, b):
    return lax.dot_general(a, b, (((1,), (1,)), ((), ())),
                           preferred_element_type=_F32)


def _rms(x, gain):
    ms = jnp.mean(x * x, axis=-1, keepdims=True)
    return x * lax.rsqrt(ms + EPS) * gain


def _resident(shape):
    return pl.BlockSpec(shape, lambda *_: (0,) * len(shape),
                        pipeline_mode=pl.Buffered(1))


def _layer_resident(shape, layer):
    zeros = (0,) * (len(shape) - 1)
    return pl.BlockSpec((None,) + tuple(shape[1:]), lambda *_: (layer,) + zeros,
                        pipeline_mode=pl.Buffered(1))


def _params(n_axes):
    return pltpu.CompilerParams(
        dimension_semantics=("arbitrary",) * n_axes,
        vmem_limit_bytes=VMEM_LIMIT)


def _staggered(n_sub, stages):
    state = [{} for _ in range(n_sub)]
    for t in range(n_sub + len(stages) - 1):
        for s in range(n_sub):
            if 0 <= t - s < len(stages):
                stages[t - s](s, state[s])


def _sub_rows(s):
    return slice(s * SUB_ROWS, (s + 1) * SUB_ROWS)


def _ffn_kernel(x_ref, g_ref, w_in_ref, w_out_ref, o_ref, *, ffn_dim, g0):
    def norm_in(s, st):
        st["h"] = _rms(x_ref[_sub_rows(s), :], g_ref[g0:g0 + 1, :]).astype(_BF16)

    bounds = [(c0, min(c0 + FFN_CHUNK, ffn_dim)) for c0 in range(0, ffn_dim, FFN_CHUNK)]

    def step(c):
        def run(s, st):
            prev = st.pop("ab", None)
            if c < len(bounds):
                c0, c1 = bounds[c]
                st["ab"] = (_dot(st["h"], w_in_ref[:, c0:c1]),
                            _dot(st["h"], w_in_ref[:, ffn_dim + c0:ffn_dim + c1]))
            if prev is not None:
                c0, c1 = bounds[c - 1]
                a, b = prev
                act = (a * jax.nn.sigmoid(a) * b).astype(_BF16)
                part = _dot(act, w_out_ref[c0:c1, :])
                st["y"] = part if "y" not in st else st["y"] + part
        return run

    def norm_out(s, st):
        rows = _sub_rows(s)
        o_ref[rows, :] = x_ref[rows, :] + 0.5 * _rms(st["y"], g_ref[g0 + 1:g0 + 2, :])

    _staggered(ROW_TILE // SUB_ROWS,
               [norm_in] + [step(c) for c in range(len(bounds) + 1)] + [norm_out])


def _ffn(x2d, gains, w_in, w_out, layer, g0):
    t, d = x2d.shape
    ffn_dim = w_out.shape[1]
    return pl.pallas_call(
        functools.partial(_ffn_kernel, ffn_dim=ffn_dim, g0=g0),
        grid=(t // ROW_TILE,),
        in_specs=[
            pl.BlockSpec((ROW_TILE, d), lambda i: (i, 0)),
            _layer_resident(gains.shape, layer),
            _layer_resident(w_in.shape, layer),
            _layer_resident(w_out.shape, layer),
        ],
        out_specs=pl.BlockSpec((ROW_TILE, d), lambda i: (i, 0)),
        out_shape=jax.ShapeDtypeStruct((t, d), _F32),
        compiler_params=_params(1),
        name="ffn",
    )(x2d, gains, w_in, w_out)


def _pre_kernel(x_ref, g_ref, w_ref, qs_ref, a_ref, c_ref, b0_ref, b1_ref, b2_ref, z_scr):
    def norm_in(s, st):
        st["h"] = _rms(x_ref[_sub_rows(s), :], g_ref[2:3, :]).astype(_BF16)

    first = (a_ref, c_ref, b0_ref)
    first_cols = sum(ref.shape[-1] for ref in first)

    def project_rows(s, st):
        z = _dot(st["h"], w_ref[:, 0:first_cols]) * qs_ref[:, 0:first_cols]
        col = 0
        for ref in first:
            ref[_sub_rows(s), :] = z[:, col:col + ref.shape[-1]].astype(_BF16)
            col += ref.shape[-1]

    def project_residues(s, st):
        z = _dot(st["h"], w_ref[:, first_cols:]) * qs_ref[:, first_cols:]
        n_slabs = z.shape[-1] // LANES
        for n in range(n_slabs):
            z_scr[s, n] = z[:, n * LANES:(n + 1) * LANES]
        for g, ref in ((1, b1_ref), (2, b2_ref)):
            rate = DIL_RATES[g]
            per = SUB_ROWS // rate
            for c in range(rate):
                for n in range(B_COLS // LANES):
                    slab = (g - 1) * (B_COLS // LANES) + n
                    ref[c, s * per:(s + 1) * per, n * LANES:(n + 1) * LANES] = (
                        z_scr[s, slab, pl.ds(c, per, stride=rate), :].astype(_BF16))

    _staggered(PROJ_TILE // SUB_ROWS, [norm_in, project_residues, project_rows])


def _pre(x2d, gains, w_perm, q_scale, layer):
    t, d = x2d.shape
    row = lambda i: (i, 0)
    return pl.pallas_call(
        _pre_kernel,
        grid=(t // PROJ_TILE,),
        in_specs=[
            pl.BlockSpec((PROJ_TILE, d), row),
            _layer_resident(gains.shape, layer),
            _layer_resident(w_perm.shape, layer),
            _resident(q_scale.shape),
        ],
        out_specs=[
            pl.BlockSpec((PROJ_TILE, A_COLS), row),
            pl.BlockSpec((PROJ_TILE, C_W), row),
            pl.BlockSpec((PROJ_TILE, B_COLS), row),
        ] + [pl.BlockSpec((r, PROJ_TILE // r, B_COLS), lambda i: (0, i, 0))
             for r in DIL_RATES[1:]],
        out_shape=[
            jax.ShapeDtypeStruct((t, A_COLS), _BF16),
            jax.ShapeDtypeStruct((t, C_W), _BF16),
            jax.ShapeDtypeStruct((t, B_COLS), _BF16),
        ] + [jax.ShapeDtypeStruct((r, t // r, B_COLS), _BF16) for r in DIL_RATES[1:]],
        scratch_shapes=[
            pltpu.VMEM((PROJ_TILE // SUB_ROWS, (DIL_GROUPS - 1) * B_COLS // LANES,
                        SUB_ROWS, LANES), _F32),
        ],
        compiler_params=_params(1),
        name="mix_in_proj",
    )(x2d, gains, w_perm, q_scale)


def _mem_kernel(m_ref, g_ref, w_ref, o_ref):
    h = _rms(m_ref[...], g_ref[...]).astype(_BF16)
    o_ref[...] = _dot(h, w_ref[...]).astype(_BF16)


def _mem_kv(mem2d, gains, w, layer):
    rows = mem2d.shape[0]
    return pl.pallas_call(
        _mem_kernel,
        grid=(1,),
        in_specs=[_resident(mem2d.shape),
                  _layer_resident(gains.shape, layer),
                  _layer_resident(w.shape, layer)],
        out_specs=pl.BlockSpec((rows, 2 * C_W), lambda i: (0, 0)),
        out_shape=jax.ShapeDtypeStruct((rows, 2 * C_W), _BF16),
        compiler_params=_params(1),
        name="mem_kv",
    )(mem2d, gains, w)


def _low_lanes():
    return lax.broadcasted_iota(jnp.int32, (1, LANES), 1) < HEAD_DIM


def _pair_scores(q, k, bias_lo=None, bias_hi=None):
    low = _low_lanes()
    zero = jnp.zeros((), k.dtype)
    if q.shape[0] <= k.shape[0]:
        s_lo = _dot_nt(jnp.where(low, q, zero), k)
        s_hi = _dot_nt(jnp.where(low, zero, q), k)
    else:
        s_lo = _dot_nt(q, jnp.where(low, k, zero))
        s_hi = _dot_nt(q, jnp.where(low, zero, k))
    if bias_lo is not None:
        s_lo = s_lo + bias_lo
        s_hi = s_hi + bias_hi
    return s_lo, s_hi


def _pair_softmax_pv(scores, v):
    s_lo, s_hi = scores
    low = _low_lanes()
    zero = jnp.zeros((), v.dtype)
    lane = lax.broadcasted_iota(jnp.int32, v.shape, 1)
    ones_lo = jnp.where(lane < HEAD_DIM, 1.0, 0.0).astype(v.dtype)
    ones_hi = jnp.where(lane < HEAD_DIM, 0.0, 1.0).astype(v.dtype)
    m_lo = jnp.max(s_lo, axis=-1, keepdims=True)
    m_hi = jnp.max(s_hi, axis=-1, keepdims=True)
    p_lo = jnp.exp2(s_lo - m_lo).astype(_BF16)
    p_hi = jnp.exp2(s_hi - m_hi).astype(_BF16)
    rhs = jnp.concatenate([
        jnp.concatenate([jnp.where(low, v, zero), ones_lo], axis=1),
        jnp.concatenate([jnp.where(low, zero, v), ones_hi], axis=1)],
        axis=0)
    res = _dot(jnp.concatenate([p_lo, p_hi], axis=1), rhs)
    return res[:, 0:LANES], res[:, LANES:2 * LANES], m_lo, m_hi


WINDOW_LOOKAHEAD = 1
DILATED_LOOKAHEAD = 2


def _software_pipeline(tiles, lookahead):
    pending = []
    for n, (scores_fn, _) in enumerate(tiles):
        pending.append(scores_fn())
        if n >= lookahead:
            done = n - lookahead
            tiles[done][1](pending[done])
            pending[done] = None
    for done in range(max(len(tiles) - lookahead, 0), len(tiles)):
        tiles[done][1](pending[done])


def _window_cross_tiles(cur_ref, prev_ref, qc_ref, kvc_ref, bias0_ref, bias_ref,
                        oa_ref, oc_ref):
    rows_a = SWA_GROUP * BLOCK
    sink_row = lax.broadcasted_iota(jnp.int32, (BLOCK, A_KV), 0) == 0

    def window_tile(j):
        rows = slice(j * BLOCK, (j + 1) * BLOCK)
        prev_rows = slice((j - 1) * BLOCK, j * BLOCK)
        bias = bias0_ref if j == 0 else bias_ref

        def kv_window(c0, c1):
            prev = prev_ref[:, c0:c1] if j == 0 else cur_ref[prev_rows, c0:c1]
            prev = jnp.where(sink_row, jnp.zeros((), prev.dtype), prev)
            return jnp.concatenate([prev, cur_ref[rows, c0:c1]], axis=0)

        def scores():
            q = jnp.concatenate(
                [cur_ref[rows, 2 * A_KV + p * LANES:2 * A_KV + (p + 1) * LANES]
                 for p in range(SWA_GROUP)], axis=0)
            return _pair_scores(q, kv_window(0, A_KV),
                                bias[0:SWA_GROUP].reshape(rows_a, 2 * BLOCK),
                                bias[SWA_GROUP:SWA_HEADS].reshape(rows_a, 2 * BLOCK))

        def finish(s):
            out, den, _, _ = _pair_softmax_pv(s, kv_window(A_KV, 2 * A_KV))
            o = out / den
            for p in range(SWA_GROUP):
                oa_ref[rows, p * LANES:(p + 1) * LANES] = (
                    o[p * BLOCK:(p + 1) * BLOCK]).astype(_BF16)

        return scores, finish

    def cross_tile(p, rows):
        cols = slice(p * LANES, (p + 1) * LANES)

        def scores():
            return _pair_scores(qc_ref[rows, cols], kvc_ref[:, cols])

        def finish(s):
            out, den, _, _ = _pair_softmax_pv(
                s, kvc_ref[:, C_W + p * LANES:C_W + (p + 1) * LANES])
            oc_ref[rows, cols] = (out / den).astype(_BF16)

        return scores, finish

    return window_tile, cross_tile


def _dil_kernel(*refs):
    cur_refs, prev_refs = refs[0:2 * DIL_GROUPS:2], refs[1:2 * DIL_GROUPS:2]
    bias0_ref, bias_ref, ob_ref, o_scr, d_scr, m_scr = refs[2 * DIL_GROUPS:]
    low = _low_lanes()
    k_cols, v_cols, q_cols = (slice(n * B_G, (n + 1) * B_G) for n in range(3))

    def tile(g, sb, c):
        rate = DIL_RATES[g]
        rows = slice(sb * BLOCK, (sb + 1) * BLOCK)
        bias = bias0_ref if sb == 0 else bias_ref
        if rate == 1:
            out_rows = pl.ds(sb * BLOCK, BLOCK)
            cur = lambda r, cols: cur_refs[g][r, cols]
            prev = lambda cols: prev_refs[g][:, cols]
        else:
            out_rows = pl.ds(sb * BLOCK * rate + c, BLOCK, stride=rate)
            cur = lambda r, cols: cur_refs[g][c, r, cols]
            prev = lambda cols: prev_refs[g][c, :, cols]

        def window(cols):
            before = prev(cols) if sb == 0 else cur(slice((sb - 1) * BLOCK, sb * BLOCK), cols)
            return jnp.concatenate([before, cur(rows, cols)], axis=0)

        def scores():
            return _pair_scores(cur(rows, q_cols), window(k_cols), bias[g, 0], bias[g, 1])

        def finish(s):
            out, den, m_lo, m_hi = _pair_softmax_pv(s, window(v_cols))
            o_scr[g, out_rows, :] = out
            d_scr[g, out_rows, :] = den
            m_scr[g, out_rows, :] = jnp.where(low, m_lo, m_hi)

        return scores, finish

    per_group = [[tile(g, sb, c)
                  for sb in range(DIL_TILE // (BLOCK * rate)) for c in range(rate)]
                 for g, rate in enumerate(DIL_RATES)]
    _software_pipeline([t for trio in zip(*per_group) for t in trio], DILATED_LOOKAHEAD)

    top = functools.reduce(jnp.maximum, [m_scr[g] for g in range(DIL_GROUPS)])
    scale = [jnp.exp2(m_scr[g] - top) for g in range(DIL_GROUPS)]
    total = sum(d_scr[g] * scale[g] for g in range(DIL_GROUPS))
    for g in range(DIL_GROUPS):
        ob_ref[:, g * B_G:(g + 1) * B_G] = (o_scr[g] * (scale[g] / total)).astype(_BF16)


def _attn_dil(zb, bias_b, batch, seq):
    t = zb[0].shape[0]
    tiles = seq // DIL_TILE
    row = lambda b, i: (b * tiles + i, 0)
    bias_shape = (None,) + bias_b.shape[1:]
    operands, specs = [], []
    for z, rate in zip(zb, DIL_RATES):
        rows = DIL_TILE // rate
        prev_block = lambda b, i, n=rows // BLOCK: jnp.maximum((b * tiles + i) * n - 1, 0)
        operands += [z, z]
        if rate == 1:
            specs += [pl.BlockSpec((rows, B_COLS), row),
                      pl.BlockSpec((BLOCK, B_COLS),
                                   lambda b, i, f=prev_block: (f(b, i), 0))]
        else:
            specs += [pl.BlockSpec((rate, rows, B_COLS), lambda b, i: (0, b * tiles + i, 0)),
                      pl.BlockSpec((rate, BLOCK, B_COLS),
                                   lambda b, i, f=prev_block: (0, f(b, i), 0))]
    return pl.pallas_call(
        _dil_kernel,
        grid=(batch, tiles),
        in_specs=specs + [
            pl.BlockSpec(bias_shape, lambda b, i: (jnp.minimum(i, 1), 0, 0, 0, 0)),
            pl.BlockSpec(bias_shape, lambda b, i: (1, 0, 0, 0, 0)),
        ],
        out_specs=pl.BlockSpec((DIL_TILE, B_W), row),
        out_shape=jax.ShapeDtypeStruct((t, B_W), _BF16),
        scratch_shapes=[
            pltpu.VMEM((DIL_GROUPS, DIL_TILE, B_G), _F32),
            pltpu.VMEM((DIL_GROUPS, DIL_TILE, B_G), _F32),
            pltpu.VMEM((DIL_GROUPS, DIL_TILE, B_G), _F32),
        ],
        compiler_params=_params(2),
        name="attn_dilated",
    )(*operands, bias_b, bias_b)


def _post_kernel(x_ref, za_ref, za_prev_ref, qc_ref, kvc_ref, bias0_ref, bias_ref, ob_ref,
                 g_ref, wg_ref, bg_ref, wa_ref, wb_ref, wc_ref, wo_ref, o_ref,
                 oa_ref, oc_ref):
    d = x_ref.shape[-1]
    window_tile, cross_tile = _window_cross_tiles(
        za_ref, za_prev_ref, qc_ref, kvc_ref, bias0_ref, bias_ref, oa_ref, oc_ref)
    blocks_per_sub = SUB_ROWS // BLOCK

    def attend(s, st):
        _software_pipeline(
            [window_tile(j) for j in range(s * blocks_per_sub, (s + 1) * blocks_per_sub)]
            + [cross_tile(p, _sub_rows(s)) for p in range(MEM_HEADS // 2)],
            WINDOW_LOOKAHEAD)

    def norm_in(s, st):
        st["h"] = _rms(x_ref[_sub_rows(s), :], g_ref[2:3, :]).astype(_BF16)

    def branch(n, o_branch_ref, w_branch_ref):
        def run(s, st):
            gate = jax.nn.sigmoid(_dot(st["h"], wg_ref[:, n * d:(n + 1) * d])
                                  + bg_ref[n:n + 1, :])
            term = gate * _dot(o_branch_ref[_sub_rows(s), :], w_branch_ref[...])
            st["merged"] = term if n == 0 else st["merged"] + term
        return run

    def project(s, st):
        st["y"] = _dot(st["merged"].astype(_BF16), wo_ref[...])

    def norm_out(s, st):
        rows = _sub_rows(s)
        o_ref[rows, :] = x_ref[rows, :] + _rms(st["y"], g_ref[3:4, :])

    _staggered(ROW_TILE // SUB_ROWS,
               [attend, norm_in, branch(0, oa_ref, wa_ref), branch(1, ob_ref, wb_ref),
                branch(2, oc_ref, wc_ref), project, norm_out])


def _post(x2d, za, qc, kvc, bias_a, ob, gains, w_gate, b_gate, w_a, w_b, w_c, w_o,
          layer, batch, seq):
    t, d = x2d.shape
    tiles = seq // ROW_TILE
    mem_len = kvc.shape[0] // batch
    row = lambda b, i: (b * tiles + i, 0)
    bias_shape = (None,) + bias_a.shape[1:]
    return pl.pallas_call(
        _post_kernel,
        grid=(batch, tiles),
        in_specs=[
            pl.BlockSpec((ROW_TILE, d), row),
            pl.BlockSpec((ROW_TILE, A_COLS), row),
            pl.BlockSpec((BLOCK, 2 * A_KV), lambda b, i: (
                jnp.maximum((b * tiles + i) * (ROW_TILE // BLOCK) - 1, 0), 0)),
            pl.BlockSpec((ROW_TILE, C_W), row),
            pl.BlockSpec((mem_len, 2 * C_W), lambda b, i: (b, 0)),
            pl.BlockSpec(bias_shape, lambda b, i: (jnp.minimum(i, 1), 0, 0, 0)),
            pl.BlockSpec(bias_shape, lambda b, i: (1, 0, 0, 0)),
            pl.BlockSpec((ROW_TILE, B_W), row),
            _layer_resident(gains.shape, layer),
            _layer_resident(w_gate.shape, layer),
            _layer_resident(b_gate.shape, layer),
            _layer_resident(w_a.shape, layer),
            _layer_resident(w_b.shape, layer),
            _layer_resident(w_c.shape, layer),
            _layer_resident(w_o.shape, layer),
        ],
        out_specs=pl.BlockSpec((ROW_TILE, d), row),
        out_shape=jax.ShapeDtypeStruct((t, d), _F32),
        scratch_shapes=[pltpu.VMEM((ROW_TILE, A_Q), _BF16),
                        pltpu.VMEM((ROW_TILE, C_W), _BF16)],
        compiler_params=_params(2),
        name="attn_window_cross_out_proj",
    )(x2d, za, za, qc, kvc, bias_a, bias_a, ob, gains, w_gate, b_gate, w_a, w_b, w_c, w_o)


def _t5_bucket(dist):
    max_exact = N_BUCKETS // 2
    d = jnp.maximum(dist, 1).astype(_F32)
    large = max_exact + (jnp.log(d / max_exact) / math.log(MAX_DISTANCE / max_exact)
                         * (N_BUCKETS - max_exact)).astype(jnp.int32)
    large = jnp.minimum(large, N_BUCKETS - 1)
    return jnp.where(dist < max_exact, dist, large)


def _band_bias(table, head0, n_heads, rate, max_dist):
    row = jnp.arange(BLOCK)[:, None]
    col = jnp.arange(2 * BLOCK)[None, :]
    dist = row + BLOCK - col
    bucket = _t5_bucket(jnp.maximum(dist, 0) * rate)
    hit = bucket[None, None] == jnp.arange(N_BUCKETS)[None, :, None, None]
    vals = table.T[head0:head0 + n_heads, :, None, None]
    bias = jnp.sum(jnp.where(hit, vals, 0.0), axis=1) * LOG2_E
    valid = (dist >= 0) & (dist <= max_dist)
    later = jnp.where(valid[None], bias, MASKED)
    first = jnp.where((valid & (col >= BLOCK))[None], bias, MASKED)
    return jnp.stack([first, later])


def _permute_in_proj(w):
    hd = HEAD_DIM
    qa, ka, va = w[..., 0:A_Q], w[..., A_Q:A_Q + A_KV], w[..., A_Q + A_KV:A_Q + 2 * A_KV]
    off = A_Q + 2 * A_KV
    qb, kb, vb = (w[..., off + n * B_W:off + (n + 1) * B_W] for n in range(3))
    qc = w[..., off + 3 * B_W:]
    grp = lambda w3, g: w3[..., g * B_G:(g + 1) * B_G]
    cols = [ka, va] + [qa[..., h * hd:(h + 1) * hd] for h in A_PAIR_ORDER] + [qc]
    for g in range(DIL_GROUPS):
        cols += [grp(kb, g), grp(vb, g), grp(qb, g)]
    return jnp.concatenate(cols, axis=-1)


def kernel(x, mem, rel_bias, norm_gain, mem_norm_gain, w_ffn1_in, w_ffn1_out, w_in,
           sinks, w_mem_kv, w_gate, b_gate, w_br_a, w_br_b, w_br_c, w_o,
           w_ffn2_in, w_ffn2_out):
    batch, seq, d = x.shape
    depth = norm_gain.shape[0]
    assert seq % DIL_TILE == 0 and seq % ROW_TILE == 0 and (batch * seq) % PROJ_TILE == 0

    table = rel_bias.astype(_F32)
    bias_a = _band_bias(table, 0, SWA_HEADS, 1, SWA_WINDOW - 1)
    bias_b = jnp.stack([
        _band_bias(table, SWA_HEADS + g * DIL_HEADS_PER_GROUP, DIL_HEADS_PER_GROUP,
                   DIL_RATES[g], DIL_WINDOWS[g] // DIL_RATES[g])
        for g in range(DIL_GROUPS)], axis=1)

    bf = lambda w: w.astype(_BF16)
    w1_in, w1_out, w2_in, w2_out = bf(w_ffn1_in), bf(w_ffn1_out), bf(w_ffn2_in), bf(w_ffn2_out)
    w_in_p = _permute_in_proj(bf(w_in))
    w_kv, w_g, w_b, w_c, w_out = bf(w_mem_kv), bf(w_gate), bf(w_br_b), bf(w_br_c), bf(w_o)
    w_a = bf(jnp.concatenate(
        [w_br_a[:, h * HEAD_DIM:(h + 1) * HEAD_DIM] for h in A_PAIR_ORDER], axis=1))

    group0 = A_COLS + C_W
    q_cols = [(2 * A_KV, group0)]
    q_cols += [(group0 + g * B_COLS + 2 * B_G, group0 + (g + 1) * B_COLS)
               for g in range(DIL_GROUPS)]
    q_scale = jnp.ones((1, w_in.shape[-1]), _F32)
    for c0, c1 in q_cols:
        q_scale = q_scale.at[:, c0:c1].set(Q_LOG2_SCALE)

    x2d = x.reshape(batch * seq, d)
    mem2d = mem.reshape(-1, d)
    for l in range(depth):
        x2d = _ffn(x2d, norm_gain, w1_in, w1_out, l, 0)
        za, qc, *zb = _pre(x2d, norm_gain, w_in_p, q_scale, l)
        kvc = _mem_kv(mem2d, mem_norm_gain[:, None, :], w_kv, l)
        bias_a_l = jnp.where(jnp.arange(2 * BLOCK) == 0,
                             (sinks[l].astype(_F32) * LOG2_E)[None, :, None, None], bias_a)
        ob = _attn_dil(zb, bias_b, batch, seq)
        x2d = _post(x2d, za, qc, kvc, bias_a_l, ob, norm_gain, w_g, b_gate,
                    w_a, w_b, w_c, w_out, l, batch, seq)
        x2d = _ffn(x2d, norm_gain, w2_in, w2_out, l, 4)
    return x2d.reshape(batch, seq, d)
```

```python
import functools
import math

import jax
import jax.numpy as jnp
from jax import lax
from jax.experimental import pallas as pl
from jax.experimental.pallas import tpu as pltpu

HEAD_DIM = 64
SWA_HEADS = 6
SWA_KV_HEADS = 2
SWA_GROUP = SWA_HEADS // SWA_KV_HEADS
SWA_WINDOW = 128
DIL_WINDOWS = (128, 512, 2048)
DIL_RATES = (1, 4, 16)
DIL_GROUPS = 3
DIL_HEADS_PER_GROUP = 2
MEM_HEADS = 4
N_BUCKETS = 32
MAX_DISTANCE = 2048
N_BRANCH = 3
EPS = 1e-6
BLOCK = 128
LANES = 128

A_Q = SWA_HEADS * HEAD_DIM
A_KV = SWA_KV_HEADS * HEAD_DIM
B_G = DIL_HEADS_PER_GROUP * HEAD_DIM
B_W = DIL_GROUPS * B_G
C_W = MEM_HEADS * HEAD_DIM
A_COLS = A_Q + 2 * A_KV
B_COLS = 3 * B_G
A_PAIR_ORDER = tuple(h for p in range(SWA_GROUP) for h in (p, p + SWA_GROUP))

LOG2_E = math.log2(math.e)
Q_LOG2_SCALE = HEAD_DIM ** -0.5 * LOG2_E
MASKED = -1e30

ROW_TILE = 1024
PROJ_TILE = 2048
SUB_ROWS = 256
FFN_CHUNK = 512
DIL_TILE = BLOCK * max(DIL_RATES)
VMEM_LIMIT = 56 * 1024 * 1024

_F32 = jnp.float32
_BF16 = jnp.bfloat16


def _dot(a, b):
    return jnp.dot(a, b, preferred_element_type=_F32)


def _dot_nt(a, b):
    return lax.dot_general(a, b, (((1,), (1,)), ((), ())),
                           preferred_element_type=_F32)


def _rms(x, gain):
    ms = jnp.mean(x * x, axis=-1, keepdims=True)
    return x * lax.rsqrt(ms + EPS) * gain


def _resident(shape):
    return pl.BlockSpec(shape, lambda *_: (0,) * len(shape),
                        pipeline_mode=pl.Buffered(1))


def _layer_resident(shape, layer):
    zeros = (0,) * (len(shape) - 1)
    return pl.BlockSpec((None,) + tuple(shape[1:]), lambda *_: (layer,) + zeros,
                        pipeline_mode=pl.Buffered(1))


def _params(n_axes):
    return pltpu.CompilerParams(
        dimension_semantics=("arbitrary",) * n_axes,
        vmem_limit_bytes=VMEM_LIMIT)


def _staggered(n_sub, stages):
    state = [{} for _ in range(n_sub)]
    for t in range(n_sub + len(stages) - 1):
        for s in range(n_sub):
            if 0 <= t - s < len(stages):
                stages[t - s](s, state[s])


def _sub_rows(s):
    return slice(s * SUB_ROWS, (s + 1) * SUB_ROWS)


def _ffn_kernel(x_ref, g_ref, w_in_ref, w_out_ref, o_ref, *, ffn_dim, g0):
    def norm_in(s, st):
        st["h"] = _rms(x_ref[_sub_rows(s), :], g_ref[g0:g0 + 1, :]).astype(_BF16)

    bounds = [(c0, min(c0 + FFN_CHUNK, ffn_dim)) for c0 in range(0, ffn_dim, FFN_CHUNK)]

    def step(c):
        def run(s, st):
            prev = st.pop("ab", None)
            if c < len(bounds):
                c0, c1 = bounds[c]
                st["ab"] = (_dot(st["h"], w_in_ref[:, c0:c1]),
                            _dot(st["h"], w_in_ref[:, ffn_dim + c0:ffn_dim + c1]))
            if prev is not None:
                c0, c1 = bounds[c - 1]
                a, b = prev
                act = (a * jax.nn.sigmoid(a) * b).astype(_BF16)
                part = _dot(act, w_out_ref[c0:c1, :])
                st["y"] = part if "y" not in st else st["y"] + part
        return run

    def norm_out(s, st):
        rows = _sub_rows(s)
        o_ref[rows, :] = x_ref[rows, :] + 0.5 * _rms(st["y"], g_ref[g0 + 1:g0 + 2, :])

    _staggered(ROW_TILE // SUB_ROWS,
               [norm_in] + [step(c) for c in range(len(bounds) + 1)] + [norm_out])


def _ffn(x2d, gains, w_in, w_out, layer, g0):
    t, d = x2d.shape
    ffn_dim = w_out.shape[1]
    return pl.pallas_call(
        functools.partial(_ffn_kernel, ffn_dim=ffn_dim, g0=g0),
        grid=(t // ROW_TILE,),
        in_specs=[
            pl.BlockSpec((ROW_TILE, d), lambda i: (i, 0)),
            _layer_resident(gains.shape, layer),
            _layer_resident(w_in.shape, layer),
            _layer_resident(w_out.shape, layer),
        ],
        out_specs=pl.BlockSpec((ROW_TILE, d), lambda i: (i, 0)),
        out_shape=jax.ShapeDtypeStruct((t, d), _F32),
        compiler_params=_params(1),
        name="ffn",
    )(x2d, gains, w_in, w_out)


def _pre_kernel(x_ref, g_ref, w_ref, qs_ref, a_ref, c_ref, b0_ref, b1_ref, b2_ref, z_scr):
    def norm_in(s, st):
        st["h"] = _rms(x_ref[_sub_rows(s), :], g_ref[2:3, :]).astype(_BF16)

    first = (a_ref, c_ref, b0_ref)
    first_cols = sum(ref.shape[-1] for ref in first)

    def project_rows(s, st):
        z = _dot(st["h"], w_ref[:, 0:first_cols]) * qs_ref[:, 0:first_cols]
        col = 0
        for ref in first:
            ref[_sub_rows(s), :] = z[:, col:col + ref.shape[-1]].astype(_BF16)
            col += ref.shape[-1]

    def project_residues(s, st):
        z = _dot(st["h"], w_ref[:, first_cols:]) * qs_ref[:, first_cols:]
        n_slabs = z.shape[-1] // LANES
        for n in range(n_slabs):
            z_scr[s, n] = z[:, n * LANES:(n + 1) * LANES]
        for g, ref in ((1, b1_ref), (2, b2_ref)):
            rate = DIL_RATES[g]
            per = SUB_ROWS // rate
            for c in range(rate):
                for n in range(B_COLS // LANES):
                    slab = (g - 1) * (B_COLS // LANES) + n
                    ref[c, s * per:(s + 1) * per, n * LANES:(n + 1) * LANES] = (
                        z_scr[s, slab, pl.ds(c, per, stride=rate), :].astype(_BF16))

    _staggered(PROJ_TILE // SUB_ROWS, [norm_in, project_residues, project_rows])


def _pre(x2d, gains, w_perm, q_scale, layer):
    t, d = x2d.shape
    row = lambda i: (i, 0)
    return pl.pallas_call(
        _pre_kernel,
        grid=(t // PROJ_TILE,),
        in_specs=[
            pl.BlockSpec((PROJ_TILE, d), row),
            _layer_resident(gains.shape, layer),
            _layer_resident(w_perm.shape, layer),
            _resident(q_scale.shape),
        ],
        out_specs=[
            pl.BlockSpec((PROJ_TILE, A_COLS), row),
            pl.BlockSpec((PROJ_TILE, C_W), row),
            pl.BlockSpec((PROJ_TILE, B_COLS), row),
        ] + [pl.BlockSpec((r, PROJ_TILE // r, B_COLS), lambda i: (0, i, 0))
             for r in DIL_RATES[1:]],
        out_shape=[
            jax.ShapeDtypeStruct((t, A_COLS), _BF16),
            jax.ShapeDtypeStruct((t, C_W), _BF16),
            jax.ShapeDtypeStruct((t, B_COLS), _BF16),
        ] + [jax.ShapeDtypeStruct((r, t // r, B_COLS), _BF16) for r in DIL_RATES[1:]],
        scratch_shapes=[
            pltpu.VMEM((PROJ_TILE // SUB_ROWS, (DIL_GROUPS - 1) * B_COLS // LANES,
                        SUB_ROWS, LANES), _F32),
        ],
        compiler_params=_params(1),
        name="mix_in_proj",
    )(x2d, gains, w_perm, q_scale)


def _mem_kernel(m_ref, g_ref, w_ref, o_ref):
    h = _rms(m_ref[...], g_ref[...]).astype(_BF16)
    o_ref[...] = _dot(h, w_ref[...]).astype(_BF16)


def _mem_kv(mem2d, gains, w, layer):
    rows = mem2d.shape[0]
    return pl.pallas_call(
        _mem_kernel,
        grid=(1,),
        in_specs=[_resident(mem2d.shape),
                  _layer_resident(gains.shape, layer),
                  _layer_resident(w.shape, layer)],
        out_specs=pl.BlockSpec((rows, 2 * C_W), lambda i: (0, 0)),
        out_shape=jax.ShapeDtypeStruct((rows, 2 * C_W), _BF16),
        compiler_params=_params(1),
        name="mem_kv",
    )(mem2d, gains, w)


def _low_lanes():
    return lax.broadcasted_iota(jnp.int32, (1, LANES), 1) < HEAD_DIM


def _pair_scores(q, k, bias_lo=None, bias_hi=None):
    low = _low_lanes()
    zero = jnp.zeros((), k.dtype)
    if q.shape[0] <= k.shape[0]:
        s_lo = _dot_nt(jnp.where(low, q, zero), k)
        s_hi = _dot_nt(jnp.where(low, zero, q), k)
    else:
        s_lo = _dot_nt(q, jnp.where(low, k, zero))
        s_hi = _dot_nt(q, jnp.where(low, zero, k))
    if bias_lo is not None:
        s_lo = s_lo + bias_lo
        s_hi = s_hi + bias_hi
    return s_lo, s_hi


def _pair_softmax_pv(scores, v):
    s_lo, s_hi = scores
    low = _low_lanes()
    zero = jnp.zeros((), v.dtype)
    lane = lax.broadcasted_iota(jnp.int32, v.shape, 1)
    ones_lo = jnp.where(lane < HEAD_DIM, 1.0, 0.0).astype(v.dtype)
    ones_hi = jnp.where(lane < HEAD_DIM, 0.0, 1.0).astype(v.dtype)
    m_lo = jnp.max(s_lo, axis=-1, keepdims=True)
    m_hi = jnp.max(s_hi, axis=-1, keepdims=True)
    p_lo = jnp.exp2(s_lo - m_lo).astype(_BF16)
    p_hi = jnp.exp2(s_hi - m_hi).astype(_BF16)
    rhs = jnp.concatenate([
        jnp.concatenate([jnp.where(low, v, zero), ones_lo], axis=1),
        jnp.concatenate([jnp.where(low, zero, v), ones_hi], axis=1)],
        axis=0)
    res = _dot(jnp.concatenate([p_lo, p_hi], axis=1), rhs)
    return res[:, 0:LANES], res[:, LANES:2 * LANES], m_lo, m_hi


WINDOW_LOOKAHEAD = 1
DILATED_LOOKAHEAD = 1


def _software_pipeline(tiles, lookahead):
    pending = []
    for n, (scores_fn, _) in enumerate(tiles):
        pending.append(scores_fn())
        if n >= lookahead:
            done = n - lookahead
            tiles[done][1](pending[done])
            pending[done] = None
    for done in range(max(len(tiles) - lookahead, 0), len(tiles)):
        tiles[done][1](pending[done])


def _window_cross_tiles(cur_ref, prev_ref, qc_ref, kvc_ref, bias0_ref, bias_ref,
                        oa_ref, oc_ref):
    rows_a = SWA_GROUP * BLOCK
    sink_row = lax.broadcasted_iota(jnp.int32, (BLOCK, A_KV), 0) == 0

    def window_tile(j):
        rows = slice(j * BLOCK, (j + 1) * BLOCK)
        prev_rows = slice((j - 1) * BLOCK, j * BLOCK)
        bias = bias0_ref if j == 0 else bias_ref

        def kv_window(c0, c1):
            prev = prev_ref[:, c0:c1] if j == 0 else cur_ref[prev_rows, c0:c1]
            prev = jnp.where(sink_row, jnp.zeros((), prev.dtype), prev)
            return jnp.concatenate([prev, cur_ref[rows, c0:c1]], axis=0)

        def scores():
            q = jnp.concatenate(
                [cur_ref[rows, 2 * A_KV + p * LANES:2 * A_KV + (p + 1) * LANES]
                 for p in range(SWA_GROUP)], axis=0)
            return _pair_scores(q, kv_window(0, A_KV),
                                bias[0:SWA_GROUP].reshape(rows_a, 2 * BLOCK),
                                bias[SWA_GROUP:SWA_HEADS].reshape(rows_a, 2 * BLOCK))

        def finish(s):
            out, den, _, _ = _pair_softmax_pv(s, kv_window(A_KV, 2 * A_KV))
            o = out / den
            for p in range(SWA_GROUP):
                oa_ref[rows, p * LANES:(p + 1) * LANES] = (
                    o[p * BLOCK:(p + 1) * BLOCK]).astype(_BF16)

        return scores, finish

    def cross_tile(p, rows):
        cols = slice(p * LANES, (p + 1) * LANES)

        def scores():
            return _pair_scores(qc_ref[rows, cols], kvc_ref[:, cols])

        def finish(s):
            out, den, _, _ = _pair_softmax_pv(
                s, kvc_ref[:, C_W + p * LANES:C_W + (p + 1) * LANES])
            oc_ref[rows, cols] = (out / den).astype(_BF16)

        return scores, finish

    return window_tile, cross_tile


def _dil_kernel(*refs):
    cur_refs, prev_refs = refs[0:2 * DIL_GROUPS:2], refs[1:2 * DIL_GROUPS:2]
    bias0_ref, bias_ref, ob_ref, o_scr, d_scr, m_scr = refs[2 * DIL_GROUPS:]
    low = _low_lanes()
    k_cols, v_cols, q_cols = (slice(n * B_G, (n + 1) * B_G) for n in range(3))

    def tile(g, sb, c):
        rate = DIL_RATES[g]
        rows = slice(sb * BLOCK, (sb + 1) * BLOCK)
        bias = bias0_ref if sb == 0 else bias_ref
        if rate == 1:
            out_rows = pl.ds(sb * BLOCK, BLOCK)
            cur = lambda r, cols: cur_refs[g][r, cols]
            prev = lambda cols: prev_refs[g][:, cols]
        else:
            out_rows = pl.ds(sb * BLOCK * rate + c, BLOCK, stride=rate)
            cur = lambda r, cols: cur_refs[g][c, r, cols]
            prev = lambda cols: prev_refs[g][c, :, cols]

        def window(cols):
            before = prev(cols) if sb == 0 else cur(slice((sb - 1) * BLOCK, sb * BLOCK), cols)
            return jnp.concatenate([before, cur(rows, cols)], axis=0)

        def scores():
            return _pair_scores(cur(rows, q_cols), window(k_cols), bias[g, 0], bias[g, 1])

        def finish(s):
            out, den, m_lo, m_hi = _pair_softmax_pv(s, window(v_cols))
            o_scr[g, out_rows, :] = out
            d_scr[g, out_rows, :] = den
            m_scr[g, out_rows, :] = jnp.where(low, m_lo, m_hi)

        return scores, finish

    per_group = [[tile(g, sb, c)
                  for sb in range(DIL_TILE // (BLOCK * rate)) for c in range(rate)]
                 for g, rate in enumerate(DIL_RATES)]
    _software_pipeline([t for trio in zip(*per_group) for t in trio], DILATED_LOOKAHEAD)

    top = functools.reduce(jnp.maximum, [m_scr[g] for g in range(DIL_GROUPS)])
    scale = [jnp.exp2(m_scr[g] - top) for g in range(DIL_GROUPS)]
    total = sum(d_scr[g] * scale[g] for g in range(DIL_GROUPS))
    for g in range(DIL_GROUPS):
        ob_ref[:, g * B_G:(g + 1) * B_G] = (o_scr[g] * (scale[g] / total)).astype(_BF16)


def _attn_dil(zb, bias_b, batch, seq):
    t = zb[0].shape[0]
    tiles = seq // DIL_TILE
    row = lambda b, i: (b * tiles + i, 0)
    bias_shape = (None,) + bias_b.shape[1:]
    operands, specs = [], []
    for z, rate in zip(zb, DIL_RATES):
        rows = DIL_TILE // rate
        prev_block = lambda b, i, n=rows // BLOCK: jnp.maximum((b * tiles + i) * n - 1, 0)
        operands += [z, z]
        if rate == 1:
            specs += [pl.BlockSpec((rows, B_COLS), row),
                      pl.BlockSpec((BLOCK, B_COLS),
                                   lambda b, i, f=prev_block: (f(b, i), 0))]
        else:
            specs += [pl.BlockSpec((rate, rows, B_COLS), lambda b, i: (0, b * tiles + i, 0)),
                      pl.BlockSpec((rate, BLOCK, B_COLS),
                                   lambda b, i, f=prev_block: (0, f(b, i), 0))]
    return pl.pallas_call(
        _dil_kernel,
        grid=(batch, tiles),
        in_specs=specs + [
            pl.BlockSpec(bias_shape, lambda b, i: (jnp.minimum(i, 1), 0, 0, 0, 0)),
            pl.BlockSpec(bias_shape, lambda b, i: (1, 0, 0, 0, 0)),
        ],
        out_specs=pl.BlockSpec((DIL_TILE, B_W), row),
        out_shape=jax.ShapeDtypeStruct((t, B_W), _BF16),
        scratch_shapes=[
            pltpu.VMEM((DIL_GROUPS, DIL_TILE, B_G), _F32),
            pltpu.VMEM((DIL_GROUPS, DIL_TILE, B_G), _F32),
            pltpu.VMEM((DIL_GROUPS, DIL_TILE, B_G), _F32),
        ],
        compiler_params=_params(2),
        name="attn_dilated",
    )(*operands, bias_b, bias_b)


def _post_kernel(x_ref, za_ref, za_prev_ref, qc_ref, kvc_ref, bias0_ref, bias_ref, ob_ref,
                 g_ref, wg_ref, bg_ref, wa_ref, wb_ref, wc_ref, wo_ref, o_ref,
                 oa_ref, oc_ref):
    d = x_ref.shape[-1]
    window_tile, cross_tile = _window_cross_tiles(
        za_ref, za_prev_ref, qc_ref, kvc_ref, bias0_ref, bias_ref, oa_ref, oc_ref)
    blocks_per_sub = SUB_ROWS // BLOCK

    def attend(s, st):
        _software_pipeline(
            [window_tile(j) for j in range(s * blocks_per_sub, (s + 1) * blocks_per_sub)]
            + [cross_tile(p, _sub_rows(s)) for p in range(MEM_HEADS // 2)],
            WINDOW_LOOKAHEAD)

    def norm_in(s, st):
        st["h"] = _rms(x_ref[_sub_rows(s), :], g_ref[2:3, :]).astype(_BF16)

    def branch(n, o_branch_ref, w_branch_ref):
        def run(s, st):
            gate = jax.nn.sigmoid(_dot(st["h"], wg_ref[:, n * d:(n + 1) * d])
                                  + bg_ref[n:n + 1, :])
            term = gate * _dot(o_branch_ref[_sub_rows(s), :], w_branch_ref[...])
            st["merged"] = term if n == 0 else st["merged"] + term
        return run

    def project(s, st):
        st["y"] = _dot(st["merged"].astype(_BF16), wo_ref[...])

    def norm_out(s, st):
        rows = _sub_rows(s)
        o_ref[rows, :] = x_ref[rows, :] + _rms(st["y"], g_ref[3:4, :])

    _staggered(ROW_TILE // SUB_ROWS,
               [attend, norm_in, branch(0, oa_ref, wa_ref), branch(1, ob_ref, wb_ref),
                branch(2, oc_ref, wc_ref), project, norm_out])


def _post(x2d, za, qc, kvc, bias_a, ob, gains, w_gate, b_gate, w_a, w_b, w_c, w_o,
          layer, batch, seq):
    t, d = x2d.shape
    tiles = seq // ROW_TILE
    mem_len = kvc.shape[0] // batch
    row = lambda b, i: (b * tiles + i, 0)
    bias_shape = (None,) + bias_a.shape[1:]
    return pl.pallas_call(
        _post_kernel,
        grid=(batch, tiles),
        in_specs=[
            pl.BlockSpec((ROW_TILE, d), row),
            pl.BlockSpec((ROW_TILE, A_COLS), row),
            pl.BlockSpec((BLOCK, 2 * A_KV), lambda b, i: (
                jnp.maximum((b * tiles + i) * (ROW_TILE // BLOCK) - 1, 0), 0)),
            pl.BlockSpec((ROW_TILE, C_W), row),
            pl.BlockSpec((mem_len, 2 * C_W), lambda b, i: (b, 0)),
            pl.BlockSpec(bias_shape, lambda b, i: (jnp.minimum(i, 1), 0, 0, 0)),
            pl.BlockSpec(bias_shape, lambda b, i: (1, 0, 0, 0)),
            pl.BlockSpec((ROW_TILE, B_W), row),
            _layer_resident(gains.shape, layer),
            _layer_resident(w_gate.shape, layer),
            _layer_resident(b_gate.shape, layer),
            _layer_resident(w_a.shape, layer),
            _layer_resident(w_b.shape, layer),
            _layer_resident(w_c.shape, layer),
            _layer_resident(w_o.shape, layer),
        ],
        out_specs=pl.BlockSpec((ROW_TILE, d), row),
        out_shape=jax.ShapeDtypeStruct((t, d), _F32),
        scratch_shapes=[pltpu.VMEM((ROW_TILE, A_Q), _BF16),
                        pltpu.VMEM((ROW_TILE, C_W), _BF16)],
        compiler_params=_params(2),
        name="attn_window_cross_out_proj",
    )(x2d, za, za, qc, kvc, bias_a, bias_a, ob, gains, w_gate, b_gate, w_a, w_b, w_c, w_o)


def _t5_bucket(dist):
    max_exact = N_BUCKETS // 2
    d = jnp.maximum(dist, 1).astype(_F32)
    large = max_exact + (jnp.log(d / max_exact) / math.log(MAX_DISTANCE / max_exact)
                         * (N_BUCKETS - max_exact)).astype(jnp.int32)
    large = jnp.minimum(large, N_BUCKETS - 1)
    return jnp.where(dist < max_exact, dist, large)


def _band_bias(table, head0, n_heads, rate, max_dist):
    row = jnp.arange(BLOCK)[:, None]
    col = jnp.arange(2 * BLOCK)[None, :]
    dist = row + BLOCK - col
    bucket = _t5_bucket(jnp.maximum(dist, 0) * rate)
    hit = bucket[None, None] == jnp.arange(N_BUCKETS)[None, :, None, None]
    vals = table.T[head0:head0 + n_heads, :, None, None]
    bias = jnp.sum(jnp.where(hit, vals, 0.0), axis=1) * LOG2_E
    valid = (dist >= 0) & (dist <= max_dist)
    later = jnp.where(valid[None], bias, MASKED)
    first = jnp.where((valid & (col >= BLOCK))[None], bias, MASKED)
    return jnp.stack([first, later])


def _permute_in_proj(w):
    hd = HEAD_DIM
    qa, ka, va = w[..., 0:A_Q], w[..., A_Q:A_Q + A_KV], w[..., A_Q + A_KV:A_Q + 2 * A_KV]
    off = A_Q + 2 * A_KV
    qb, kb, vb = (w[..., off + n * B_W:off + (n + 1) * B_W] for n in range(3))
    qc = w[..., off + 3 * B_W:]
    grp = lambda w3, g: w3[..., g * B_G:(g + 1) * B_G]
    cols = [ka, va] + [qa[..., h * hd:(h + 1) * hd] for h in A_PAIR_ORDER] + [qc]
    for g in range(DIL_GROUPS):
        cols += [grp(kb, g), grp(vb, g), grp(qb, g)]
    return jnp.concatenate(cols, axis=-1)


def kernel(x, mem, rel_bias, norm_gain, mem_norm_gain, w_ffn1_in, w_ffn1_out, w_in,
           sinks, w_mem_kv, w_gate, b_gate, w_br_a, w_br_b, w_br_c, w_o,
           w_ffn2_in, w_ffn2_out):
    batch, seq, d = x.shape
    depth = norm_gain.shape[0]
    assert seq % DIL_TILE == 0 and seq % ROW_TILE == 0 and (batch * seq) % PROJ_TILE == 0

    table = rel_bias.astype(_F32)
    bias_a = _band_bias(table, 0, SWA_HEADS, 1, SWA_WINDOW - 1)
    bias_b = jnp.stack([
        _band_bias(table, SWA_HEADS + g * DIL_HEADS_PER_GROUP, DIL_HEADS_PER_GROUP,
                   DIL_RATES[g], DIL_WINDOWS[g] // DIL_RATES[g])
        for g in range(DIL_GROUPS)], axis=1)

    bf = lambda w: w.astype(_BF16)
    w1_in, w1_out, w2_in, w2_out = bf(w_ffn1_in), bf(w_ffn1_out), bf(w_ffn2_in), bf(w_ffn2_out)
    w_in_p = _permute_in_proj(bf(w_in))
    w_kv, w_g, w_b, w_c, w_out = bf(w_mem_kv), bf(w_gate), bf(w_br_b), bf(w_br_c), bf(w_o)
    w_a = bf(jnp.concatenate(
        [w_br_a[:, h * HEAD_DIM:(h + 1) * HEAD_DIM] for h in A_PAIR_ORDER], axis=1))

    group0 = A_COLS + C_W
    q_cols = [(2 * A_KV, group0)]
    q_cols += [(group0 + g * B_COLS + 2 * B_G, group0 + (g + 1) * B_COLS)
               for g in range(DIL_GROUPS)]
    q_scale = jnp.ones((1, w_in.shape[-1]), _F32)
    for c0, c1 in q_cols:
        q_scale = q_scale.at[:, c0:c1].set(Q_LOG2_SCALE)

    x2d = x.reshape(batch * seq, d)
    mem2d = mem.reshape(-1, d)
    for l in range(depth):
        x2d = _ffn(x2d, norm_gain, w1_in, w1_out, l, 0)
        za, qc, *zb = _pre(x2d, norm_gain, w_in_p, q_scale, l)
        kvc = _mem_kv(mem2d, mem_norm_gain[:, None, :], w_kv, l)
        bias_a_l = jnp.where(jnp.arange(2 * BLOCK) == 0,
                             (sinks[l].astype(_F32) * LOG2_E)[None, :, None, None], bias_a)
        ob = _attn_dil(zb, bias_b, batch, seq)
        x2d = _post(x2d, za, qc, kvc, bias_a_l, ob, norm_gain, w_g, b_gate,
                    w_a, w_b, w_c, w_out, l, batch, seq)
        x2d = _ffn(x2d, norm_gain, w2_in, w2_out, l, 4)
    return x2d.reshape(batch, seq, d)
```
